```python
import jax, jax.numpy as jnp
from jax import lax
import numpy as np

D_MODEL = 2048
BATCH = 2
SEQ = 8192
DEPTH = 2
DEC_BATCH = 4
DEC_SEQ = 8192
PAST_LEN = 128

GRID_W = 64
Q_BLOCK = 128
D_FF = 5632
NORM_EPS = 1e-6
ROPE_BASE = 10000.0
A_HEADS = 8
A_KV_HEADS = 2
A_HEAD_DIM = 128
B_HEADS = 8
B_Q_LORA = 512
B_KV_LORA = 256
B_NOPE_DIM = 128
B_ROPE_DIM = 64
B_V_DIM = 128
POOL_WINDOWS = (2, 4, 8, 16)
POOL_GROUP = D_MODEL // 4
IN_SIZES = (A_HEADS * A_HEAD_DIM, A_KV_HEADS * A_HEAD_DIM, A_KV_HEADS * A_HEAD_DIM, B_Q_LORA, B_KV_LORA, B_ROPE_DIM)
IN_COLS = 1024 + 256 + 256 + 512 + 256 + 64
MIX_WIDTH = A_HEADS * A_HEAD_DIM + B_HEADS * B_V_DIM

kernel_name = 'hybrid_gqa_mla_pool_macaron_encoder'


def rms_norm(x, g):
    xf = x.astype(jnp.float32)
    y = xf * lax.rsqrt(jnp.mean(xf * xf, axis=-1, keepdims=True) + NORM_EPS)
    return (y * g.astype(jnp.float32)).astype(x.dtype)


def swiglu(x, w_gate, w_up, w_down):
    return (jax.nn.silu(x @ w_gate) * (x @ w_up)) @ w_down


def axial_rope_angles(seq_len, rot_dim):
    rows = seq_len // GRID_W
    row = jnp.repeat(jnp.arange(rows, dtype=jnp.float32), GRID_W)
    col = jnp.tile(jnp.arange(GRID_W, dtype=jnp.float32), rows)
    half = rot_dim // 2
    freqs = ROPE_BASE ** (-jnp.arange(0, half, 2, dtype=jnp.float32) / half)
    ang = jnp.concatenate([row[:, None] * freqs, col[:, None] * freqs], axis=-1)
    return jnp.cos(ang), jnp.sin(ang)


def apply_rope(x, cos, sin):
    xp = x.astype(jnp.float32).reshape(*x.shape[:-1], x.shape[-1] // 2, 2)
    x0, x1 = xp[..., 0], xp[..., 1]
    c = cos[None, :, None, :]
    s = sin[None, :, None, :]
    out = jnp.stack([x0 * c - x1 * s, x0 * s + x1 * c], axis=-1)
    return out.reshape(x.shape).astype(x.dtype)


def blocked_attention(q, k, v, scale):
    b, s, h, dq = q.shape
    hkv = k.shape[2]
    g = h // hkv
    dv = v.shape[-1]
    nb = s // Q_BLOCK
    qb = q.reshape(b, nb, Q_BLOCK, hkv, g, dq).transpose(1, 0, 2, 3, 4, 5)

    def one_block(qblk):
        sc = jnp.einsum('bqhgd,bkhd->bhgqk', qblk, k, preferred_element_type=jnp.float32) * scale
        p = jax.nn.softmax(sc, axis=-1).astype(v.dtype)
        return jnp.einsum('bhgqk,bkhe->bqhge', p, v)

    o = lax.map(one_block, qb)
    return o.transpose(1, 0, 2, 3, 4, 5).reshape(b, s, h * dv)


def parallel_attention_mixer(h, w_in, a_q_norm_g, a_k_norm_g, b_cq_norm_g, b_w_uq, b_ckv_norm_g, b_w_ukv, w_out, rope_a, rope_b):
    b, s, _ = h.shape
    offs = [int(v) for v in np.cumsum(IN_SIZES)[:-1]]
    qa, ka, va, cq, ckv, kr = jnp.split(h @ w_in, offs, axis=-1)
    cos_a, sin_a = rope_a
    qa = apply_rope(rms_norm(qa.reshape(b, s, A_HEADS, A_HEAD_DIM), a_q_norm_g), cos_a, sin_a)
    ka = apply_rope(rms_norm(ka.reshape(b, s, A_KV_HEADS, A_HEAD_DIM), a_k_norm_g), cos_a, sin_a)
    va = va.reshape(b, s, A_KV_HEADS, A_HEAD_DIM)
    oa = blocked_attention(qa, ka, va, A_HEAD_DIM ** -0.5)
    cos_b, sin_b = rope_b
    qb = (rms_norm(cq, b_cq_norm_g) @ b_w_uq).reshape(b, s, B_HEADS, B_NOPE_DIM + B_ROPE_DIM)
    q_nope, q_rope = qb[..., :B_NOPE_DIM], qb[..., B_NOPE_DIM:]
    q_rope = apply_rope(q_rope, cos_b, sin_b)
    kv = (rms_norm(ckv, b_ckv_norm_g) @ b_w_ukv).reshape(b, s, B_HEADS, B_NOPE_DIM + B_V_DIM)
    k_nope, vb = kv[..., :B_NOPE_DIM], kv[..., B_NOPE_DIM:]
    k_rope = apply_rope(kr[:, :, None, :], cos_b, sin_b)
    q_full = jnp.concatenate([q_nope, q_rope], axis=-1)
    k_full = jnp.concatenate([k_nope, jnp.broadcast_to(k_rope, (b, s, B_HEADS, B_ROPE_DIM))], axis=-1)
    ob = blocked_attention(q_full, k_full, vb, (B_NOPE_DIM + B_ROPE_DIM) ** -0.5)
    return jnp.concatenate([oa, ob], axis=-1) @ w_out


def multiscale_pool_mixer(h, pool_w, pool_scale):
    b, s, _ = h.shape
    t = jnp.arange(s)
    outs = []
    for gi, w in enumerate(POOL_WINDOWS):
        xf = h[..., gi * POOL_GROUP:(gi + 1) * POOL_GROUP].astype(jnp.float32)
        cs = jnp.concatenate([jnp.zeros((b, 1, POOL_GROUP), jnp.float32), jnp.cumsum(xf, axis=1)], axis=1)
        lo = jnp.clip(t - w // 2, 0, s)
        hi = jnp.clip(t + w // 2, 0, s)
        mean = (cs[:, hi] - cs[:, lo]) / (hi - lo).astype(jnp.float32)[None, :, None]
        pooled = (mean - xf).astype(h.dtype)
        outs.append(pooled @ pool_w[gi])
    return jnp.concatenate(outs, axis=-1) * pool_scale


def ffn_half(x, pre_g, w_gate, w_up, w_down, post_g):
    return x + 0.5 * rms_norm(swiglu(rms_norm(x, pre_g), w_gate, w_up, w_down), post_g)


def run_trunk(x, layers):
    s = x.shape[1]
    rope_a = axial_rope_angles(s, A_HEAD_DIM)
    rope_b = axial_rope_angles(s, B_ROPE_DIM)
    for i in range(DEPTH):
        p = layers[i]
        x = ffn_half(x, *p['ffn1'])
        pre_g, *mix_w, post_g = p['mix']
        hn = rms_norm(x, pre_g)
        if i % 2 == 0:
            m = parallel_attention_mixer(hn, *mix_w, rope_a, rope_b)
        else:
            m = multiscale_pool_mixer(hn, *mix_w)
        x = x + rms_norm(m, post_g)
        x = ffn_half(x, *p['ffn2'])
    return x


def setup_inputs(seed: int = 0) -> dict:
    key = jax.random.key(seed)
    ki = iter(jax.random.split(key, 64))

    def dense(shape, fan_in):
        return jax.random.normal(next(ki), shape, jnp.float32) * fan_in ** -0.5

    def gain(n):
        return 1.0 + 0.05 * jax.random.normal(next(ki), (n,), jnp.float32)

    def ffn(prefix):
        return {prefix + '_pre_g': gain(D_MODEL),
                prefix + '_w_gate': dense((D_MODEL, D_FF), D_MODEL),
                prefix + '_w_up': dense((D_MODEL, D_FF), D_MODEL),
                prefix + '_w_down': dense((D_FF, D_MODEL), D_FF),
                prefix + '_post_g': gain(D_MODEL)}

    d = {}
    d['x_prompt'] = jax.random.normal(next(ki), (BATCH, SEQ, D_MODEL), jnp.float32)
    d['x_sample'] = jax.random.normal(next(ki), (DEC_BATCH, DEC_SEQ, D_MODEL), jnp.float32)
    d.update(ffn('l0_ffn1'))
    d['l0_mix_pre_g'] = gain(D_MODEL)
    d['l0_w_in'] = dense((D_MODEL, IN_COLS), D_MODEL)
    d['l0_a_q_norm_g'] = gain(A_HEAD_DIM)
    d['l0_a_k_norm_g'] = gain(A_HEAD_DIM)
    d['l0_b_cq_norm_g'] = gain(B_Q_LORA)
    d['l0_b_w_uq'] = dense((B_Q_LORA, B_HEADS * (B_NOPE_DIM + B_ROPE_DIM)), B_Q_LORA)
    d['l0_b_ckv_norm_g'] = gain(B_KV_LORA)
    d['l0_b_w_ukv'] = dense((B_KV_LORA, B_HEADS * (B_NOPE_DIM + B_V_DIM)), B_KV_LORA)
    d['l0_w_out'] = dense((MIX_WIDTH, D_MODEL), MIX_WIDTH)
    d['l0_mix_post_g'] = gain(D_MODEL)
    d.update(ffn('l0_ffn2'))
    d.update(ffn('l1_ffn1'))
    d['l1_mix_pre_g'] = gain(D_MODEL)
    d['l1_pool_w'] = dense((len(POOL_WINDOWS), POOL_GROUP, POOL_GROUP), POOL_GROUP)
    d['l1_pool_scale'] = 1.0 + 0.1 * jax.random.normal(next(ki), (D_MODEL,), jnp.float32)
    d['l1_mix_post_g'] = gain(D_MODEL)
    d.update(ffn('l1_ffn2'))
    return d


def reference(x_prompt, x_sample,
              l0_ffn1_pre_g, l0_ffn1_w_gate, l0_ffn1_w_up, l0_ffn1_w_down, l0_ffn1_post_g,
              l0_mix_pre_g, l0_w_in, l0_a_q_norm_g, l0_a_k_norm_g, l0_b_cq_norm_g, l0_b_w_uq,
              l0_b_ckv_norm_g, l0_b_w_ukv, l0_w_out, l0_mix_post_g,
              l0_ffn2_pre_g, l0_ffn2_w_gate, l0_ffn2_w_up, l0_ffn2_w_down, l0_ffn2_post_g,
              l1_ffn1_pre_g, l1_ffn1_w_gate, l1_ffn1_w_up, l1_ffn1_w_down, l1_ffn1_post_g,
              l1_mix_pre_g, l1_pool_w, l1_pool_scale, l1_mix_post_g,
              l1_ffn2_pre_g, l1_ffn2_w_gate, l1_ffn2_w_up, l1_ffn2_w_down, l1_ffn2_post_g):
    layers = [
        {'ffn1': (l0_ffn1_pre_g, l0_ffn1_w_gate, l0_ffn1_w_up, l0_ffn1_w_down, l0_ffn1_post_g),
         'mix': (l0_mix_pre_g, l0_w_in, l0_a_q_norm_g, l0_a_k_norm_g, l0_b_cq_norm_g, l0_b_w_uq,
                 l0_b_ckv_norm_g, l0_b_w_ukv, l0_w_out, l0_mix_post_g),
         'ffn2': (l0_ffn2_pre_g, l0_ffn2_w_gate, l0_ffn2_w_up, l0_ffn2_w_down, l0_ffn2_post_g)},
        {'ffn1': (l1_ffn1_pre_g, l1_ffn1_w_gate, l1_ffn1_w_up, l1_ffn1_w_down, l1_ffn1_post_g),
         'mix': (l1_mix_pre_g, l1_pool_w, l1_pool_scale, l1_mix_post_g),
         'ffn2': (l1_ffn2_pre_g, l1_ffn2_w_gate, l1_ffn2_w_up, l1_ffn2_w_down, l1_ffn2_post_g)},
    ]
    y_prompt = run_trunk(x_prompt, layers)
    y_sample = run_trunk(x_sample, layers)
    return (y_prompt, y_sample)
```

```python
import functools
import math

import jax
import jax.numpy as jnp
import numpy as np
from jax import lax
from jax.experimental import pallas as pl
from jax.experimental.pallas import tpu as pltpu

F32 = jnp.float32
BF16 = jnp.bfloat16

NORM_EPS = 1e-6
ROPE_BASE = 10000.0
GRID_W = 64
A_HEADS = 8
A_KV_HEADS = 2
A_HEAD_DIM = 128
B_HEADS = 8
B_Q_LORA = 512
B_KV_LORA = 256
B_NOPE_DIM = 128
B_ROPE_DIM = 64
B_V_DIM = 128
POOL_WINDOWS = (2, 4, 8, 16)
LANES = 128
B_QK_PAD = 2 * LANES
POOL_HALO = 8

VMEM_LIMIT_BYTES = 56 * 1024 * 1024


def _cparams(*semantics):
    return pltpu.CompilerParams(dimension_semantics=semantics, vmem_limit_bytes=VMEM_LIMIT_BYTES)


def _rms(x, g):
    return x * lax.rsqrt(jnp.mean(x * x, axis=-1, keepdims=True) + NORM_EPS) * g


def _silu(x):
    return x * (1.0 / (1.0 + jnp.exp(-x)))


def _ffn_kernel(x_ref, pre_g_ref, wg_ref, wu_ref, wd_ref, post_g_ref, o_ref, h_ref, acc_ref):
    f = pl.program_id(1)

    @pl.when(f == 0)
    def _():
        h_ref[...] = _rms(x_ref[...], pre_g_ref[...]).astype(BF16)

    h = h_ref[...]
    gate = jnp.dot(h, wg_ref[...], preferred_element_type=F32)
    up = jnp.dot(h, wu_ref[...], preferred_element_type=F32)
    act = (_silu(gate) * up).astype(BF16)
    down = jnp.dot(act, wd_ref[...], preferred_element_type=F32)

    @pl.when(f == 0)
    def _():
        acc_ref[...] = down

    @pl.when(f > 0)
    def _():
        acc_ref[...] += down

    @pl.when(f == pl.num_programs(1) - 1)
    def _():
        o_ref[...] = x_ref[...] + 0.5 * _rms(acc_ref[...], post_g_ref[...])


def _ffn_half(x, pre_g, wg, wu, wd, post_g, *, tm=512, tf=512):
    t, d = x.shape
    dff = wg.shape[1]
    row = lambda i, f: (i, 0)
    const = lambda i, f: (0, 0)
    return pl.pallas_call(
        _ffn_kernel,
        grid=(t // tm, dff // tf),
        in_specs=[
            pl.BlockSpec((tm, d), row),
            pl.BlockSpec((1, d), const),
            pl.BlockSpec((d, tf), lambda i, f: (0, f)),
            pl.BlockSpec((d, tf), lambda i, f: (0, f)),
            pl.BlockSpec((tf, d), lambda i, f: (f, 0)),
            pl.BlockSpec((1, d), const),
        ],
        out_specs=pl.BlockSpec((tm, d), row),
        out_shape=jax.ShapeDtypeStruct((t, d), F32),
        scratch_shapes=[pltpu.VMEM((tm, d), BF16), pltpu.VMEM((tm, d), F32)],
        compiler_params=_cparams("parallel", "arbitrary"),
        name="ffn_half",
    )(x, pre_g, wg, wu, wd, post_g)


def _rope(x, cos, sin):
    return x * cos + pltpu.roll(x, LANES // 2, 1) * sin


def _inproj_kernel(x_ref, pre_g_ref, w_in_ref, gq_ref, gk_ref, gcq_ref, w_uq_ref, gckv_ref, w_ukv_ref,
                   cos_a_ref, sin_a_ref, cos_b_ref, sin_b_ref,
                   qa_ref, ka_ref, va_ref, qb_ref, kb_ref, vb_ref):
    hn = _rms(x_ref[...], pre_g_ref[...]).astype(BF16)
    y = jnp.dot(hn, w_in_ref[...], preferred_element_type=F32)
    cos_a, sin_a = cos_a_ref[...], sin_a_ref[...]
    cos_b, sin_b = cos_b_ref[...], sin_b_ref[...]
    a_scale = A_HEAD_DIM ** -0.5
    b_scale = (B_NOPE_DIM + B_ROPE_DIM) ** -0.5
    q_cols = A_HEADS * A_HEAD_DIM
    kv_cols = A_KV_HEADS * A_HEAD_DIM

    for h in range(A_HEADS):
        sl = slice(h * LANES, (h + 1) * LANES)
        qa_ref[:, sl] = (_rope(_rms(y[:, sl], gq_ref[...]), cos_a, sin_a) * a_scale).astype(BF16)
    for h in range(A_KV_HEADS):
        sl = slice(h * LANES, (h + 1) * LANES)
        ka_ref[:, sl] = _rope(_rms(y[:, q_cols + h * LANES:q_cols + (h + 1) * LANES], gk_ref[...]),
                              cos_a, sin_a).astype(BF16)
    off = q_cols + kv_cols
    va_ref[...] = y[:, off:off + kv_cols].astype(BF16)
    off += kv_cols

    cq = _rms(y[:, off:off + B_Q_LORA], gcq_ref[...]).astype(BF16)
    off += B_Q_LORA
    ckv = _rms(y[:, off:off + B_KV_LORA], gckv_ref[...]).astype(BF16)
    off += B_KV_LORA
    k_rope = _rope(y[:, off:off + LANES], cos_b, sin_b).astype(BF16)

    qf = jnp.dot(cq, w_uq_ref[...], preferred_element_type=F32)
    kv = jnp.dot(ckv, w_ukv_ref[...], preferred_element_type=F32)
    for h in range(B_HEADS):
        base = h * B_QK_PAD
        qb_ref[:, base:base + LANES] = (qf[:, base:base + LANES] * b_scale).astype(BF16)
        qb_ref[:, base + LANES:base + B_QK_PAD] = (
            _rope(qf[:, base + LANES:base + B_QK_PAD], cos_b, sin_b) * b_scale).astype(BF16)
        kb_ref[:, base:base + LANES] = kv[:, h * LANES:(h + 1) * LANES].astype(BF16)
        kb_ref[:, base + LANES:base + B_QK_PAD] = k_rope
    vb_ref[...] = kv[:, B_HEADS * B_NOPE_DIM:].astype(BF16)


def _mixer_inproj(x, seq, pre_g, w_in, gq, gk, gcq, w_uq, gckv, w_ukv, tabs, *, tm=256):
    t, d = x.shape
    nblk = seq // tm
    row = lambda i: (i, 0)
    const = lambda i: (0, 0)
    pos = lambda i: (i % nblk, 0)
    full = lambda a: pl.BlockSpec(a.shape, const)
    widths = (A_HEADS * A_HEAD_DIM, A_KV_HEADS * A_HEAD_DIM, A_KV_HEADS * A_HEAD_DIM,
              B_HEADS * B_QK_PAD, B_HEADS * B_QK_PAD, B_HEADS * B_V_DIM)
    return pl.pallas_call(
        _inproj_kernel,
        grid=(t // tm,),
        in_specs=[pl.BlockSpec((tm, d), row), full(pre_g), full(w_in), full(gq), full(gk), full(gcq),
                  full(w_uq), full(gckv), full(w_ukv)] + [pl.BlockSpec((tm, LANES), pos)] * 4,
        out_specs=[pl.BlockSpec((tm, w), row) for w in widths],
        out_shape=[jax.ShapeDtypeStruct((t, w), BF16) for w in widths],
        compiler_params=_cparams("parallel"),
        name="mixer_inproj",
    )(x, pre_g, w_in, gq, gk, gcq, w_uq, gckv, w_ukv, *tabs)


def _attn_kernel(q_ref, k_ref, v_ref, o_ref, *, group, dk, dv, tk):
    tq = q_ref.shape[0]
    seq = k_ref.shape[0]
    q = q_ref[...]
    qs = jnp.concatenate([q[:, j * dk:(j + 1) * dk] for j in range(group)], axis=0) if group > 1 else q
    rows = group * tq

    def body(c, carry):
        m, l, acc = carry
        start = pl.multiple_of(c * tk, tk)
        k = k_ref[pl.ds(start, tk), :]
        v = v_ref[pl.ds(start, tk), :]
        s = lax.dot_general(qs, k, (((1,), (1,)), ((), ())), preferred_element_type=F32)
        m_new = jnp.maximum(m, jnp.max(s, axis=-1, keepdims=True))
        alpha = jnp.exp(m - m_new)
        p = jnp.exp(s - m_new)
        l = alpha * l + jnp.sum(p, axis=-1, keepdims=True)
        acc = alpha * acc + jnp.dot(p.astype(BF16), v, preferred_element_type=F32)
        return m_new, l, acc

    init = (jnp.full((rows, 1), -jnp.inf, F32), jnp.zeros((rows, 1), F32), jnp.zeros((rows, dv), F32))
    _, l, acc = lax.fori_loop(0, seq // tk, body, init)
    out = acc / l
    for j in range(group):
        o_ref[:, j * dv:(j + 1) * dv] = out[j * tq:(j + 1) * tq].astype(o_ref.dtype)


def _attention(q, k, v, *, kv_heads, group, dk, dv, tq, tk=512):
    b, seq, _ = q.shape
    tq, tk = min(tq, seq), min(tk, seq)
    kern = functools.partial(_attn_kernel, group=group, dk=dk, dv=dv, tk=tk)
    return pl.pallas_call(
        kern,
        grid=(b, kv_heads, seq // tq),
        in_specs=[
            pl.BlockSpec((None, tq, group * dk), lambda bi, h, i: (bi, i, h)),
            pl.BlockSpec((None, seq, dk), lambda bi, h, i: (bi, 0, h)),
            pl.BlockSpec((None, seq, dv), lambda bi, h, i: (bi, 0, h)),
        ],
        out_specs=pl.BlockSpec((None, tq, group * dv), lambda bi, h, i: (bi, i, h)),
        out_shape=jax.ShapeDtypeStruct((b, seq, kv_heads * group * dv), BF16),
        compiler_params=_cparams("parallel", "parallel", "arbitrary"),
        name=f"attention_g{group}_dk{dk}",
    )(q, k, v)


def _outproj_kernel(x_ref, oa_ref, ob_ref, wa_ref, wb_ref, post_g_ref, o_ref):
    m = jnp.dot(oa_ref[...], wa_ref[...], preferred_element_type=F32)
    m += jnp.dot(ob_ref[...], wb_ref[...], preferred_element_type=F32)
    o_ref[...] = x_ref[...] + _rms(m, post_g_ref[...])


def _mixer_outproj(x, oa, ob, wa, wb, post_g, *, tm=512):
    t, d = x.shape
    row = lambda i: (i, 0)
    const = lambda i: (0, 0)
    return pl.pallas_call(
        _outproj_kernel,
        grid=(t // tm,),
        in_specs=[pl.BlockSpec((tm, d), row), pl.BlockSpec((tm, oa.shape[1]), row),
                  pl.BlockSpec((tm, ob.shape[1]), row), pl.BlockSpec(wa.shape, const),
                  pl.BlockSpec(wb.shape, const), pl.BlockSpec((1, d), const)],
        out_specs=pl.BlockSpec((tm, d), row),
        out_shape=jax.ShapeDtypeStruct((t, d), F32),
        compiler_params=_cparams("parallel"),
        name="mixer_outproj",
    )(x, oa, ob, wa, wb, post_g)


def _pool_kernel(x_ref, prev_ref, next_ref, pre_g_ref, w_ref, scale_ref, post_g_ref, o_ref, ext_ref, m_ref,
                 *, seq):
    tm, d = x_ref.shape
    group = d // len(POOL_WINDOWS)
    nblk = seq // tm
    blk = pl.program_id(0) % nblk
    g = pre_g_ref[...]
    x = x_ref[...]
    ext_ref[0:POOL_HALO, :] = jnp.where(blk > 0, _rms(prev_ref[...], g), 0.0)
    ext_ref[POOL_HALO:POOL_HALO + tm, :] = _rms(x, g)
    ext_ref[POOL_HALO + tm:, :] = jnp.where(blk < nblk - 1, _rms(next_ref[...], g), 0.0)

    t = blk * tm + lax.broadcasted_iota(jnp.int32, (tm, 1), 0)
    for gi, w in enumerate(POOL_WINDOWS):
        cols = slice(gi * group, (gi + 1) * group)
        half = w // 2
        tot = ext_ref[POOL_HALO - half:POOL_HALO - half + tm, cols]
        for j in range(1, w):
            tot += ext_ref[POOL_HALO - half + j:POOL_HALO - half + j + tm, cols]
        cnt = (jnp.minimum(t + half, seq) - jnp.maximum(t - half, 0)).astype(F32)
        pooled = tot / cnt - ext_ref[POOL_HALO:POOL_HALO + tm, cols]
        m_ref[:, cols] = jnp.dot(pooled.astype(BF16), w_ref[gi], preferred_element_type=F32)
    o_ref[...] = x + _rms(m_ref[...] * scale_ref[...], post_g_ref[...])


def _pool_mixer(x, seq, pre_g, pool_w, pool_scale, post_g, *, tm=512):
    t, d = x.shape
    hb = tm // POOL_HALO
    last = t // POOL_HALO - 1
    row = lambda i: (i, 0)
    const = lambda i: (0, 0)
    return pl.pallas_call(
        functools.partial(_pool_kernel, seq=seq),
        grid=(t // tm,),
        in_specs=[
            pl.BlockSpec((tm, d), row),
            pl.BlockSpec((POOL_HALO, d), lambda i: (jnp.maximum(i * hb - 1, 0), 0)),
            pl.BlockSpec((POOL_HALO, d), lambda i: (jnp.minimum((i + 1) * hb, last), 0)),
            pl.BlockSpec((1, d), const),
            pl.BlockSpec(pool_w.shape, lambda i: (0, 0, 0)),
            pl.BlockSpec((1, d), const),
            pl.BlockSpec((1, d), const),
        ],
        out_specs=pl.BlockSpec((tm, d), row),
        out_shape=jax.ShapeDtypeStruct((t, d), F32),
        scratch_shapes=[pltpu.VMEM((tm + 2 * POOL_HALO, d), F32), pltpu.VMEM((tm, d), F32)],
        compiler_params=_cparams("parallel"),
        name="pool_mixer",
    )(x, x, x, pre_g, pool_w, pool_scale, post_g)


def _rope_tables(seq):
    rows = seq // GRID_W
    row = jnp.repeat(jnp.arange(rows, dtype=F32), GRID_W)
    col = jnp.tile(jnp.arange(GRID_W, dtype=F32), rows)

    def angles(rot_dim):
        half = rot_dim // 2
        freqs = ROPE_BASE ** (-jnp.arange(0, half, 2, dtype=F32) / half)
        return jnp.concatenate([row[:, None] * freqs, col[:, None] * freqs], axis=-1)

    ang_a = angles(A_HEAD_DIM)
    cos_a = jnp.concatenate([jnp.cos(ang_a)] * 2, axis=-1)
    sin_a = jnp.concatenate([-jnp.sin(ang_a), jnp.sin(ang_a)], axis=-1)
    ang_b = angles(B_ROPE_DIM)
    z = jnp.zeros_like(ang_b)
    cos_b = jnp.concatenate([jnp.cos(ang_b), z, jnp.cos(ang_b), z], axis=-1)
    sin_b = jnp.concatenate([-jnp.sin(ang_b), z, jnp.sin(ang_b), z], axis=-1)
    return cos_a, sin_a, cos_b, sin_b


def _pair_split_perm(n):
    return np.concatenate([np.arange(0, n, 2), np.arange(1, n, 2)])


def _spread_rope_cols(w):
    half = B_ROPE_DIM // 2
    z = jnp.zeros((w.shape[0], half), w.dtype)
    return jnp.concatenate([w[:, 0::2], z, w[:, 1::2], z], axis=-1)


def _prep_mixer_weights(w_in, gq, gk, w_uq, w_ukv, w_out):
    d = w_in.shape[0]
    perm = _pair_split_perm(A_HEAD_DIM)
    nq, nkv = A_HEADS * A_HEAD_DIM, A_KV_HEADS * A_HEAD_DIM
    wq = w_in[:, :nq].reshape(d, A_HEADS, A_HEAD_DIM)[:, :, perm].reshape(d, nq)
    wk = w_in[:, nq:nq + nkv].reshape(d, A_KV_HEADS, A_HEAD_DIM)[:, :, perm].reshape(d, nkv)
    rest = w_in[:, nq + nkv:-B_ROPE_DIM]
    w_in_p = jnp.concatenate([wq, wk, rest, _spread_rope_cols(w_in[:, -B_ROPE_DIM:])], axis=-1).astype(BF16)

    uq = w_uq.reshape(B_Q_LORA, B_HEADS, B_NOPE_DIM + B_ROPE_DIM)
    uq_rope = _spread_rope_cols(uq[:, :, B_NOPE_DIM:].reshape(B_Q_LORA * B_HEADS, B_ROPE_DIM))
    uq_p = jnp.concatenate([uq[:, :, :B_NOPE_DIM], uq_rope.reshape(B_Q_LORA, B_HEADS, LANES)], axis=-1)
    uq_p = uq_p.reshape(B_Q_LORA, B_HEADS * B_QK_PAD).astype(BF16)

    ukv = w_ukv.reshape(B_KV_LORA, B_HEADS, B_NOPE_DIM + B_V_DIM)
    ukv_p = jnp.concatenate([ukv[:, :, :B_NOPE_DIM].reshape(B_KV_LORA, -1),
                             ukv[:, :, B_NOPE_DIM:].reshape(B_KV_LORA, -1)], axis=-1).astype(BF16)
    na = A_HEADS * A_HEAD_DIM
    return (w_in_p, gq[perm][None, :], gk[perm][None, :], uq_p, ukv_p,
            w_out[:na].astype(BF16), w_out[na:].astype(BF16))


def _trunk(x, ffn_params, mix0, mix1, tabs):
    b, seq, d = x.shape
    x = x.reshape(b * seq, d)
    (pre0, w_in, gq, gk, gcq, w_uq, gckv, w_ukv, wa, wb, post0) = mix0
    (pre1, pool_w, pool_scale, post1) = mix1

    x = _ffn_half(x, *ffn_params[0])
    qa, ka, va, qb, kb, vb = _mixer_inproj(x, seq, pre0, w_in, gq, gk, gcq, w_uq, gckv, w_ukv, tabs)
    shp = lambda a: a.reshape(b, seq, a.shape[-1])
    oa = _attention(shp(qa), shp(ka), shp(va), kv_heads=A_KV_HEADS, group=A_HEADS // A_KV_HEADS,
                    dk=A_HEAD_DIM, dv=A_HEAD_DIM, tq=256)
    ob = _attention(shp(qb), shp(kb), shp(vb), kv_heads=B_HEADS, group=1, dk=B_QK_PAD, dv=B_V_DIM, tq=1024)
    x = _mixer_outproj(x, oa.reshape(b * seq, -1), ob.reshape(b * seq, -1), wa, wb, post0)
    x = _ffn_half(x, *ffn_params[1])
    x = _ffn_half(x, *ffn_params[2])
    x = _pool_mixer(x, seq, pre1, pool_w, pool_scale, post1)
    x = _ffn_half(x, *ffn_params[3])
    return x.reshape(b, seq, d)


def kernel(x_prompt, x_sample, l0_ffn1_pre_g, l0_ffn1_w_gate, l0_ffn1_w_up, l0_ffn1_w_down, l0_ffn1_post_g, l0_mix_pre_g, l0_w_in, l0_a_q_norm_g, l0_a_k_norm_g, l0_b_cq_norm_g, l0_b_w_uq, l0_b_ckv_norm_g, l0_b_w_ukv, l0_w_out, l0_mix_post_g, l0_ffn2_pre_g, l0_ffn2_w_gate, l0_ffn2_w_up, l0_ffn2_w_down, l0_ffn2_post_g, l1_ffn1_pre_g, l1_ffn1_w_gate, l1_ffn1_w_up, l1_ffn1_w_down, l1_ffn1_post_g, l1_mix_pre_g, l1_pool_w, l1_pool_scale, l1_mix_post_g, l1_ffn2_pre_g, l1_ffn2_w_gate, l1_ffn2_w_up, l1_ffn2_w_down, l1_ffn2_post_g):
    vec = lambda g: g[None, :]
    ffn = lambda pre, wg, wu, wd, post: (vec(pre), wg.astype(BF16), wu.astype(BF16), wd.astype(BF16), vec(post))
    ffn_params = (
        ffn(l0_ffn1_pre_g, l0_ffn1_w_gate, l0_ffn1_w_up, l0_ffn1_w_down, l0_ffn1_post_g),
        ffn(l0_ffn2_pre_g, l0_ffn2_w_gate, l0_ffn2_w_up, l0_ffn2_w_down, l0_ffn2_post_g),
        ffn(l1_ffn1_pre_g, l1_ffn1_w_gate, l1_ffn1_w_up, l1_ffn1_w_down, l1_ffn1_post_g),
        ffn(l1_ffn2_pre_g, l1_ffn2_w_gate, l1_ffn2_w_up, l1_ffn2_w_down, l1_ffn2_post_g),
    )
    w_in, gq, gk, w_uq, w_ukv, wa, wb = _prep_mixer_weights(
        l0_w_in, l0_a_q_norm_g, l0_a_k_norm_g, l0_b_w_uq, l0_b_w_ukv, l0_w_out)
    mix0 = (vec(l0_mix_pre_g), w_in, gq, gk, vec(l0_b_cq_norm_g), w_uq, vec(l0_b_ckv_norm_g), w_ukv,
            wa, wb, vec(l0_mix_post_g))
    mix1 = (vec(l1_mix_pre_g), l1_pool_w.astype(BF16), vec(l1_pool_scale), vec(l1_mix_post_g))

    outs = []
    for x in (x_prompt, x_sample):
        tabs = _rope_tables(x.shape[1])
        outs.append(_trunk(x, ffn_params, mix0, mix1, tabs))
    return tuple(outs)
```

```python
import functools
import math

import jax
import jax.numpy as jnp
import numpy as np
from jax import lax
from jax.experimental import pallas as pl
from jax.experimental.pallas import tpu as pltpu

F32 = jnp.float32
BF16 = jnp.bfloat16

NORM_EPS = 1e-6
ROPE_BASE = 10000.0
GRID_W = 64
A_HEADS = 8
A_KV_HEADS = 2
A_HEAD_DIM = 128
B_HEADS = 8
B_Q_LORA = 512
B_KV_LORA = 256
B_NOPE_DIM = 128
B_ROPE_DIM = 64
B_V_DIM = 128
POOL_WINDOWS = (2, 4, 8, 16)
LANES = 128
B_QK_PAD = 2 * LANES
ROW_CHUNK = 32
ROW_CHUNK_UNROLL = 4
FFN_SUB_ROWS = 256
POOL_HALO = 8

VMEM_LIMIT_BYTES = 60 * 1024 * 1024


def _cparams(*semantics):
    return pltpu.CompilerParams(dimension_semantics=semantics, vmem_limit_bytes=VMEM_LIMIT_BYTES)


def _rms(x, g):
    return x * lax.rsqrt(jnp.mean(x * x, axis=-1, keepdims=True) + NORM_EPS) * g


def _silu(x):
    return x * (1.0 / (1.0 + jnp.exp(-x)))


def _for_row_chunks(nrows, fn):
    def body(i, carry):
        fn(pl.ds(pl.multiple_of(i * ROW_CHUNK, ROW_CHUNK), ROW_CHUNK))
        return carry

    lax.fori_loop(0, nrows // ROW_CHUNK, body, 0, unroll=ROW_CHUNK_UNROLL)


def _ffn_kernel(x_ref, pre_g_ref, wg_ref, wu_ref, wd_ref, post_g_ref, o_ref, h_ref):
    f = pl.program_id(1)

    tm = x_ref.shape[0]

    @pl.when(f == 0)
    def _():
        def prologue(rows):
            h_ref[rows, :] = _rms(x_ref[rows, :], pre_g_ref[...]).astype(BF16)
            o_ref[rows, :] = jnp.zeros((ROW_CHUNK, o_ref.shape[1]), F32)

        _for_row_chunks(tm, prologue)

    for r in range(tm // FFN_SUB_ROWS):
        rows = slice(r * FFN_SUB_ROWS, (r + 1) * FFN_SUB_ROWS)
        h = h_ref[rows, :]
        gate = jnp.dot(h, wg_ref[...], preferred_element_type=F32)
        up = jnp.dot(h, wu_ref[...], preferred_element_type=F32)
        act = (_silu(gate) * up).astype(BF16)
        o_ref[rows, :] += jnp.dot(act, wd_ref[...], preferred_element_type=F32)

    @pl.when(f == pl.num_programs(1) - 1)
    def _():
        for c in range(tm // ROW_CHUNK):
            rows = slice(c * ROW_CHUNK, (c + 1) * ROW_CHUNK)
            o_ref[rows, :] = x_ref[rows, :] + 0.5 * _rms(o_ref[rows, :], post_g_ref[...])


def _ffn_half(x, pre_g, wg, wu, wd, post_g, *, tm=1024, tf=512):
    t, d = x.shape
    dff = wg.shape[1]
    row = lambda i, f: (i, 0)
    const = lambda i, f: (0, 0)
    return pl.pallas_call(
        _ffn_kernel,
        grid=(t // tm, dff // tf),
        in_specs=[
            pl.BlockSpec((tm, d), row),
            pl.BlockSpec((1, d), const),
            pl.BlockSpec((d, tf), lambda i, f: (0, f)),
            pl.BlockSpec((d, tf), lambda i, f: (0, f)),
            pl.BlockSpec((tf, d), lambda i, f: (f, 0)),
            pl.BlockSpec((1, d), const),
        ],
        out_specs=pl.BlockSpec((tm, d), row),
        out_shape=jax.ShapeDtypeStruct((t, d), F32),
        scratch_shapes=[pltpu.VMEM((tm, d), BF16)],
        compiler_params=_cparams("parallel", "arbitrary"),
        name="ffn_half",
    )(x, pre_g, wg, wu, wd, post_g)


def _rope(x, cos, sin):
    return x * cos + pltpu.roll(x, LANES // 2, 1) * sin


def _inproj_kernel(x_ref, pre_g_ref, w_in_ref, gq_ref, gk_ref, gcq_ref, w_uq_ref, gckv_ref, w_ukv_ref,
                   cos_a_ref, sin_a_ref, cos_b_ref, sin_b_ref,
                   qa_ref, ka_ref, va_ref, qb_ref, kb_ref, vb_ref):
    hn = _rms(x_ref[...], pre_g_ref[...]).astype(BF16)
    y = jnp.dot(hn, w_in_ref[...], preferred_element_type=F32)
    cos_a, sin_a = cos_a_ref[...], sin_a_ref[...]
    cos_b, sin_b = cos_b_ref[...], sin_b_ref[...]
    a_scale = A_HEAD_DIM ** -0.5
    b_scale = (B_NOPE_DIM + B_ROPE_DIM) ** -0.5
    q_cols = A_HEADS * A_HEAD_DIM
    kv_cols = A_KV_HEADS * A_HEAD_DIM

    for h in range(A_HEADS):
        sl = slice(h * LANES, (h + 1) * LANES)
        qa_ref[:, sl] = (_rope(_rms(y[:, sl], gq_ref[...]), cos_a, sin_a) * a_scale).astype(BF16)
    for h in range(A_KV_HEADS):
        sl = slice(h * LANES, (h + 1) * LANES)
        ka_ref[:, sl] = _rope(_rms(y[:, q_cols + h * LANES:q_cols + (h + 1) * LANES], gk_ref[...]),
                              cos_a, sin_a).astype(BF16)
    off = q_cols + kv_cols
    va_ref[...] = y[:, off:off + kv_cols].astype(BF16)
    off += kv_cols

    cq = _rms(y[:, off:off + B_Q_LORA], gcq_ref[...]).astype(BF16)
    off += B_Q_LORA
    ckv = _rms(y[:, off:off + B_KV_LORA], gckv_ref[...]).astype(BF16)
    off += B_KV_LORA
    k_rope = _rope(y[:, off:off + LANES], cos_b, sin_b).astype(BF16)

    qf = jnp.dot(cq, w_uq_ref[...], preferred_element_type=F32)
    kv = jnp.dot(ckv, w_ukv_ref[...], preferred_element_type=F32)
    for h in range(B_HEADS):
        base = h * B_QK_PAD
        qb_ref[:, base:base + LANES] = (qf[:, base:base + LANES] * b_scale).astype(BF16)
        qb_ref[:, base + LANES:base + B_QK_PAD] = (
            _rope(qf[:, base + LANES:base + B_QK_PAD], cos_b, sin_b) * b_scale).astype(BF16)
        kb_ref[:, base:base + LANES] = kv[:, h * LANES:(h + 1) * LANES].astype(BF16)
        kb_ref[:, base + LANES:base + B_QK_PAD] = k_rope
    vb_ref[...] = kv[:, B_HEADS * B_NOPE_DIM:].astype(BF16)


def _mixer_inproj(x, seq, pre_g, w_in, gq, gk, gcq, w_uq, gckv, w_ukv, tabs, *, tm=256):
    t, d = x.shape
    nblk = seq // tm
    row = lambda i: (i, 0)
    const = lambda i: (0, 0)
    pos = lambda i: (i % nblk, 0)
    full = lambda a: pl.BlockSpec(a.shape, const)
    widths = (A_HEADS * A_HEAD_DIM, A_KV_HEADS * A_HEAD_DIM, A_KV_HEADS * A_HEAD_DIM,
              B_HEADS * B_QK_PAD, B_HEADS * B_QK_PAD, B_HEADS * B_V_DIM)
    return pl.pallas_call(
        _inproj_kernel,
        grid=(t // tm,),
        in_specs=[pl.BlockSpec((tm, d), row), full(pre_g), full(w_in), full(gq), full(gk), full(gcq),
                  full(w_uq), full(gckv), full(w_ukv)] + [pl.BlockSpec((tm, LANES), pos)] * 4,
        out_specs=[pl.BlockSpec((tm, w), row) for w in widths],
        out_shape=[jax.ShapeDtypeStruct((t, w), BF16) for w in widths],
        compiler_params=_cparams("parallel"),
        name="mixer_inproj",
    )(x, pre_g, w_in, gq, gk, gcq, w_uq, gckv, w_ukv, *tabs)


def _attn_kernel(q_ref, k_ref, v_ref, o_ref, *, group, dk, dv, tk):
    tq = q_ref.shape[0]
    seq = k_ref.shape[0]
    q = q_ref[...]
    qs = jnp.concatenate([q[:, j * dk:(j + 1) * dk] for j in range(group)], axis=0) if group > 1 else q
    rows = group * tq

    def body(c, carry):
        m, l, acc = carry
        start = pl.multiple_of(c * tk, tk)
        k = k_ref[pl.ds(start, tk), :]
        v = v_ref[pl.ds(start, tk), :]
        s = lax.dot_general(qs, k, (((1,), (1,)), ((), ())), preferred_element_type=F32)
        m_new = jnp.maximum(m, jnp.max(s, axis=-1, keepdims=True))
        alpha = jnp.exp(m - m_new)
        p = jnp.exp(s - m_new)
        l = alpha * l + jnp.sum(p, axis=-1, keepdims=True)
        acc = alpha * acc + jnp.dot(p.astype(BF16), v, preferred_element_type=F32)
        return m_new, l, acc

    init = (jnp.full((rows, 1), -jnp.inf, F32), jnp.zeros((rows, 1), F32), jnp.zeros((rows, dv), F32))
    _, l, acc = lax.fori_loop(0, seq // tk, body, init)
    out = acc / l
    for j in range(group):
        o_ref[:, j * dv:(j + 1) * dv] = out[j * tq:(j + 1) * tq].astype(o_ref.dtype)


def _attention(q, k, v, *, kv_heads, group, dk, dv, tq, tk=512):
    b, seq, _ = q.shape
    tq, tk = min(tq, seq), min(tk, seq)
    kern = functools.partial(_attn_kernel, group=group, dk=dk, dv=dv, tk=tk)
    return pl.pallas_call(
        kern,
        grid=(b, kv_heads, seq // tq),
        in_specs=[
            pl.BlockSpec((None, tq, group * dk), lambda bi, h, i: (bi, i, h)),
            pl.BlockSpec((None, seq, dk), lambda bi, h, i: (bi, 0, h)),
            pl.BlockSpec((None, seq, dv), lambda bi, h, i: (bi, 0, h)),
        ],
        out_specs=pl.BlockSpec((None, tq, group * dv), lambda bi, h, i: (bi, i, h)),
        out_shape=jax.ShapeDtypeStruct((b, seq, kv_heads * group * dv), BF16),
        compiler_params=_cparams("parallel", "parallel", "arbitrary"),
        name=f"attention_g{group}_dk{dk}",
    )(q, k, v)


def _outproj_kernel(x_ref, oa_ref, ob_ref, wa_ref, wb_ref, post_g_ref, o_ref):
    m = jnp.dot(oa_ref[...], wa_ref[...], preferred_element_type=F32)
    m += jnp.dot(ob_ref[...], wb_ref[...], preferred_element_type=F32)
    o_ref[...] = x_ref[...] + _rms(m, post_g_ref[...])


def _mixer_outproj(x, oa, ob, wa, wb, post_g, *, tm=512):
    t, d = x.shape
    row = lambda i: (i, 0)
    const = lambda i: (0, 0)
    return pl.pallas_call(
        _outproj_kernel,
        grid=(t // tm,),
        in_specs=[pl.BlockSpec((tm, d), row), pl.BlockSpec((tm, oa.shape[1]), row),
                  pl.BlockSpec((tm, ob.shape[1]), row), pl.BlockSpec(wa.shape, const),
                  pl.BlockSpec(wb.shape, const), pl.BlockSpec((1, d), const)],
        out_specs=pl.BlockSpec((tm, d), row),
        out_shape=jax.ShapeDtypeStruct((t, d), F32),
        compiler_params=_cparams("parallel"),
        name="mixer_outproj",
    )(x, oa, ob, wa, wb, post_g)


def _pool_kernel(x_ref, prev_ref, next_ref, pre_g_ref, w_ref, scale_ref, post_g_ref, o_ref, ext_ref, m_ref,
                 *, seq):
    tm, d = x_ref.shape
    group = d // len(POOL_WINDOWS)
    nblk = seq // tm
    blk = pl.program_id(0) % nblk
    g = pre_g_ref[...]
    x = x_ref[...]
    ext_ref[0:POOL_HALO, :] = jnp.where(blk > 0, _rms(prev_ref[...], g), 0.0)
    ext_ref[POOL_HALO:POOL_HALO + tm, :] = _rms(x, g)
    ext_ref[POOL_HALO + tm:, :] = jnp.where(blk < nblk - 1, _rms(next_ref[...], g), 0.0)

    t = blk * tm + lax.broadcasted_iota(jnp.int32, (tm, 1), 0)
    for gi, w in enumerate(POOL_WINDOWS):
        cols = slice(gi * group, (gi + 1) * group)
        half = w // 2
        tot = ext_ref[POOL_HALO - half:POOL_HALO - half + tm, cols]
        for j in range(1, w):
            tot += ext_ref[POOL_HALO - half + j:POOL_HALO - half + j + tm, cols]
        cnt = (jnp.minimum(t + half, seq) - jnp.maximum(t - half, 0)).astype(F32)
        pooled = tot / cnt - ext_ref[POOL_HALO:POOL_HALO + tm, cols]
        m_ref[:, cols] = jnp.dot(pooled.astype(BF16), w_ref[gi], preferred_element_type=F32)
    o_ref[...] = x + _rms(m_ref[...] * scale_ref[...], post_g_ref[...])


def _pool_mixer(x, seq, pre_g, pool_w, pool_scale, post_g, *, tm=512):
    t, d = x.shape
    hb = tm // POOL_HALO
    last = t // POOL_HALO - 1
    row = lambda i: (i, 0)
    const = lambda i: (0, 0)
    return pl.pallas_call(
        functools.partial(_pool_kernel, seq=seq),
        grid=(t // tm,),
        in_specs=[
            pl.BlockSpec((tm, d), row),
            pl.BlockSpec((POOL_HALO, d), lambda i: (jnp.maximum(i * hb - 1, 0), 0)),
            pl.BlockSpec((POOL_HALO, d), lambda i: (jnp.minimum((i + 1) * hb, last), 0)),
            pl.BlockSpec((1, d), const),
            pl.BlockSpec(pool_w.shape, lambda i: (0, 0, 0)),
            pl.BlockSpec((1, d), const),
            pl.BlockSpec((1, d), const),
        ],
        out_specs=pl.BlockSpec((tm, d), row),
        out_shape=jax.ShapeDtypeStruct((t, d), F32),
        scratch_shapes=[pltpu.VMEM((tm + 2 * POOL_HALO, d), F32), pltpu.VMEM((tm, d), F32)],
        compiler_params=_cparams("parallel"),
        name="pool_mixer",
    )(x, x, x, pre_g, pool_w, pool_scale, post_g)


def _rope_tables(seq):
    rows = seq // GRID_W
    row = jnp.repeat(jnp.arange(rows, dtype=F32), GRID_W)
    col = jnp.tile(jnp.arange(GRID_W, dtype=F32), rows)

    def angles(rot_dim):
        half = rot_dim // 2
        freqs = ROPE_BASE ** (-jnp.arange(0, half, 2, dtype=F32) / half)
        return jnp.concatenate([row[:, None] * freqs, col[:, None] * freqs], axis=-1)

    ang_a = angles(A_HEAD_DIM)
    cos_a = jnp.concatenate([jnp.cos(ang_a)] * 2, axis=-1)
    sin_a = jnp.concatenate([-jnp.sin(ang_a), jnp.sin(ang_a)], axis=-1)
    ang_b = angles(B_ROPE_DIM)
    z = jnp.zeros_like(ang_b)
    cos_b = jnp.concatenate([jnp.cos(ang_b), z, jnp.cos(ang_b), z], axis=-1)
    sin_b = jnp.concatenate([-jnp.sin(ang_b), z, jnp.sin(ang_b), z], axis=-1)
    return cos_a, sin_a, cos_b, sin_b


def _pair_split_perm(n):
    return np.concatenate([np.arange(0, n, 2), np.arange(1, n, 2)])


def _spread_rope_cols(w):
    half = B_ROPE_DIM // 2
    z = jnp.zeros((w.shape[0], half), w.dtype)
    return jnp.concatenate([w[:, 0::2], z, w[:, 1::2], z], axis=-1)


def _prep_mixer_weights(w_in, gq, gk, w_uq, w_ukv, w_out):
    d = w_in.shape[0]
    perm = _pair_split_perm(A_HEAD_DIM)
    nq, nkv = A_HEADS * A_HEAD_DIM, A_KV_HEADS * A_HEAD_DIM
    wq = w_in[:, :nq].reshape(d, A_HEADS, A_HEAD_DIM)[:, :, perm].reshape(d, nq)
    wk = w_in[:, nq:nq + nkv].reshape(d, A_KV_HEADS, A_HEAD_DIM)[:, :, perm].reshape(d, nkv)
    rest = w_in[:, nq + nkv:-B_ROPE_DIM]
    w_in_p = jnp.concatenate([wq, wk, rest, _spread_rope_cols(w_in[:, -B_ROPE_DIM:])], axis=-1).astype(BF16)

    uq = w_uq.reshape(B_Q_LORA, B_HEADS, B_NOPE_DIM + B_ROPE_DIM)
    uq_rope = _spread_rope_cols(uq[:, :, B_NOPE_DIM:].reshape(B_Q_LORA * B_HEADS, B_ROPE_DIM))
    uq_p = jnp.concatenate([uq[:, :, :B_NOPE_DIM], uq_rope.reshape(B_Q_LORA, B_HEADS, LANES)], axis=-1)
    uq_p = uq_p.reshape(B_Q_LORA, B_HEADS * B_QK_PAD).astype(BF16)

    ukv = w_ukv.reshape(B_KV_LORA, B_HEADS, B_NOPE_DIM + B_V_DIM)
    ukv_p = jnp.concatenate([ukv[:, :, :B_NOPE_DIM].reshape(B_KV_LORA, -1),
                             ukv[:, :, B_NOPE_DIM:].reshape(B_KV_LORA, -1)], axis=-1).astype(BF16)
    na = A_HEADS * A_HEAD_DIM
    return (w_in_p, gq[perm][None, :], gk[perm][None, :], uq_p, ukv_p,
            w_out[:na].astype(BF16), w_out[na:].astype(BF16))


def _trunk(x, ffn_params, mix0, mix1, tabs):
    b, seq, d = x.shape
    x = x.reshape(b * seq, d)
    (pre0, w_in, gq, gk, gcq, w_uq, gckv, w_ukv, wa, wb, post0) = mix0
    (pre1, pool_w, pool_scale, post1) = mix1

    x = _ffn_half(x, *ffn_params[0])
    qa, ka, va, qb, kb, vb = _mixer_inproj(x, seq, pre0, w_in, gq, gk, gcq, w_uq, gckv, w_ukv, tabs)
    shp = lambda a: a.reshape(b, seq, a.shape[-1])
    oa = _attention(shp(qa), shp(ka), shp(va), kv_heads=A_KV_HEADS, group=A_HEADS // A_KV_HEADS,
                    dk=A_HEAD_DIM, dv=A_HEAD_DIM, tq=256)
    ob = _attention(shp(qb), shp(kb), shp(vb), kv_heads=B_HEADS, group=1, dk=B_QK_PAD, dv=B_V_DIM, tq=1024)
    x = _mixer_outproj(x, oa.reshape(b * seq, -1), ob.reshape(b * seq, -1), wa, wb, post0)
    x = _ffn_half(x, *ffn_params[1])
    x = _ffn_half(x, *ffn_params[2])
    x = _pool_mixer(x, seq, pre1, pool_w, pool_scale, post1)
    x = _ffn_half(x, *ffn_params[3])
    return x.reshape(b, seq, d)


def kernel(x_prompt, x_sample, l0_ffn1_pre_g, l0_ffn1_w_gate, l0_ffn1_w_up, l0_ffn1_w_down, l0_ffn1_post_g, l0_mix_pre_g, l0_w_in, l0_a_q_norm_g, l0_a_k_norm_g, l0_b_cq_norm_g, l0_b_w_uq, l0_b_ckv_norm_g, l0_b_w_ukv, l0_w_out, l0_mix_post_g, l0_ffn2_pre_g, l0_ffn2_w_gate, l0_ffn2_w_up, l0_ffn2_w_down, l0_ffn2_post_g, l1_ffn1_pre_g, l1_ffn1_w_gate, l1_ffn1_w_up, l1_ffn1_w_down, l1_ffn1_post_g, l1_mix_pre_g, l1_pool_w, l1_pool_scale, l1_mix_post_g, l1_ffn2_pre_g, l1_ffn2_w_gate, l1_ffn2_w_up, l1_ffn2_w_down, l1_ffn2_post_g):
    vec = lambda g: g[None, :]
    ffn = lambda pre, wg, wu, wd, post: (vec(pre), wg.astype(BF16), wu.astype(BF16), wd.astype(BF16), vec(post))
    ffn_params = (
        ffn(l0_ffn1_pre_g, l0_ffn1_w_gate, l0_ffn1_w_up, l0_ffn1_w_down, l0_ffn1_post_g),
        ffn(l0_ffn2_pre_g, l0_ffn2_w_gate, l0_ffn2_w_up, l0_ffn2_w_down, l0_ffn2_post_g),
        ffn(l1_ffn1_pre_g, l1_ffn1_w_gate, l1_ffn1_w_up, l1_ffn1_w_down, l1_ffn1_post_g),
        ffn(l1_ffn2_pre_g, l1_ffn2_w_gate, l1_ffn2_w_up, l1_ffn2_w_down, l1_ffn2_post_g),
    )
    w_in, gq, gk, w_uq, w_ukv, wa, wb = _prep_mixer_weights(
        l0_w_in, l0_a_q_norm_g, l0_a_k_norm_g, l0_b_w_uq, l0_b_w_ukv, l0_w_out)
    mix0 = (vec(l0_mix_pre_g), w_in, gq, gk, vec(l0_b_cq_norm_g), w_uq, vec(l0_b_ckv_norm_g), w_ukv,
            wa, wb, vec(l0_mix_post_g))
    mix1 = (vec(l1_mix_pre_g), l1_pool_w.astype(BF16), vec(l1_pool_scale), vec(l1_mix_post_g))

    outs = []
    for x in (x_prompt, x_sample):
        tabs = _rope_tables(x.shape[1])
        outs.append(_trunk(x, ffn_params, mix0, mix1, tabs))
    return tuple(outs)
```

```python
import functools
import math

import jax
import jax.numpy as jnp
import numpy as np
from jax import lax
from jax.experimental import pallas as pl
from jax.experimental.pallas import tpu as pltpu

F32 = jnp.float32
BF16 = jnp.bfloat16

NORM_EPS = 1e-6
ROPE_BASE = 10000.0
GRID_W = 64
A_HEADS = 8
A_KV_HEADS = 2
A_HEAD_DIM = 128
B_HEADS = 8
B_Q_LORA = 512
B_KV_LORA = 256
B_NOPE_DIM = 128
B_ROPE_DIM = 64
B_V_DIM = 128
POOL_WINDOWS = (2, 4, 8, 16)
LANES = 128
LOG2_E = math.log2(math.e)
ATTN_CHUNK_UNROLL = 8
B_QK_PAD = 2 * LANES
ROW_CHUNK = 32
ROW_CHUNK_UNROLL = 4
FFN_SUB_ROWS = 256
POOL_HALO = 8

VMEM_LIMIT_BYTES = 60 * 1024 * 1024


def _cparams(*semantics):
    return pltpu.CompilerParams(dimension_semantics=semantics, vmem_limit_bytes=VMEM_LIMIT_BYTES)


def _rms(x, g):
    return x * lax.rsqrt(jnp.mean(x * x, axis=-1, keepdims=True) + NORM_EPS) * g


def _silu(x):
    return x * (1.0 / (1.0 + jnp.exp(-x)))


def _for_row_chunks(nrows, fn):
    def body(i, carry):
        fn(pl.ds(pl.multiple_of(i * ROW_CHUNK, ROW_CHUNK), ROW_CHUNK))
        return carry

    lax.fori_loop(0, nrows // ROW_CHUNK, body, 0, unroll=ROW_CHUNK_UNROLL)


def _ffn_kernel(x_ref, pre_g_ref, wg_ref, wu_ref, wd_ref, post_g_ref, o_ref, h_ref):
    f = pl.program_id(1)

    tm = x_ref.shape[0]

    @pl.when(f == 0)
    def _():
        def prologue(rows):
            h_ref[rows, :] = _rms(x_ref[rows, :], pre_g_ref[...]).astype(BF16)
            o_ref[rows, :] = jnp.zeros((ROW_CHUNK, o_ref.shape[1]), F32)

        _for_row_chunks(tm, prologue)

    for r in range(tm // FFN_SUB_ROWS):
        rows = slice(r * FFN_SUB_ROWS, (r + 1) * FFN_SUB_ROWS)
        h = h_ref[rows, :]
        gate = jnp.dot(h, wg_ref[...], preferred_element_type=F32)
        up = jnp.dot(h, wu_ref[...], preferred_element_type=F32)
        act = (_silu(gate) * up).astype(BF16)
        o_ref[rows, :] += jnp.dot(act, wd_ref[...], preferred_element_type=F32)

    @pl.when(f == pl.num_programs(1) - 1)
    def _():
        for c in range(tm // ROW_CHUNK):
            rows = slice(c * ROW_CHUNK, (c + 1) * ROW_CHUNK)
            o_ref[rows, :] = x_ref[rows, :] + 0.5 * _rms(o_ref[rows, :], post_g_ref[...])


def _ffn_half(x, pre_g, wg, wu, wd, post_g, *, tm=1024, tf=512):
    t, d = x.shape
    dff = wg.shape[1]
    row = lambda i, f: (i, 0)
    const = lambda i, f: (0, 0)
    return pl.pallas_call(
        _ffn_kernel,
        grid=(t // tm, dff // tf),
        in_specs=[
            pl.BlockSpec((tm, d), row),
            pl.BlockSpec((1, d), const),
            pl.BlockSpec((d, tf), lambda i, f: (0, f)),
            pl.BlockSpec((d, tf), lambda i, f: (0, f)),
            pl.BlockSpec((tf, d), lambda i, f: (f, 0)),
            pl.BlockSpec((1, d), const),
        ],
        out_specs=pl.BlockSpec((tm, d), row),
        out_shape=jax.ShapeDtypeStruct((t, d), F32),
        scratch_shapes=[pltpu.VMEM((tm, d), BF16)],
        compiler_params=_cparams("parallel", "arbitrary"),
        name="ffn_half",
    )(x, pre_g, wg, wu, wd, post_g)


def _rope(x, cos, sin):
    return x * cos + pltpu.roll(x, LANES // 2, 1) * sin


def _inproj_kernel(x_ref, pre_g_ref, w_in_ref, gq_ref, gk_ref, gcq_ref, w_uq_ref, gckv_ref, w_ukv_ref,
                   cos_a_ref, sin_a_ref, cos_b_ref, sin_b_ref,
                   qa_ref, ka_ref, va_ref, qb_ref, kb_ref, vb_ref):
    hn = _rms(x_ref[...], pre_g_ref[...]).astype(BF16)
    y = jnp.dot(hn, w_in_ref[...], preferred_element_type=F32)
    cos_a, sin_a = cos_a_ref[...], sin_a_ref[...]
    cos_b, sin_b = cos_b_ref[...], sin_b_ref[...]
    a_scale = A_HEAD_DIM ** -0.5 * LOG2_E
    b_scale = (B_NOPE_DIM + B_ROPE_DIM) ** -0.5 * LOG2_E
    ones = jnp.ones((x_ref.shape[0], LANES), BF16)
    q_cols = A_HEADS * A_HEAD_DIM
    kv_cols = A_KV_HEADS * A_HEAD_DIM

    for h in range(A_HEADS):
        sl = slice(h * LANES, (h + 1) * LANES)
        qa_ref[:, sl] = (_rope(_rms(y[:, sl], gq_ref[...]), cos_a, sin_a) * a_scale).astype(BF16)
    for h in range(A_KV_HEADS):
        sl = slice(h * LANES, (h + 1) * LANES)
        ka_ref[:, sl] = _rope(_rms(y[:, q_cols + h * LANES:q_cols + (h + 1) * LANES], gk_ref[...]),
                              cos_a, sin_a).astype(BF16)
    off = q_cols + kv_cols
    for h in range(A_KV_HEADS):
        va_ref[:, 2 * h * LANES:(2 * h + 1) * LANES] = y[:, off + h * LANES:off + (h + 1) * LANES].astype(BF16)
        va_ref[:, (2 * h + 1) * LANES:(2 * h + 2) * LANES] = ones
    off += kv_cols

    cq = _rms(y[:, off:off + B_Q_LORA], gcq_ref[...]).astype(BF16)
    off += B_Q_LORA
    ckv = _rms(y[:, off:off + B_KV_LORA], gckv_ref[...]).astype(BF16)
    off += B_KV_LORA
    k_rope = _rope(y[:, off:off + LANES], cos_b, sin_b).astype(BF16)

    qf = jnp.dot(cq, w_uq_ref[...], preferred_element_type=F32)
    kv = jnp.dot(ckv, w_ukv_ref[...], preferred_element_type=F32)
    for h in range(B_HEADS):
        base = h * B_QK_PAD
        qb_ref[:, base:base + LANES] = (qf[:, base:base + LANES] * b_scale).astype(BF16)
        qb_ref[:, base + LANES:base + B_QK_PAD] = (
            _rope(qf[:, base + LANES:base + B_QK_PAD], cos_b, sin_b) * b_scale).astype(BF16)
        kb_ref[:, base:base + LANES] = kv[:, h * LANES:(h + 1) * LANES].astype(BF16)
        kb_ref[:, base + LANES:base + B_QK_PAD] = k_rope
        v_col = B_HEADS * B_NOPE_DIM + h * LANES
        vb_ref[:, 2 * h * LANES:(2 * h + 1) * LANES] = kv[:, v_col:v_col + LANES].astype(BF16)
        vb_ref[:, (2 * h + 1) * LANES:(2 * h + 2) * LANES] = ones


def _mixer_inproj(x, seq, pre_g, w_in, gq, gk, gcq, w_uq, gckv, w_ukv, tabs, *, tm=256):
    t, d = x.shape
    nblk = seq // tm
    row = lambda i: (i, 0)
    const = lambda i: (0, 0)
    pos = lambda i: (i % nblk, 0)
    full = lambda a: pl.BlockSpec(a.shape, const)
    widths = (A_HEADS * A_HEAD_DIM, A_KV_HEADS * A_HEAD_DIM, 2 * A_KV_HEADS * A_HEAD_DIM,
              B_HEADS * B_QK_PAD, B_HEADS * B_QK_PAD, 2 * B_HEADS * B_V_DIM)
    return pl.pallas_call(
        _inproj_kernel,
        grid=(t // tm,),
        in_specs=[pl.BlockSpec((tm, d), row), full(pre_g), full(w_in), full(gq), full(gk), full(gcq),
                  full(w_uq), full(gckv), full(w_ukv)] + [pl.BlockSpec((tm, LANES), pos)] * 4,
        out_specs=[pl.BlockSpec((tm, w), row) for w in widths],
        out_shape=[jax.ShapeDtypeStruct((t, w), BF16) for w in widths],
        compiler_params=_cparams("parallel"),
        name="mixer_inproj",
    )(x, pre_g, w_in, gq, gk, gcq, w_uq, gckv, w_ukv, *tabs)


def _attn_kernel(q_ref, k_ref, v_ref, o_ref, *, group, dk, dv, tk, rb):
    tq = q_ref.shape[0]
    seq = k_ref.shape[0]
    per_head = tq // rb
    nblocks = group * per_head

    def q_block(i):
        j, r = divmod(i, per_head)
        return q_ref[r * rb:(r + 1) * rb, j * dk:(j + 1) * dk]

    def body(c, carry):
        ms, accs = carry
        start = pl.multiple_of(c * tk, tk)
        k = k_ref[pl.ds(start, tk), :]
        v = v_ref[pl.ds(start, tk), :]
        new_ms, new_accs = [], []
        for i in range(nblocks):
            s = lax.dot_general(q_block(i), k, (((1,), (1,)), ((), ())), preferred_element_type=F32)
            m_new = jnp.maximum(ms[i], jnp.max(s, axis=-1, keepdims=True))
            alpha = jnp.exp2(ms[i] - m_new)
            p = jnp.exp2(s - m_new).astype(BF16)
            new_accs.append(alpha * accs[i] + jnp.dot(p, v, preferred_element_type=F32))
            new_ms.append(m_new)
        return tuple(new_ms), tuple(new_accs)

    init = (tuple(jnp.full((rb, 1), -jnp.inf, F32) for _ in range(nblocks)),
            tuple(jnp.zeros((rb, 2 * dv), F32) for _ in range(nblocks)))
    _, accs = lax.fori_loop(0, seq // tk, body, init, unroll=ATTN_CHUNK_UNROLL)
    for i in range(nblocks):
        j, r = divmod(i, per_head)
        out = accs[i][:, :dv] * (1.0 / accs[i][:, dv:])
        o_ref[r * rb:(r + 1) * rb, j * dv:(j + 1) * dv] = out.astype(o_ref.dtype)


def _attention(q, k, v, *, kv_heads, group, dk, dv, tq, tk=512, rb=256):
    b, seq, _ = q.shape
    tq, tk = min(tq, seq), min(tk, seq)
    kern = functools.partial(_attn_kernel, group=group, dk=dk, dv=dv, tk=tk, rb=min(rb, tq))
    return pl.pallas_call(
        kern,
        grid=(b, kv_heads, seq // tq),
        in_specs=[
            pl.BlockSpec((None, tq, group * dk), lambda bi, h, i: (bi, i, h)),
            pl.BlockSpec((None, seq, dk), lambda bi, h, i: (bi, 0, h)),
            pl.BlockSpec((None, seq, 2 * dv), lambda bi, h, i: (bi, 0, h)),
        ],
        out_specs=pl.BlockSpec((None, tq, group * dv), lambda bi, h, i: (bi, i, h)),
        out_shape=jax.ShapeDtypeStruct((b, seq, kv_heads * group * dv), BF16),
        compiler_params=_cparams("parallel", "parallel", "arbitrary"),
        name=f"attention_g{group}_dk{dk}",
    )(q, k, v)


def _outproj_kernel(x_ref, oa_ref, ob_ref, wa_ref, wb_ref, post_g_ref, o_ref):
    m = jnp.dot(oa_ref[...], wa_ref[...], preferred_element_type=F32)
    m += jnp.dot(ob_ref[...], wb_ref[...], preferred_element_type=F32)
    o_ref[...] = x_ref[...] + _rms(m, post_g_ref[...])


def _mixer_outproj(x, oa, ob, wa, wb, post_g, *, tm=512):
    t, d = x.shape
    row = lambda i: (i, 0)
    const = lambda i: (0, 0)
    return pl.pallas_call(
        _outproj_kernel,
        grid=(t // tm,),
        in_specs=[pl.BlockSpec((tm, d), row), pl.BlockSpec((tm, oa.shape[1]), row),
                  pl.BlockSpec((tm, ob.shape[1]), row), pl.BlockSpec(wa.shape, const),
                  pl.BlockSpec(wb.shape, const), pl.BlockSpec((1, d), const)],
        out_specs=pl.BlockSpec((tm, d), row),
        out_shape=jax.ShapeDtypeStruct((t, d), F32),
        compiler_params=_cparams("parallel"),
        name="mixer_outproj",
    )(x, oa, ob, wa, wb, post_g)


def _pool_kernel(x_ref, prev_ref, next_ref, pre_g_ref, w_ref, scale_ref, post_g_ref, o_ref, ext_ref, m_ref,
                 *, seq):
    tm, d = x_ref.shape
    group = d // len(POOL_WINDOWS)
    nblk = seq // tm
    blk = pl.program_id(0) % nblk
    g = pre_g_ref[...]
    x = x_ref[...]
    ext_ref[0:POOL_HALO, :] = jnp.where(blk > 0, _rms(prev_ref[...], g), 0.0)
    ext_ref[POOL_HALO:POOL_HALO + tm, :] = _rms(x, g)
    ext_ref[POOL_HALO + tm:, :] = jnp.where(blk < nblk - 1, _rms(next_ref[...], g), 0.0)

    t = blk * tm + lax.broadcasted_iota(jnp.int32, (tm, 1), 0)
    for gi, w in enumerate(POOL_WINDOWS):
        cols = slice(gi * group, (gi + 1) * group)
        half = w // 2
        tot = ext_ref[POOL_HALO - half:POOL_HALO - half + tm, cols]
        for j in range(1, w):
            tot += ext_ref[POOL_HALO - half + j:POOL_HALO - half + j + tm, cols]
        cnt = (jnp.minimum(t + half, seq) - jnp.maximum(t - half, 0)).astype(F32)
        pooled = tot / cnt - ext_ref[POOL_HALO:POOL_HALO + tm, cols]
        m_ref[:, cols] = jnp.dot(pooled.astype(BF16), w_ref[gi], preferred_element_type=F32)
    o_ref[...] = x + _rms(m_ref[...] * scale_ref[...], post_g_ref[...])


def _pool_mixer(x, seq, pre_g, pool_w, pool_scale, post_g, *, tm=512):
    t, d = x.shape
    hb = tm // POOL_HALO
    last = t // POOL_HALO - 1
    row = lambda i: (i, 0)
    const = lambda i: (0, 0)
    return pl.pallas_call(
        functools.partial(_pool_kernel, seq=seq),
        grid=(t // tm,),
        in_specs=[
            pl.BlockSpec((tm, d), row),
            pl.BlockSpec((POOL_HALO, d), lambda i: (jnp.maximum(i * hb - 1, 0), 0)),
            pl.BlockSpec((POOL_HALO, d), lambda i: (jnp.minimum((i + 1) * hb, last), 0)),
            pl.BlockSpec((1, d), const),
            pl.BlockSpec(pool_w.shape, lambda i: (0, 0, 0)),
            pl.BlockSpec((1, d), const),
            pl.BlockSpec((1, d), const),
        ],
        out_specs=pl.BlockSpec((tm, d), row),
        out_shape=jax.ShapeDtypeStruct((t, d), F32),
        scratch_shapes=[pltpu.VMEM((tm + 2 * POOL_HALO, d), F32), pltpu.VMEM((tm, d), F32)],
        compiler_params=_cparams("parallel"),
        name="pool_mixer",
    )(x, x, x, pre_g, pool_w, pool_scale, post_g)


def _rope_tables(seq):
    rows = seq // GRID_W
    row = jnp.repeat(jnp.arange(rows, dtype=F32), GRID_W)
    col = jnp.tile(jnp.arange(GRID_W, dtype=F32), rows)

    def angles(rot_dim):
        half = rot_dim // 2
        freqs = ROPE_BASE ** (-jnp.arange(0, half, 2, dtype=F32) / half)
        return jnp.concatenate([row[:, None] * freqs, col[:, None] * freqs], axis=-1)

    ang_a = angles(A_HEAD_DIM)
    cos_a = jnp.concatenate([jnp.cos(ang_a)] * 2, axis=-1)
    sin_a = jnp.concatenate([-jnp.sin(ang_a), jnp.sin(ang_a)], axis=-1)
    ang_b = angles(B_ROPE_DIM)
    z = jnp.zeros_like(ang_b)
    cos_b = jnp.concatenate([jnp.cos(ang_b), z, jnp.cos(ang_b), z], axis=-1)
    sin_b = jnp.concatenate([-jnp.sin(ang_b), z, jnp.sin(ang_b), z], axis=-1)
    return cos_a, sin_a, cos_b, sin_b


def _pair_split_perm(n):
    return np.concatenate([np.arange(0, n, 2), np.arange(1, n, 2)])


def _spread_rope_cols(w):
    half = B_ROPE_DIM // 2
    z = jnp.zeros((w.shape[0], half), w.dtype)
    return jnp.concatenate([w[:, 0::2], z, w[:, 1::2], z], axis=-1)


def _prep_mixer_weights(w_in, gq, gk, w_uq, w_ukv, w_out):
    d = w_in.shape[0]
    perm = _pair_split_perm(A_HEAD_DIM)
    nq, nkv = A_HEADS * A_HEAD_DIM, A_KV_HEADS * A_HEAD_DIM
    wq = w_in[:, :nq].reshape(d, A_HEADS, A_HEAD_DIM)[:, :, perm].reshape(d, nq)
    wk = w_in[:, nq:nq + nkv].reshape(d, A_KV_HEADS, A_HEAD_DIM)[:, :, perm].reshape(d, nkv)
    rest = w_in[:, nq + nkv:-B_ROPE_DIM]
    w_in_p = jnp.concatenate([wq, wk, rest, _spread_rope_cols(w_in[:, -B_ROPE_DIM:])], axis=-1).astype(BF16)

    uq = w_uq.reshape(B_Q_LORA, B_HEADS, B_NOPE_DIM + B_ROPE_DIM)
    uq_rope = _spread_rope_cols(uq[:, :, B_NOPE_DIM:].reshape(B_Q_LORA * B_HEADS, B_ROPE_DIM))
    uq_p = jnp.concatenate([uq[:, :, :B_NOPE_DIM], uq_rope.reshape(B_Q_LORA, B_HEADS, LANES)], axis=-1)
    uq_p = uq_p.reshape(B_Q_LORA, B_HEADS * B_QK_PAD).astype(BF16)

    ukv = w_ukv.reshape(B_KV_LORA, B_HEADS, B_NOPE_DIM + B_V_DIM)
    ukv_p = jnp.concatenate([ukv[:, :, :B_NOPE_DIM].reshape(B_KV_LORA, -1),
                             ukv[:, :, B_NOPE_DIM:].reshape(B_KV_LORA, -1)], axis=-1).astype(BF16)
    na = A_HEADS * A_HEAD_DIM
    return (w_in_p, gq[perm][None, :], gk[perm][None, :], uq_p, ukv_p,
            w_out[:na].astype(BF16), w_out[na:].astype(BF16))


def _trunk(x, ffn_params, mix0, mix1, tabs):
    b, seq, d = x.shape
    x = x.reshape(b * seq, d)
    (pre0, w_in, gq, gk, gcq, w_uq, gckv, w_ukv, wa, wb, post0) = mix0
    (pre1, pool_w, pool_scale, post1) = mix1

    x = _ffn_half(x, *ffn_params[0])
    qa, ka, va, qb, kb, vb = _mixer_inproj(x, seq, pre0, w_in, gq, gk, gcq, w_uq, gckv, w_ukv, tabs)
    shp = lambda a: a.reshape(b, seq, a.shape[-1])
    oa = _attention(shp(qa), shp(ka), shp(va), kv_heads=A_KV_HEADS, group=A_HEADS // A_KV_HEADS,
                    dk=A_HEAD_DIM, dv=A_HEAD_DIM, tq=256)
    ob = _attention(shp(qb), shp(kb), shp(vb), kv_heads=B_HEADS, group=1, dk=B_QK_PAD, dv=B_V_DIM, tq=1024)
    x = _mixer_outproj(x, oa.reshape(b * seq, -1), ob.reshape(b * seq, -1), wa, wb, post0)
    x = _ffn_half(x, *ffn_params[1])
    x = _ffn_half(x, *ffn_params[2])
    x = _pool_mixer(x, seq, pre1, pool_w, pool_scale, post1)
    x = _ffn_half(x, *ffn_params[3])
    return x.reshape(b, seq, d)


def kernel(x_prompt, x_sample, l0_ffn1_pre_g, l0_ffn1_w_gate, l0_ffn1_w_up, l0_ffn1_w_down, l0_ffn1_post_g, l0_mix_pre_g, l0_w_in, l0_a_q_norm_g, l0_a_k_norm_g, l0_b_cq_norm_g, l0_b_w_uq, l0_b_ckv_norm_g, l0_b_w_ukv, l0_w_out, l0_mix_post_g, l0_ffn2_pre_g, l0_ffn2_w_gate, l0_ffn2_w_up, l0_ffn2_w_down, l0_ffn2_post_g, l1_ffn1_pre_g, l1_ffn1_w_gate, l1_ffn1_w_up, l1_ffn1_w_down, l1_ffn1_post_g, l1_mix_pre_g, l1_pool_w, l1_pool_scale, l1_mix_post_g, l1_ffn2_pre_g, l1_ffn2_w_gate, l1_ffn2_w_up, l1_ffn2_w_down, l1_ffn2_post_g):
    vec = lambda g: g[None, :]
    ffn = lambda pre, wg, wu, wd, post: (vec(pre), wg.astype(BF16), wu.astype(BF16), wd.astype(BF16), vec(post))
    ffn_params = (
        ffn(l0_ffn1_pre_g, l0_ffn1_w_gate, l0_ffn1_w_up, l0_ffn1_w_down, l0_ffn1_post_g),
        ffn(l0_ffn2_pre_g, l0_ffn2_w_gate, l0_ffn2_w_up, l0_ffn2_w_down, l0_ffn2_post_g),
        ffn(l1_ffn1_pre_g, l1_ffn1_w_gate, l1_ffn1_w_up, l1_ffn1_w_down, l1_ffn1_post_g),
        ffn(l1_ffn2_pre_g, l1_ffn2_w_gate, l1_ffn2_w_up, l1_ffn2_w_down, l1_ffn2_post_g),
    )
    w_in, gq, gk, w_uq, w_ukv, wa, wb = _prep_mixer_weights(
        l0_w_in, l0_a_q_norm_g, l0_a_k_norm_g, l0_b_w_uq, l0_b_w_ukv, l0_w_out)
    mix0 = (vec(l0_mix_pre_g), w_in, gq, gk, vec(l0_b_cq_norm_g), w_uq, vec(l0_b_ckv_norm_g), w_ukv,
            wa, wb, vec(l0_mix_post_g))
    mix1 = (vec(l1_mix_pre_g), l1_pool_w.astype(BF16), vec(l1_pool_scale), vec(l1_mix_post_g))

    outs = []
    for x in (x_prompt, x_sample):
        tabs = _rope_tables(x.shape[1])
        outs.append(_trunk(x, ffn_params, mix0, mix1, tabs))
    return tuple(outs)
```

```python
import functools
import math

import jax
import jax.numpy as jnp
import numpy as np
from jax import lax
from jax.experimental import pallas as pl
from jax.experimental.pallas import tpu as pltpu

F32 = jnp.float32
BF16 = jnp.bfloat16

NORM_EPS = 1e-6
ROPE_BASE = 10000.0
GRID_W = 64
A_HEADS = 8
A_KV_HEADS = 2
A_HEAD_DIM = 128
B_HEADS = 8
B_Q_LORA = 512
B_KV_LORA = 256
B_NOPE_DIM = 128
B_ROPE_DIM = 64
B_V_DIM = 128
POOL_WINDOWS = (2, 4, 8, 16)
LANES = 128
LOG2_E = math.log2(math.e)
ATTN_CHUNK_UNROLL = 16
B_QK_PAD = 2 * LANES
ROW_CHUNK = 32
ROW_CHUNK_UNROLL = 4
FFN_SUB_ROWS = 256
POOL_HALO = 8

VMEM_LIMIT_BYTES = 60 * 1024 * 1024


def _cparams(*semantics):
    return pltpu.CompilerParams(dimension_semantics=semantics, vmem_limit_bytes=VMEM_LIMIT_BYTES)


def _rms(x, g):
    return x * lax.rsqrt(jnp.mean(x * x, axis=-1, keepdims=True) + NORM_EPS) * g


def _silu(x):
    return x * (1.0 / (1.0 + jnp.exp(-x)))


def _for_row_chunks(nrows, fn):
    def body(i, carry):
        fn(pl.ds(pl.multiple_of(i * ROW_CHUNK, ROW_CHUNK), ROW_CHUNK))
        return carry

    lax.fori_loop(0, nrows // ROW_CHUNK, body, 0, unroll=ROW_CHUNK_UNROLL)


def _ffn_kernel(x_ref, pre_g_ref, wg_ref, wu_ref, wd_ref, post_g_ref, o_ref, h_ref):
    f = pl.program_id(1)

    tm = x_ref.shape[0]

    @pl.when(f == 0)
    def _():
        def prologue(rows):
            h_ref[rows, :] = _rms(x_ref[rows, :], pre_g_ref[...]).astype(BF16)
            o_ref[rows, :] = jnp.zeros((ROW_CHUNK, o_ref.shape[1]), F32)

        _for_row_chunks(tm, prologue)

    for r in range(tm // FFN_SUB_ROWS):
        rows = slice(r * FFN_SUB_ROWS, (r + 1) * FFN_SUB_ROWS)
        h = h_ref[rows, :]
        gate = jnp.dot(h, wg_ref[...], preferred_element_type=F32)
        up = jnp.dot(h, wu_ref[...], preferred_element_type=F32)
        act = (_silu(gate) * up).astype(BF16)
        o_ref[rows, :] += jnp.dot(act, wd_ref[...], preferred_element_type=F32)

    @pl.when(f == pl.num_programs(1) - 1)
    def _():
        for c in range(tm // ROW_CHUNK):
            rows = slice(c * ROW_CHUNK, (c + 1) * ROW_CHUNK)
            o_ref[rows, :] = x_ref[rows, :] + 0.5 * _rms(o_ref[rows, :], post_g_ref[...])


def _ffn_half(x, pre_g, wg, wu, wd, post_g, *, tm=1024, tf=512):
    t, d = x.shape
    dff = wg.shape[1]
    row = lambda i, f: (i, 0)
    const = lambda i, f: (0, 0)
    return pl.pallas_call(
        _ffn_kernel,
        grid=(t // tm, dff // tf),
        in_specs=[
            pl.BlockSpec((tm, d), row),
            pl.BlockSpec((1, d), const),
            pl.BlockSpec((d, tf), lambda i, f: (0, f)),
            pl.BlockSpec((d, tf), lambda i, f: (0, f)),
            pl.BlockSpec((tf, d), lambda i, f: (f, 0)),
            pl.BlockSpec((1, d), const),
        ],
        out_specs=pl.BlockSpec((tm, d), row),
        out_shape=jax.ShapeDtypeStruct((t, d), F32),
        scratch_shapes=[pltpu.VMEM((tm, d), BF16)],
        compiler_params=_cparams("parallel", "arbitrary"),
        name="ffn_half",
    )(x, pre_g, wg, wu, wd, post_g)


def _rope(x, cos, sin):
    return x * cos + pltpu.roll(x, LANES // 2, 1) * sin


def _inproj_kernel(x_ref, pre_g_ref, w_in_ref, gq_ref, gk_ref, gcq_ref, w_uq_ref, gckv_ref, w_ukv_ref,
                   cos_a_ref, sin_a_ref, cos_b_ref, sin_b_ref,
                   qa_ref, ka_ref, va_ref, qb_ref, kb_ref, vb_ref):
    hn = _rms(x_ref[...], pre_g_ref[...]).astype(BF16)
    y = jnp.dot(hn, w_in_ref[...], preferred_element_type=F32)
    cos_a, sin_a = cos_a_ref[...], sin_a_ref[...]
    cos_b, sin_b = cos_b_ref[...], sin_b_ref[...]
    a_scale = A_HEAD_DIM ** -0.5 * LOG2_E
    b_scale = (B_NOPE_DIM + B_ROPE_DIM) ** -0.5 * LOG2_E
    ones = jnp.ones((x_ref.shape[0], LANES), BF16)
    q_cols = A_HEADS * A_HEAD_DIM
    kv_cols = A_KV_HEADS * A_HEAD_DIM

    for h in range(A_HEADS):
        sl = slice(h * LANES, (h + 1) * LANES)
        qa_ref[:, sl] = (_rope(_rms(y[:, sl], gq_ref[...]), cos_a, sin_a) * a_scale).astype(BF16)
    for h in range(A_KV_HEADS):
        sl = slice(h * LANES, (h + 1) * LANES)
        ka_ref[:, sl] = _rope(_rms(y[:, q_cols + h * LANES:q_cols + (h + 1) * LANES], gk_ref[...]),
                              cos_a, sin_a).astype(BF16)
    off = q_cols + kv_cols
    for h in range(A_KV_HEADS):
        va_ref[:, 2 * h * LANES:(2 * h + 1) * LANES] = y[:, off + h * LANES:off + (h + 1) * LANES].astype(BF16)
        va_ref[:, (2 * h + 1) * LANES:(2 * h + 2) * LANES] = ones
    off += kv_cols

    cq = _rms(y[:, off:off + B_Q_LORA], gcq_ref[...]).astype(BF16)
    off += B_Q_LORA
    ckv = _rms(y[:, off:off + B_KV_LORA], gckv_ref[...]).astype(BF16)
    off += B_KV_LORA
    k_rope = _rope(y[:, off:off + LANES], cos_b, sin_b).astype(BF16)

    qf = jnp.dot(cq, w_uq_ref[...], preferred_element_type=F32)
    kv = jnp.dot(ckv, w_ukv_ref[...], preferred_element_type=F32)
    for h in range(B_HEADS):
        base = h * B_QK_PAD
        qb_ref[:, base:base + LANES] = (qf[:, base:base + LANES] * b_scale).astype(BF16)
        qb_ref[:, base + LANES:base + B_QK_PAD] = (
            _rope(qf[:, base + LANES:base + B_QK_PAD], cos_b, sin_b) * b_scale).astype(BF16)
        kb_ref[:, base:base + LANES] = kv[:, h * LANES:(h + 1) * LANES].astype(BF16)
        kb_ref[:, base + LANES:base + B_QK_PAD] = k_rope
        v_col = B_HEADS * B_NOPE_DIM + h * LANES
        vb_ref[:, 2 * h * LANES:(2 * h + 1) * LANES] = kv[:, v_col:v_col + LANES].astype(BF16)
        vb_ref[:, (2 * h + 1) * LANES:(2 * h + 2) * LANES] = ones


def _mixer_inproj(x, seq, pre_g, w_in, gq, gk, gcq, w_uq, gckv, w_ukv, tabs, *, tm=256):
    t, d = x.shape
    nblk = seq // tm
    row = lambda i: (i, 0)
    const = lambda i: (0, 0)
    pos = lambda i: (i % nblk, 0)
    full = lambda a: pl.BlockSpec(a.shape, const)
    widths = (A_HEADS * A_HEAD_DIM, A_KV_HEADS * A_HEAD_DIM, 2 * A_KV_HEADS * A_HEAD_DIM,
              B_HEADS * B_QK_PAD, B_HEADS * B_QK_PAD, 2 * B_HEADS * B_V_DIM)
    return pl.pallas_call(
        _inproj_kernel,
        grid=(t // tm,),
        in_specs=[pl.BlockSpec((tm, d), row), full(pre_g), full(w_in), full(gq), full(gk), full(gcq),
                  full(w_uq), full(gckv), full(w_ukv)] + [pl.BlockSpec((tm, LANES), pos)] * 4,
        out_specs=[pl.BlockSpec((tm, w), row) for w in widths],
        out_shape=[jax.ShapeDtypeStruct((t, w), BF16) for w in widths],
        compiler_params=_cparams("parallel"),
        name="mixer_inproj",
    )(x, pre_g, w_in, gq, gk, gcq, w_uq, gckv, w_ukv, *tabs)


def _attn_kernel(q_ref, k_ref, v_ref, o_ref, *, group, dk, dv, tk, rb):
    tq = q_ref.shape[0]
    seq = k_ref.shape[0]
    per_head = tq // rb
    nblocks = group * per_head

    def q_block(i):
        j, r = divmod(i, per_head)
        return q_ref[r * rb:(r + 1) * rb, j * dk:(j + 1) * dk]

    def body(c, carry):
        ms, accs = carry
        start = pl.multiple_of(c * tk, tk)
        k = k_ref[pl.ds(start, tk), :]
        v = v_ref[pl.ds(start, tk), :]
        new_ms, new_accs = [], []
        for i in range(nblocks):
            s = lax.dot_general(q_block(i), k, (((1,), (1,)), ((), ())), preferred_element_type=F32)
            m_new = jnp.maximum(ms[i], jnp.max(s, axis=-1, keepdims=True))
            alpha = jnp.exp2(ms[i] - m_new)
            p = jnp.exp2(s - m_new).astype(BF16)
            new_accs.append(alpha * accs[i] + jnp.dot(p, v, preferred_element_type=F32))
            new_ms.append(m_new)
        return tuple(new_ms), tuple(new_accs)

    init = (tuple(jnp.full((rb, 1), -jnp.inf, F32) for _ in range(nblocks)),
            tuple(jnp.zeros((rb, 2 * dv), F32) for _ in range(nblocks)))
    _, accs = lax.fori_loop(0, seq // tk, body, init, unroll=ATTN_CHUNK_UNROLL)
    for i in range(nblocks):
        j, r = divmod(i, per_head)
        out = accs[i][:, :dv] * (1.0 / accs[i][:, dv:])
        o_ref[r * rb:(r + 1) * rb, j * dv:(j + 1) * dv] = out.astype(o_ref.dtype)


def _attention(q, k, v, *, kv_heads, group, dk, dv, tq, tk=512, rb=256):
    b, seq, _ = q.shape
    tq, tk = min(tq, seq), min(tk, seq)
    kern = functools.partial(_attn_kernel, group=group, dk=dk, dv=dv, tk=tk, rb=min(rb, tq))
    return pl.pallas_call(
        kern,
        grid=(b, kv_heads, seq // tq),
        in_specs=[
            pl.BlockSpec((None, tq, group * dk), lambda bi, h, i: (bi, i, h)),
            pl.BlockSpec((None, seq, dk), lambda bi, h, i: (bi, 0, h)),
            pl.BlockSpec((None, seq, 2 * dv), lambda bi, h, i: (bi, 0, h)),
        ],
        out_specs=pl.BlockSpec((None, tq, group * dv), lambda bi, h, i: (bi, i, h)),
        out_shape=jax.ShapeDtypeStruct((b, seq, kv_heads * group * dv), BF16),
        compiler_params=_cparams("parallel", "parallel", "arbitrary"),
        name=f"attention_g{group}_dk{dk}",
    )(q, k, v)


def _outproj_kernel(x_ref, oa_ref, ob_ref, wa_ref, wb_ref, post_g_ref, o_ref):
    m = jnp.dot(oa_ref[...], wa_ref[...], preferred_element_type=F32)
    m += jnp.dot(ob_ref[...], wb_ref[...], preferred_element_type=F32)
    o_ref[...] = x_ref[...] + _rms(m, post_g_ref[...])


def _mixer_outproj(x, oa, ob, wa, wb, post_g, *, tm=512):
    t, d = x.shape
    row = lambda i: (i, 0)
    const = lambda i: (0, 0)
    return pl.pallas_call(
        _outproj_kernel,
        grid=(t // tm,),
        in_specs=[pl.BlockSpec((tm, d), row), pl.BlockSpec((tm, oa.shape[1]), row),
                  pl.BlockSpec((tm, ob.shape[1]), row), pl.BlockSpec(wa.shape, const),
                  pl.BlockSpec(wb.shape, const), pl.BlockSpec((1, d), const)],
        out_specs=pl.BlockSpec((tm, d), row),
        out_shape=jax.ShapeDtypeStruct((t, d), F32),
        compiler_params=_cparams("parallel"),
        name="mixer_outproj",
    )(x, oa, ob, wa, wb, post_g)


def _pool_kernel(x_ref, prev_ref, next_ref, pre_g_ref, w_ref, scale_ref, post_g_ref, o_ref, ext_ref, m_ref,
                 *, seq):
    tm, d = x_ref.shape
    group = d // len(POOL_WINDOWS)
    nblk = seq // tm
    blk = pl.program_id(0) % nblk
    g = pre_g_ref[...]
    x = x_ref[...]
    ext_ref[0:POOL_HALO, :] = jnp.where(blk > 0, _rms(prev_ref[...], g), 0.0)
    ext_ref[POOL_HALO:POOL_HALO + tm, :] = _rms(x, g)
    ext_ref[POOL_HALO + tm:, :] = jnp.where(blk < nblk - 1, _rms(next_ref[...], g), 0.0)

    t = blk * tm + lax.broadcasted_iota(jnp.int32, (tm, 1), 0)
    for gi, w in enumerate(POOL_WINDOWS):
        cols = slice(gi * group, (gi + 1) * group)
        half = w // 2
        tot = ext_ref[POOL_HALO - half:POOL_HALO - half + tm, cols]
        for j in range(1, w):
            tot += ext_ref[POOL_HALO - half + j:POOL_HALO - half + j + tm, cols]
        cnt = (jnp.minimum(t + half, seq) - jnp.maximum(t - half, 0)).astype(F32)
        pooled = tot / cnt - ext_ref[POOL_HALO:POOL_HALO + tm, cols]
        m_ref[:, cols] = jnp.dot(pooled.astype(BF16), w_ref[gi], preferred_element_type=F32)
    o_ref[...] = x + _rms(m_ref[...] * scale_ref[...], post_g_ref[...])


def _pool_mixer(x, seq, pre_g, pool_w, pool_scale, post_g, *, tm=512):
    t, d = x.shape
    hb = tm // POOL_HALO
    last = t // POOL_HALO - 1
    row = lambda i: (i, 0)
    const = lambda i: (0, 0)
    return pl.pallas_call(
        functools.partial(_pool_kernel, seq=seq),
        grid=(t // tm,),
        in_specs=[
            pl.BlockSpec((tm, d), row),
            pl.BlockSpec((POOL_HALO, d), lambda i: (jnp.maximum(i * hb - 1, 0), 0)),
            pl.BlockSpec((POOL_HALO, d), lambda i: (jnp.minimum((i + 1) * hb, last), 0)),
            pl.BlockSpec((1, d), const),
            pl.BlockSpec(pool_w.shape, lambda i: (0, 0, 0)),
            pl.BlockSpec((1, d), const),
            pl.BlockSpec((1, d), const),
        ],
        out_specs=pl.BlockSpec((tm, d), row),
        out_shape=jax.ShapeDtypeStruct((t, d), F32),
        scratch_shapes=[pltpu.VMEM((tm + 2 * POOL_HALO, d), F32), pltpu.VMEM((tm, d), F32)],
        compiler_params=_cparams("parallel"),
        name="pool_mixer",
    )(x, x, x, pre_g, pool_w, pool_scale, post_g)


def _rope_tables(seq):
    rows = seq // GRID_W
    row = jnp.repeat(jnp.arange(rows, dtype=F32), GRID_W)
    col = jnp.tile(jnp.arange(GRID_W, dtype=F32), rows)

    def angles(rot_dim):
        half = rot_dim // 2
        freqs = ROPE_BASE ** (-jnp.arange(0, half, 2, dtype=F32) / half)
        return jnp.concatenate([row[:, None] * freqs, col[:, None] * freqs], axis=-1)

    ang_a = angles(A_HEAD_DIM)
    cos_a = jnp.concatenate([jnp.cos(ang_a)] * 2, axis=-1)
    sin_a = jnp.concatenate([-jnp.sin(ang_a), jnp.sin(ang_a)], axis=-1)
    ang_b = angles(B_ROPE_DIM)
    z = jnp.zeros_like(ang_b)
    cos_b = jnp.concatenate([jnp.cos(ang_b), z, jnp.cos(ang_b), z], axis=-1)
    sin_b = jnp.concatenate([-jnp.sin(ang_b), z, jnp.sin(ang_b), z], axis=-1)
    return cos_a, sin_a, cos_b, sin_b


def _pair_split_perm(n):
    return np.concatenate([np.arange(0, n, 2), np.arange(1, n, 2)])


def _spread_rope_cols(w):
    half = B_ROPE_DIM // 2
    z = jnp.zeros((w.shape[0], half), w.dtype)
    return jnp.concatenate([w[:, 0::2], z, w[:, 1::2], z], axis=-1)


def _prep_mixer_weights(w_in, gq, gk, w_uq, w_ukv, w_out):
    d = w_in.shape[0]
    perm = _pair_split_perm(A_HEAD_DIM)
    nq, nkv = A_HEADS * A_HEAD_DIM, A_KV_HEADS * A_HEAD_DIM
    wq = w_in[:, :nq].reshape(d, A_HEADS, A_HEAD_DIM)[:, :, perm].reshape(d, nq)
    wk = w_in[:, nq:nq + nkv].reshape(d, A_KV_HEADS, A_HEAD_DIM)[:, :, perm].reshape(d, nkv)
    rest = w_in[:, nq + nkv:-B_ROPE_DIM]
    w_in_p = jnp.concatenate([wq, wk, rest, _spread_rope_cols(w_in[:, -B_ROPE_DIM:])], axis=-1).astype(BF16)

    uq = w_uq.reshape(B_Q_LORA, B_HEADS, B_NOPE_DIM + B_ROPE_DIM)
    uq_rope = _spread_rope_cols(uq[:, :, B_NOPE_DIM:].reshape(B_Q_LORA * B_HEADS, B_ROPE_DIM))
    uq_p = jnp.concatenate([uq[:, :, :B_NOPE_DIM], uq_rope.reshape(B_Q_LORA, B_HEADS, LANES)], axis=-1)
    uq_p = uq_p.reshape(B_Q_LORA, B_HEADS * B_QK_PAD).astype(BF16)

    ukv = w_ukv.reshape(B_KV_LORA, B_HEADS, B_NOPE_DIM + B_V_DIM)
    ukv_p = jnp.concatenate([ukv[:, :, :B_NOPE_DIM].reshape(B_KV_LORA, -1),
                             ukv[:, :, B_NOPE_DIM:].reshape(B_KV_LORA, -1)], axis=-1).astype(BF16)
    na = A_HEADS * A_HEAD_DIM
    return (w_in_p, gq[perm][None, :], gk[perm][None, :], uq_p, ukv_p,
            w_out[:na].astype(BF16), w_out[na:].astype(BF16))


def _trunk(x, ffn_params, mix0, mix1, tabs):
    b, seq, d = x.shape
    x = x.reshape(b * seq, d)
    (pre0, w_in, gq, gk, gcq, w_uq, gckv, w_ukv, wa, wb, post0) = mix0
    (pre1, pool_w, pool_scale, post1) = mix1

    x = _ffn_half(x, *ffn_params[0])
    qa, ka, va, qb, kb, vb = _mixer_inproj(x, seq, pre0, w_in, gq, gk, gcq, w_uq, gckv, w_ukv, tabs)
    shp = lambda a: a.reshape(b, seq, a.shape[-1])
    oa = _attention(shp(qa), shp(ka), shp(va), kv_heads=A_KV_HEADS, group=A_HEADS // A_KV_HEADS,
                    dk=A_HEAD_DIM, dv=A_HEAD_DIM, tq=256)
    ob = _attention(shp(qb), shp(kb), shp(vb), kv_heads=B_HEADS, group=1, dk=B_QK_PAD, dv=B_V_DIM, tq=1024)
    x = _mixer_outproj(x, oa.reshape(b * seq, -1), ob.reshape(b * seq, -1), wa, wb, post0)
    x = _ffn_half(x, *ffn_params[1])
    x = _ffn_half(x, *ffn_params[2])
    x = _pool_mixer(x, seq, pre1, pool_w, pool_scale, post1)
    x = _ffn_half(x, *ffn_params[3])
    return x.reshape(b, seq, d)


def kernel(x_prompt, x_sample, l0_ffn1_pre_g, l0_ffn1_w_gate, l0_ffn1_w_up, l0_ffn1_w_down, l0_ffn1_post_g, l0_mix_pre_g, l0_w_in, l0_a_q_norm_g, l0_a_k_norm_g, l0_b_cq_norm_g, l0_b_w_uq, l0_b_ckv_norm_g, l0_b_w_ukv, l0_w_out, l0_mix_post_g, l0_ffn2_pre_g, l0_ffn2_w_gate, l0_ffn2_w_up, l0_ffn2_w_down, l0_ffn2_post_g, l1_ffn1_pre_g, l1_ffn1_w_gate, l1_ffn1_w_up, l1_ffn1_w_down, l1_ffn1_post_g, l1_mix_pre_g, l1_pool_w, l1_pool_scale, l1_mix_post_g, l1_ffn2_pre_g, l1_ffn2_w_gate, l1_ffn2_w_up, l1_ffn2_w_down, l1_ffn2_post_g):
    vec = lambda g: g[None, :]
    ffn = lambda pre, wg, wu, wd, post: (vec(pre), wg.astype(BF16), wu.astype(BF16), wd.astype(BF16), vec(post))
    ffn_params = (
        ffn(l0_ffn1_pre_g, l0_ffn1_w_gate, l0_ffn1_w_up, l0_ffn1_w_down, l0_ffn1_post_g),
        ffn(l0_ffn2_pre_g, l0_ffn2_w_gate, l0_ffn2_w_up, l0_ffn2_w_down, l0_ffn2_post_g),
        ffn(l1_ffn1_pre_g, l1_ffn1_w_gate, l1_ffn1_w_up, l1_ffn1_w_down, l1_ffn1_post_g),
        ffn(l1_ffn2_pre_g, l1_ffn2_w_gate, l1_ffn2_w_up, l1_ffn2_w_down, l1_ffn2_post_g),
    )
    w_in, gq, gk, w_uq, w_ukv, wa, wb = _prep_mixer_weights(
        l0_w_in, l0_a_q_norm_g, l0_a_k_norm_g, l0_b_w_uq, l0_b_w_ukv, l0_w_out)
    mix0 = (vec(l0_mix_pre_g), w_in, gq, gk, vec(l0_b_cq_norm_g), w_uq, vec(l0_b_ckv_norm_g), w_ukv,
            wa, wb, vec(l0_mix_post_g))
    mix1 = (vec(l1_mix_pre_g), l1_pool_w.astype(BF16), vec(l1_pool_scale), vec(l1_mix_post_g))

    outs = []
    for x in (x_prompt, x_sample):
        tabs = _rope_tables(x.shape[1])
        outs.append(_trunk(x, ffn_params, mix0, mix1, tabs))
    return tuple(outs)
```

```python
import functools
import math

import jax
import jax.numpy as jnp
import numpy as np
from jax import lax
from jax.experimental import pallas as pl
from jax.experimental.pallas import tpu as pltpu

F32 = jnp.float32
BF16 = jnp.bfloat16

NORM_EPS = 1e-6
ROPE_BASE = 10000.0
GRID_W = 64
A_HEADS = 8
A_KV_HEADS = 2
A_HEAD_DIM = 128
B_HEADS = 8
B_Q_LORA = 512
B_KV_LORA = 256
B_NOPE_DIM = 128
B_ROPE_DIM = 64
B_V_DIM = 128
POOL_WINDOWS = (2, 4, 8, 16)
LANES = 128
LOG2_E = math.log2(math.e)
ATTN_CHUNK_UNROLL = 16
B_QK_PAD = 2 * LANES
ROW_CHUNK = 32
ROW_CHUNK_UNROLL = 4
FFN_SUB_ROWS = 256
POOL_HALO = 8

VMEM_LIMIT_BYTES = 60 * 1024 * 1024


def _cparams(*semantics):
    return pltpu.CompilerParams(dimension_semantics=semantics, vmem_limit_bytes=VMEM_LIMIT_BYTES)


def _rms(x, g):
    return x * lax.rsqrt(jnp.mean(x * x, axis=-1, keepdims=True) + NORM_EPS) * g


def _silu(x):
    return x * (1.0 / (1.0 + jnp.exp(-x)))


def _for_row_chunks(nrows, fn):
    def body(i, carry):
        fn(pl.ds(pl.multiple_of(i * ROW_CHUNK, ROW_CHUNK), ROW_CHUNK))
        return carry

    lax.fori_loop(0, nrows // ROW_CHUNK, body, 0, unroll=ROW_CHUNK_UNROLL)


def _ffn_step(x_ref, pre_g_ref, wg_ref, wu_ref, wd_ref, post_g_ref, o_ref, h_ref, *, first, last):
    tm = x_ref.shape[0]
    for r in range(tm // FFN_SUB_ROWS):
        base = r * FFN_SUB_ROWS
        rows = slice(base, base + FFN_SUB_ROWS)
        if first:
            for c in range(FFN_SUB_ROWS // ROW_CHUNK):
                chunk = slice(base + c * ROW_CHUNK, base + (c + 1) * ROW_CHUNK)
                h_ref[chunk, :] = _rms(x_ref[chunk, :], pre_g_ref[...]).astype(BF16)
        h = h_ref[rows, :]
        gate = jnp.dot(h, wg_ref[...], preferred_element_type=F32)
        up = jnp.dot(h, wu_ref[...], preferred_element_type=F32)
        act = (_silu(gate) * up).astype(BF16)
        down = jnp.dot(act, wd_ref[...], preferred_element_type=F32)
        if first:
            o_ref[rows, :] = down
        else:
            o_ref[rows, :] += down
        if last:
            for c in range(FFN_SUB_ROWS // ROW_CHUNK):
                chunk = slice(base + c * ROW_CHUNK, base + (c + 1) * ROW_CHUNK)
                o_ref[chunk, :] = x_ref[chunk, :] + 0.5 * _rms(o_ref[chunk, :], post_g_ref[...])


def _ffn_kernel(*refs):
    f = pl.program_id(1)
    nf = pl.num_programs(1)
    pl.when(f == 0)(functools.partial(_ffn_step, *refs, first=True, last=False))
    pl.when(jnp.logical_and(f > 0, f < nf - 1))(functools.partial(_ffn_step, *refs, first=False, last=False))
    pl.when(f == nf - 1)(functools.partial(_ffn_step, *refs, first=False, last=True))


def _ffn_half(x, pre_g, wg, wu, wd, post_g, *, tm=1024, tf=512):
    t, d = x.shape
    dff = wg.shape[1]
    row = lambda i, f: (i, 0)
    const = lambda i, f: (0, 0)
    return pl.pallas_call(
        _ffn_kernel,
        grid=(t // tm, dff // tf),
        in_specs=[
            pl.BlockSpec((tm, d), row),
            pl.BlockSpec((1, d), const),
            pl.BlockSpec((d, tf), lambda i, f: (0, f)),
            pl.BlockSpec((d, tf), lambda i, f: (0, f)),
            pl.BlockSpec((tf, d), lambda i, f: (f, 0)),
            pl.BlockSpec((1, d), const),
        ],
        out_specs=pl.BlockSpec((tm, d), row),
        out_shape=jax.ShapeDtypeStruct((t, d), F32),
        scratch_shapes=[pltpu.VMEM((tm, d), BF16)],
        compiler_params=_cparams("parallel", "arbitrary"),
        name="ffn_half",
    )(x, pre_g, wg, wu, wd, post_g)


def _rope(x, cos, sin):
    return x * cos + pltpu.roll(x, LANES // 2, 1) * sin


def _inproj_kernel(x_ref, pre_g_ref, w_in_ref, gq_ref, gk_ref, gcq_ref, w_uq_ref, gckv_ref, w_ukv_ref,
                   cos_a_ref, sin_a_ref, cos_b_ref, sin_b_ref,
                   qa_ref, ka_ref, va_ref, qb_ref, kb_ref, vb_ref):
    hn = _rms(x_ref[...], pre_g_ref[...]).astype(BF16)
    y = jnp.dot(hn, w_in_ref[...], preferred_element_type=F32)
    cos_a, sin_a = cos_a_ref[...], sin_a_ref[...]
    cos_b, sin_b = cos_b_ref[...], sin_b_ref[...]
    a_scale = A_HEAD_DIM ** -0.5 * LOG2_E
    b_scale = (B_NOPE_DIM + B_ROPE_DIM) ** -0.5 * LOG2_E
    ones = jnp.ones((x_ref.shape[0], LANES), BF16)
    q_cols = A_HEADS * A_HEAD_DIM
    kv_cols = A_KV_HEADS * A_HEAD_DIM

    for h in range(A_HEADS):
        sl = slice(h * LANES, (h + 1) * LANES)
        qa_ref[:, sl] = (_rope(_rms(y[:, sl], gq_ref[...]), cos_a, sin_a) * a_scale).astype(BF16)
    for h in range(A_KV_HEADS):
        sl = slice(h * LANES, (h + 1) * LANES)
        ka_ref[:, sl] = _rope(_rms(y[:, q_cols + h * LANES:q_cols + (h + 1) * LANES], gk_ref[...]),
                              cos_a, sin_a).astype(BF16)
    off = q_cols + kv_cols
    for h in range(A_KV_HEADS):
        va_ref[:, 2 * h * LANES:(2 * h + 1) * LANES] = y[:, off + h * LANES:off + (h + 1) * LANES].astype(BF16)
        va_ref[:, (2 * h + 1) * LANES:(2 * h + 2) * LANES] = ones
    off += kv_cols

    cq = _rms(y[:, off:off + B_Q_LORA], gcq_ref[...]).astype(BF16)
    off += B_Q_LORA
    ckv = _rms(y[:, off:off + B_KV_LORA], gckv_ref[...]).astype(BF16)
    off += B_KV_LORA
    k_rope = _rope(y[:, off:off + LANES], cos_b, sin_b).astype(BF16)

    qf = jnp.dot(cq, w_uq_ref[...], preferred_element_type=F32)
    kv = jnp.dot(ckv, w_ukv_ref[...], preferred_element_type=F32)
    for h in range(B_HEADS):
        base = h * B_QK_PAD
        qb_ref[:, base:base + LANES] = (qf[:, base:base + LANES] * b_scale).astype(BF16)
        qb_ref[:, base + LANES:base + B_QK_PAD] = (
            _rope(qf[:, base + LANES:base + B_QK_PAD], cos_b, sin_b) * b_scale).astype(BF16)
        kb_ref[:, base:base + LANES] = kv[:, h * LANES:(h + 1) * LANES].astype(BF16)
        kb_ref[:, base + LANES:base + B_QK_PAD] = k_rope
        v_col = B_HEADS * B_NOPE_DIM + h * LANES
        vb_ref[:, 2 * h * LANES:(2 * h + 1) * LANES] = kv[:, v_col:v_col + LANES].astype(BF16)
        vb_ref[:, (2 * h + 1) * LANES:(2 * h + 2) * LANES] = ones


def _mixer_inproj(x, seq, pre_g, w_in, gq, gk, gcq, w_uq, gckv, w_ukv, tabs, *, tm=256):
    t, d = x.shape
    nblk = seq // tm
    row = lambda i: (i, 0)
    const = lambda i: (0, 0)
    pos = lambda i: (i % nblk, 0)
    full = lambda a: pl.BlockSpec(a.shape, const)
    widths = (A_HEADS * A_HEAD_DIM, A_KV_HEADS * A_HEAD_DIM, 2 * A_KV_HEADS * A_HEAD_DIM,
              B_HEADS * B_QK_PAD, B_HEADS * B_QK_PAD, 2 * B_HEADS * B_V_DIM)
    return pl.pallas_call(
        _inproj_kernel,
        grid=(t // tm,),
        in_specs=[pl.BlockSpec((tm, d), row), full(pre_g), full(w_in), full(gq), full(gk), full(gcq),
                  full(w_uq), full(gckv), full(w_ukv)] + [pl.BlockSpec((tm, LANES), pos)] * 4,
        out_specs=[pl.BlockSpec((tm, w), row) for w in widths],
        out_shape=[jax.ShapeDtypeStruct((t, w), BF16) for w in widths],
        compiler_params=_cparams("parallel"),
        name="mixer_inproj",
    )(x, pre_g, w_in, gq, gk, gcq, w_uq, gckv, w_ukv, *tabs)


def _attn_kernel(q_ref, k_ref, v_ref, o_ref, *, group, dk, dv, tk, rb):
    tq = q_ref.shape[0]
    seq = k_ref.shape[0]
    per_head = tq // rb
    nblocks = group * per_head

    def q_block(i):
        j, r = divmod(i, per_head)
        return q_ref[r * rb:(r + 1) * rb, j * dk:(j + 1) * dk]

    def body(c, carry):
        ms, accs = carry
        start = pl.multiple_of(c * tk, tk)
        k = k_ref[pl.ds(start, tk), :]
        v = v_ref[pl.ds(start, tk), :]
        new_ms, new_accs = [], []
        for i in range(nblocks):
            s = lax.dot_general(q_block(i), k, (((1,), (1,)), ((), ())), preferred_element_type=F32)
            m_new = jnp.maximum(ms[i], jnp.max(s, axis=-1, keepdims=True))
            alpha = jnp.exp2(ms[i] - m_new)
            p = jnp.exp2(s - m_new).astype(BF16)
            new_accs.append(alpha * accs[i] + jnp.dot(p, v, preferred_element_type=F32))
            new_ms.append(m_new)
        return tuple(new_ms), tuple(new_accs)

    init = (tuple(jnp.full((rb, 1), -jnp.inf, F32) for _ in range(nblocks)),
            tuple(jnp.zeros((rb, 2 * dv), F32) for _ in range(nblocks)))
    _, accs = lax.fori_loop(0, seq // tk, body, init, unroll=ATTN_CHUNK_UNROLL)
    for i in range(nblocks):
        j, r = divmod(i, per_head)
        out = accs[i][:, :dv] * (1.0 / accs[i][:, dv:])
        o_ref[r * rb:(r + 1) * rb, j * dv:(j + 1) * dv] = out.astype(o_ref.dtype)


def _attention(q, k, v, *, kv_heads, group, dk, dv, tq, tk=512, rb=256):
    b, seq, _ = q.shape
    tq, tk = min(tq, seq), min(tk, seq)
    kern = functools.partial(_attn_kernel, group=group, dk=dk, dv=dv, tk=tk, rb=min(rb, tq))
    return pl.pallas_call(
        kern,
        grid=(b, kv_heads, seq // tq),
        in_specs=[
            pl.BlockSpec((None, tq, group * dk), lambda bi, h, i: (bi, i, h)),
            pl.BlockSpec((None, seq, dk), lambda bi, h, i: (bi, 0, h)),
            pl.BlockSpec((None, seq, 2 * dv), lambda bi, h, i: (bi, 0, h)),
        ],
        out_specs=pl.BlockSpec((None, tq, group * dv), lambda bi, h, i: (bi, i, h)),
        out_shape=jax.ShapeDtypeStruct((b, seq, kv_heads * group * dv), BF16),
        compiler_params=_cparams("parallel", "parallel", "arbitrary"),
        name=f"attention_g{group}_dk{dk}",
    )(q, k, v)


def _outproj_kernel(x_ref, oa_ref, ob_ref, wa_ref, wb_ref, post_g_ref, o_ref):
    m = jnp.dot(oa_ref[...], wa_ref[...], preferred_element_type=F32)
    m += jnp.dot(ob_ref[...], wb_ref[...], preferred_element_type=F32)
    o_ref[...] = x_ref[...] + _rms(m, post_g_ref[...])


def _mixer_outproj(x, oa, ob, wa, wb, post_g, *, tm=512):
    t, d = x.shape
    row = lambda i: (i, 0)
    const = lambda i: (0, 0)
    return pl.pallas_call(
        _outproj_kernel,
        grid=(t // tm,),
        in_specs=[pl.BlockSpec((tm, d), row), pl.BlockSpec((tm, oa.shape[1]), row),
                  pl.BlockSpec((tm, ob.shape[1]), row), pl.BlockSpec(wa.shape, const),
                  pl.BlockSpec(wb.shape, const), pl.BlockSpec((1, d), const)],
        out_specs=pl.BlockSpec((tm, d), row),
        out_shape=jax.ShapeDtypeStruct((t, d), F32),
        compiler_params=_cparams("parallel"),
        name="mixer_outproj",
    )(x, oa, ob, wa, wb, post_g)


def _pool_kernel(x_ref, prev_ref, next_ref, pre_g_ref, w_ref, scale_ref, post_g_ref, o_ref, ext_ref, m_ref,
                 *, seq):
    tm, d = x_ref.shape
    group = d // len(POOL_WINDOWS)
    nblk = seq // tm
    blk = pl.program_id(0) % nblk
    g = pre_g_ref[...]
    x = x_ref[...]
    ext_ref[0:POOL_HALO, :] = jnp.where(blk > 0, _rms(prev_ref[...], g), 0.0)
    ext_ref[POOL_HALO:POOL_HALO + tm, :] = _rms(x, g)
    ext_ref[POOL_HALO + tm:, :] = jnp.where(blk < nblk - 1, _rms(next_ref[...], g), 0.0)

    t = blk * tm + lax.broadcasted_iota(jnp.int32, (tm, 1), 0)
    for gi, w in enumerate(POOL_WINDOWS):
        cols = slice(gi * group, (gi + 1) * group)
        half = w // 2
        tot = ext_ref[POOL_HALO - half:POOL_HALO - half + tm, cols]
        for j in range(1, w):
            tot += ext_ref[POOL_HALO - half + j:POOL_HALO - half + j + tm, cols]
        cnt = (jnp.minimum(t + half, seq) - jnp.maximum(t - half, 0)).astype(F32)
        pooled = tot / cnt - ext_ref[POOL_HALO:POOL_HALO + tm, cols]
        m_ref[:, cols] = jnp.dot(pooled.astype(BF16), w_ref[gi], preferred_element_type=F32)
    o_ref[...] = x + _rms(m_ref[...] * scale_ref[...], post_g_ref[...])


def _pool_mixer(x, seq, pre_g, pool_w, pool_scale, post_g, *, tm=512):
    t, d = x.shape
    hb = tm // POOL_HALO
    last = t // POOL_HALO - 1
    row = lambda i: (i, 0)
    const = lambda i: (0, 0)
    return pl.pallas_call(
        functools.partial(_pool_kernel, seq=seq),
        grid=(t // tm,),
        in_specs=[
            pl.BlockSpec((tm, d), row),
            pl.BlockSpec((POOL_HALO, d), lambda i: (jnp.maximum(i * hb - 1, 0), 0)),
            pl.BlockSpec((POOL_HALO, d), lambda i: (jnp.minimum((i + 1) * hb, last), 0)),
            pl.BlockSpec((1, d), const),
            pl.BlockSpec(pool_w.shape, lambda i: (0, 0, 0)),
            pl.BlockSpec((1, d), const),
            pl.BlockSpec((1, d), const),
        ],
        out_specs=pl.BlockSpec((tm, d), row),
        out_shape=jax.ShapeDtypeStruct((t, d), F32),
        scratch_shapes=[pltpu.VMEM((tm + 2 * POOL_HALO, d), F32), pltpu.VMEM((tm, d), F32)],
        compiler_params=_cparams("parallel"),
        name="pool_mixer",
    )(x, x, x, pre_g, pool_w, pool_scale, post_g)


def _rope_tables(seq):
    rows = seq // GRID_W
    row = jnp.repeat(jnp.arange(rows, dtype=F32), GRID_W)
    col = jnp.tile(jnp.arange(GRID_W, dtype=F32), rows)

    def angles(rot_dim):
        half = rot_dim // 2
        freqs = ROPE_BASE ** (-jnp.arange(0, half, 2, dtype=F32) / half)
        return jnp.concatenate([row[:, None] * freqs, col[:, None] * freqs], axis=-1)

    ang_a = angles(A_HEAD_DIM)
    cos_a = jnp.concatenate([jnp.cos(ang_a)] * 2, axis=-1)
    sin_a = jnp.concatenate([-jnp.sin(ang_a), jnp.sin(ang_a)], axis=-1)
    ang_b = angles(B_ROPE_DIM)
    z = jnp.zeros_like(ang_b)
    cos_b = jnp.concatenate([jnp.cos(ang_b), z, jnp.cos(ang_b), z], axis=-1)
    sin_b = jnp.concatenate([-jnp.sin(ang_b), z, jnp.sin(ang_b), z], axis=-1)
    return cos_a, sin_a, cos_b, sin_b


def _pair_split_perm(n):
    return np.concatenate([np.arange(0, n, 2), np.arange(1, n, 2)])


def _spread_rope_cols(w):
    half = B_ROPE_DIM // 2
    z = jnp.zeros((w.shape[0], half), w.dtype)
    return jnp.concatenate([w[:, 0::2], z, w[:, 1::2], z], axis=-1)


def _prep_mixer_weights(w_in, gq, gk, w_uq, w_ukv, w_out):
    d = w_in.shape[0]
    perm = _pair_split_perm(A_HEAD_DIM)
    nq, nkv = A_HEADS * A_HEAD_DIM, A_KV_HEADS * A_HEAD_DIM
    wq = w_in[:, :nq].reshape(d, A_HEADS, A_HEAD_DIM)[:, :, perm].reshape(d, nq)
    wk = w_in[:, nq:nq + nkv].reshape(d, A_KV_HEADS, A_HEAD_DIM)[:, :, perm].reshape(d, nkv)
    rest = w_in[:, nq + nkv:-B_ROPE_DIM]
    w_in_p = jnp.concatenate([wq, wk, rest, _spread_rope_cols(w_in[:, -B_ROPE_DIM:])], axis=-1).astype(BF16)

    uq = w_uq.reshape(B_Q_LORA, B_HEADS, B_NOPE_DIM + B_ROPE_DIM)
    uq_rope = _spread_rope_cols(uq[:, :, B_NOPE_DIM:].reshape(B_Q_LORA * B_HEADS, B_ROPE_DIM))
    uq_p = jnp.concatenate([uq[:, :, :B_NOPE_DIM], uq_rope.reshape(B_Q_LORA, B_HEADS, LANES)], axis=-1)
    uq_p = uq_p.reshape(B_Q_LORA, B_HEADS * B_QK_PAD).astype(BF16)

    ukv = w_ukv.reshape(B_KV_LORA, B_HEADS, B_NOPE_DIM + B_V_DIM)
    ukv_p = jnp.concatenate([ukv[:, :, :B_NOPE_DIM].reshape(B_KV_LORA, -1),
                             ukv[:, :, B_NOPE_DIM:].reshape(B_KV_LORA, -1)], axis=-1).astype(BF16)
    na = A_HEADS * A_HEAD_DIM
    return (w_in_p, gq[perm][None, :], gk[perm][None, :], uq_p, ukv_p,
            w_out[:na].astype(BF16), w_out[na:].astype(BF16))


def _trunk(x, ffn_params, mix0, mix1, tabs):
    b, seq, d = x.shape
    x = x.reshape(b * seq, d)
    (pre0, w_in, gq, gk, gcq, w_uq, gckv, w_ukv, wa, wb, post0) = mix0
    (pre1, pool_w, pool_scale, post1) = mix1

    x = _ffn_half(x, *ffn_params[0])
    qa, ka, va, qb, kb, vb = _mixer_inproj(x, seq, pre0, w_in, gq, gk, gcq, w_uq, gckv, w_ukv, tabs)
    shp = lambda a: a.reshape(b, seq, a.shape[-1])
    oa = _attention(shp(qa), shp(ka), shp(va), kv_heads=A_KV_HEADS, group=A_HEADS // A_KV_HEADS,
                    dk=A_HEAD_DIM, dv=A_HEAD_DIM, tq=256)
    ob = _attention(shp(qb), shp(kb), shp(vb), kv_heads=B_HEADS, group=1, dk=B_QK_PAD, dv=B_V_DIM, tq=1024)
    x = _mixer_outproj(x, oa.reshape(b * seq, -1), ob.reshape(b * seq, -1), wa, wb, post0)
    x = _ffn_half(x, *ffn_params[1])
    x = _ffn_half(x, *ffn_params[2])
    x = _pool_mixer(x, seq, pre1, pool_w, pool_scale, post1)
    x = _ffn_half(x, *ffn_params[3])
    return x.reshape(b, seq, d)


def kernel(x_prompt, x_sample, l0_ffn1_pre_g, l0_ffn1_w_gate, l0_ffn1_w_up, l0_ffn1_w_down, l0_ffn1_post_g, l0_mix_pre_g, l0_w_in, l0_a_q_norm_g, l0_a_k_norm_g, l0_b_cq_norm_g, l0_b_w_uq, l0_b_ckv_norm_g, l0_b_w_ukv, l0_w_out, l0_mix_post_g, l0_ffn2_pre_g, l0_ffn2_w_gate, l0_ffn2_w_up, l0_ffn2_w_down, l0_ffn2_post_g, l1_ffn1_pre_g, l1_ffn1_w_gate, l1_ffn1_w_up, l1_ffn1_w_down, l1_ffn1_post_g, l1_mix_pre_g, l1_pool_w, l1_pool_scale, l1_mix_post_g, l1_ffn2_pre_g, l1_ffn2_w_gate, l1_ffn2_w_up, l1_ffn2_w_down, l1_ffn2_post_g):
    vec = lambda g: g[None, :]
    ffn = lambda pre, wg, wu, wd, post: (vec(pre), wg.astype(BF16), wu.astype(BF16), wd.astype(BF16), vec(post))
    ffn_params = (
        ffn(l0_ffn1_pre_g, l0_ffn1_w_gate, l0_ffn1_w_up, l0_ffn1_w_down, l0_ffn1_post_g),
        ffn(l0_ffn2_pre_g, l0_ffn2_w_gate, l0_ffn2_w_up, l0_ffn2_w_down, l0_ffn2_post_g),
        ffn(l1_ffn1_pre_g, l1_ffn1_w_gate, l1_ffn1_w_up, l1_ffn1_w_down, l1_ffn1_post_g),
        ffn(l1_ffn2_pre_g, l1_ffn2_w_gate, l1_ffn2_w_up, l1_ffn2_w_down, l1_ffn2_post_g),
    )
    w_in, gq, gk, w_uq, w_ukv, wa, wb = _prep_mixer_weights(
        l0_w_in, l0_a_q_norm_g, l0_a_k_norm_g, l0_b_w_uq, l0_b_w_ukv, l0_w_out)
    mix0 = (vec(l0_mix_pre_g), w_in, gq, gk, vec(l0_b_cq_norm_g), w_uq, vec(l0_b_ckv_norm_g), w_ukv,
            wa, wb, vec(l0_mix_post_g))
    mix1 = (vec(l1_mix_pre_g), l1_pool_w.astype(BF16), vec(l1_pool_scale), vec(l1_mix_post_g))

    outs = []
    for x in (x_prompt, x_sample):
        tabs = _rope_tables(x.shape[1])
        outs.append(_trunk(x, ffn_params, mix0, mix1, tabs))
    return tuple(outs)
```

```python
import functools
import math

import jax
import jax.numpy as jnp
import numpy as np
from jax import lax
from jax.experimental import pallas as pl
from jax.experimental.pallas import tpu as pltpu

F32 = jnp.float32
BF16 = jnp.bfloat16

NORM_EPS = 1e-6
ROPE_BASE = 10000.0
GRID_W = 64
A_HEADS = 8
A_KV_HEADS = 2
A_HEAD_DIM = 128
B_HEADS = 8
B_Q_LORA = 512
B_KV_LORA = 256
B_NOPE_DIM = 128
B_ROPE_DIM = 64
B_V_DIM = 128
POOL_WINDOWS = (2, 4, 8, 16)
LANES = 128
LOG2_E = math.log2(math.e)
ATTN_CHUNK_UNROLL = 16
B_QK_PAD = 2 * LANES
ROW_CHUNK = 32
ROW_CHUNK_UNROLL = 4
FFN_DFF_CHUNK = 512
FFN_SUB_ROWS = 256
POOL_HALO = 8

VMEM_LIMIT_BYTES = 60 * 1024 * 1024


def _cparams(*semantics):
    return pltpu.CompilerParams(dimension_semantics=semantics, vmem_limit_bytes=VMEM_LIMIT_BYTES)


def _rms(x, g):
    return x * lax.rsqrt(jnp.mean(x * x, axis=-1, keepdims=True) + NORM_EPS) * g


def _silu(x):
    return x * (1.0 / (1.0 + jnp.exp(-x)))


def _for_row_chunks(nrows, fn):
    def body(i, carry):
        fn(pl.ds(pl.multiple_of(i * ROW_CHUNK, ROW_CHUNK), ROW_CHUNK))
        return carry

    lax.fori_loop(0, nrows // ROW_CHUNK, body, 0, unroll=ROW_CHUNK_UNROLL)


def _ffn_step(x_ref, pre_g_ref, wgu_ref, wd_ref, post_g_ref, o_ref, h_ref, *, first, last):
    tm = x_ref.shape[0]
    for r in range(tm // FFN_SUB_ROWS):
        base = r * FFN_SUB_ROWS
        rows = slice(base, base + FFN_SUB_ROWS)
        if first:
            for c in range(FFN_SUB_ROWS // ROW_CHUNK):
                chunk = slice(base + c * ROW_CHUNK, base + (c + 1) * ROW_CHUNK)
                h_ref[chunk, :] = _rms(x_ref[chunk, :], pre_g_ref[...]).astype(BF16)
        h = h_ref[rows, :]
        tf = wd_ref.shape[0]
        gate_up = jnp.dot(h, wgu_ref[...], preferred_element_type=F32)
        act = (_silu(gate_up[:, :tf]) * gate_up[:, tf:]).astype(BF16)
        down = jnp.dot(act, wd_ref[...], preferred_element_type=F32)
        if first:
            o_ref[rows, :] = down
        else:
            o_ref[rows, :] += down
        if last:
            for c in range(FFN_SUB_ROWS // ROW_CHUNK):
                chunk = slice(base + c * ROW_CHUNK, base + (c + 1) * ROW_CHUNK)
                o_ref[chunk, :] = x_ref[chunk, :] + 0.5 * _rms(o_ref[chunk, :], post_g_ref[...])


def _ffn_kernel(*refs):
    f = pl.program_id(1)
    nf = pl.num_programs(1)
    pl.when(f == 0)(functools.partial(_ffn_step, *refs, first=True, last=False))
    pl.when(jnp.logical_and(f > 0, f < nf - 1))(functools.partial(_ffn_step, *refs, first=False, last=False))
    pl.when(f == nf - 1)(functools.partial(_ffn_step, *refs, first=False, last=True))


def _ffn_half(x, pre_g, wgu, wd, post_g, *, tm=1024):
    t, d = x.shape
    nf, _, tf2 = wgu.shape
    tf = tf2 // 2
    assert nf >= 2 and wd.shape[0] == nf * tf
    row = lambda i, f: (i, 0)
    const = lambda i, f: (0, 0)
    return pl.pallas_call(
        _ffn_kernel,
        grid=(t // tm, nf),
        in_specs=[
            pl.BlockSpec((tm, d), row),
            pl.BlockSpec((1, d), const),
            pl.BlockSpec((None, d, 2 * tf), lambda i, f: (f, 0, 0)),
            pl.BlockSpec((tf, d), lambda i, f: (f, 0)),
            pl.BlockSpec((1, d), const),
        ],
        out_specs=pl.BlockSpec((tm, d), row),
        out_shape=jax.ShapeDtypeStruct((t, d), F32),
        scratch_shapes=[pltpu.VMEM((tm, d), BF16)],
        compiler_params=_cparams("parallel", "arbitrary"),
        name="ffn_half",
    )(x, pre_g, wgu, wd, post_g)


def _rope(x, cos, sin):
    return x * cos + pltpu.roll(x, LANES // 2, 1) * sin


def _inproj_kernel(x_ref, pre_g_ref, w_in_ref, gq_ref, gk_ref, gcq_ref, w_uq_ref, gckv_ref, w_ukv_ref,
                   cos_a_ref, sin_a_ref, cos_b_ref, sin_b_ref,
                   qa_ref, ka_ref, va_ref, qb_ref, kb_ref, vb_ref):
    hn = _rms(x_ref[...], pre_g_ref[...]).astype(BF16)
    y = jnp.dot(hn, w_in_ref[...], preferred_element_type=F32)
    cos_a, sin_a = cos_a_ref[...], sin_a_ref[...]
    cos_b, sin_b = cos_b_ref[...], sin_b_ref[...]
    a_scale = A_HEAD_DIM ** -0.5 * LOG2_E
    b_scale = (B_NOPE_DIM + B_ROPE_DIM) ** -0.5 * LOG2_E
    ones = jnp.ones((x_ref.shape[0], LANES), BF16)
    q_cols = A_HEADS * A_HEAD_DIM
    kv_cols = A_KV_HEADS * A_HEAD_DIM

    for h in range(A_HEADS):
        sl = slice(h * LANES, (h + 1) * LANES)
        qa_ref[:, sl] = (_rope(_rms(y[:, sl], gq_ref[...]), cos_a, sin_a) * a_scale).astype(BF16)
    for h in range(A_KV_HEADS):
        sl = slice(h * LANES, (h + 1) * LANES)
        ka_ref[:, sl] = _rope(_rms(y[:, q_cols + h * LANES:q_cols + (h + 1) * LANES], gk_ref[...]),
                              cos_a, sin_a).astype(BF16)
    off = q_cols + kv_cols
    for h in range(A_KV_HEADS):
        va_ref[:, 2 * h * LANES:(2 * h + 1) * LANES] = y[:, off + h * LANES:off + (h + 1) * LANES].astype(BF16)
        va_ref[:, (2 * h + 1) * LANES:(2 * h + 2) * LANES] = ones
    off += kv_cols

    cq = _rms(y[:, off:off + B_Q_LORA], gcq_ref[...]).astype(BF16)
    off += B_Q_LORA
    ckv = _rms(y[:, off:off + B_KV_LORA], gckv_ref[...]).astype(BF16)
    off += B_KV_LORA
    k_rope = _rope(y[:, off:off + LANES], cos_b, sin_b).astype(BF16)

    qf = jnp.dot(cq, w_uq_ref[...], preferred_element_type=F32)
    kv = jnp.dot(ckv, w_ukv_ref[...], preferred_element_type=F32)
    for h in range(B_HEADS):
        base = h * B_QK_PAD
        qb_ref[:, base:base + LANES] = (qf[:, base:base + LANES] * b_scale).astype(BF16)
        qb_ref[:, base + LANES:base + B_QK_PAD] = (
            _rope(qf[:, base + LANES:base + B_QK_PAD], cos_b, sin_b) * b_scale).astype(BF16)
        kb_ref[:, base:base + LANES] = kv[:, h * LANES:(h + 1) * LANES].astype(BF16)
        kb_ref[:, base + LANES:base + B_QK_PAD] = k_rope
        v_col = B_HEADS * B_NOPE_DIM + h * LANES
        vb_ref[:, 2 * h * LANES:(2 * h + 1) * LANES] = kv[:, v_col:v_col + LANES].astype(BF16)
        vb_ref[:, (2 * h + 1) * LANES:(2 * h + 2) * LANES] = ones


def _mixer_inproj(x, seq, pre_g, w_in, gq, gk, gcq, w_uq, gckv, w_ukv, tabs, *, tm=256):
    t, d = x.shape
    nblk = seq // tm
    row = lambda i: (i, 0)
    const = lambda i: (0, 0)
    pos = lambda i: (i % nblk, 0)
    full = lambda a: pl.BlockSpec(a.shape, const)
    widths = (A_HEADS * A_HEAD_DIM, A_KV_HEADS * A_HEAD_DIM, 2 * A_KV_HEADS * A_HEAD_DIM,
              B_HEADS * B_QK_PAD, B_HEADS * B_QK_PAD, 2 * B_HEADS * B_V_DIM)
    return pl.pallas_call(
        _inproj_kernel,
        grid=(t // tm,),
        in_specs=[pl.BlockSpec((tm, d), row), full(pre_g), full(w_in), full(gq), full(gk), full(gcq),
                  full(w_uq), full(gckv), full(w_ukv)] + [pl.BlockSpec((tm, LANES), pos)] * 4,
        out_specs=[pl.BlockSpec((tm, w), row) for w in widths],
        out_shape=[jax.ShapeDtypeStruct((t, w), BF16) for w in widths],
        compiler_params=_cparams("parallel"),
        name="mixer_inproj",
    )(x, pre_g, w_in, gq, gk, gcq, w_uq, gckv, w_ukv, *tabs)


def _attn_kernel(q_ref, k_ref, v_ref, o_ref, *, group, dk, dv, tk, rb):
    tq = q_ref.shape[0]
    seq = k_ref.shape[0]
    per_head = tq // rb
    nblocks = group * per_head

    def q_block(i):
        j, r = divmod(i, per_head)
        return q_ref[r * rb:(r + 1) * rb, j * dk:(j + 1) * dk]

    def body(c, carry):
        ms, accs = carry
        start = pl.multiple_of(c * tk, tk)
        k = k_ref[pl.ds(start, tk), :]
        v = v_ref[pl.ds(start, tk), :]
        new_ms, new_accs = [], []
        for i in range(nblocks):
            s = lax.dot_general(q_block(i), k, (((1,), (1,)), ((), ())), preferred_element_type=F32)
            m_new = jnp.maximum(ms[i], jnp.max(s, axis=-1, keepdims=True))
            alpha = jnp.exp2(ms[i] - m_new)
            p = jnp.exp2(s - m_new).astype(BF16)
            new_accs.append(alpha * accs[i] + jnp.dot(p, v, preferred_element_type=F32))
            new_ms.append(m_new)
        return tuple(new_ms), tuple(new_accs)

    init = (tuple(jnp.full((rb, 1), -jnp.inf, F32) for _ in range(nblocks)),
            tuple(jnp.zeros((rb, 2 * dv), F32) for _ in range(nblocks)))
    _, accs = lax.fori_loop(0, seq // tk, body, init, unroll=ATTN_CHUNK_UNROLL)
    for i in range(nblocks):
        j, r = divmod(i, per_head)
        out = accs[i][:, :dv] * (1.0 / accs[i][:, dv:])
        o_ref[r * rb:(r + 1) * rb, j * dv:(j + 1) * dv] = out.astype(o_ref.dtype)


def _attention(q, k, v, *, kv_heads, group, dk, dv, tq, tk=512, rb=256):
    b, seq, _ = q.shape
    tq, tk = min(tq, seq), min(tk, seq)
    kern = functools.partial(_attn_kernel, group=group, dk=dk, dv=dv, tk=tk, rb=min(rb, tq))
    return pl.pallas_call(
        kern,
        grid=(b, kv_heads, seq // tq),
        in_specs=[
            pl.BlockSpec((None, tq, group * dk), lambda bi, h, i: (bi, i, h)),
            pl.BlockSpec((None, seq, dk), lambda bi, h, i: (bi, 0, h)),
            pl.BlockSpec((None, seq, 2 * dv), lambda bi, h, i: (bi, 0, h)),
        ],
        out_specs=pl.BlockSpec((None, tq, group * dv), lambda bi, h, i: (bi, i, h)),
        out_shape=jax.ShapeDtypeStruct((b, seq, kv_heads * group * dv), BF16),
        compiler_params=_cparams("parallel", "parallel", "arbitrary"),
        name=f"attention_g{group}_dk{dk}",
    )(q, k, v)


def _outproj_kernel(x_ref, oa_ref, ob_ref, wa_ref, wb_ref, post_g_ref, o_ref):
    m = jnp.dot(oa_ref[...], wa_ref[...], preferred_element_type=F32)
    m += jnp.dot(ob_ref[...], wb_ref[...], preferred_element_type=F32)
    o_ref[...] = x_ref[...] + _rms(m, post_g_ref[...])


def _mixer_outproj(x, oa, ob, wa, wb, post_g, *, tm=512):
    t, d = x.shape
    row = lambda i: (i, 0)
    const = lambda i: (0, 0)
    return pl.pallas_call(
        _outproj_kernel,
        grid=(t // tm,),
        in_specs=[pl.BlockSpec((tm, d), row), pl.BlockSpec((tm, oa.shape[1]), row),
                  pl.BlockSpec((tm, ob.shape[1]), row), pl.BlockSpec(wa.shape, const),
                  pl.BlockSpec(wb.shape, const), pl.BlockSpec((1, d), const)],
        out_specs=pl.BlockSpec((tm, d), row),
        out_shape=jax.ShapeDtypeStruct((t, d), F32),
        compiler_params=_cparams("parallel"),
        name="mixer_outproj",
    )(x, oa, ob, wa, wb, post_g)


def _pool_kernel(x_ref, prev_ref, next_ref, pre_g_ref, w_ref, scale_ref, post_g_ref, o_ref, ext_ref, m_ref,
                 *, seq):
    tm, d = x_ref.shape
    group = d // len(POOL_WINDOWS)
    nblk = seq // tm
    blk = pl.program_id(0) % nblk
    g = pre_g_ref[...]
    x = x_ref[...]
    ext_ref[0:POOL_HALO, :] = jnp.where(blk > 0, _rms(prev_ref[...], g), 0.0)
    ext_ref[POOL_HALO:POOL_HALO + tm, :] = _rms(x, g)
    ext_ref[POOL_HALO + tm:, :] = jnp.where(blk < nblk - 1, _rms(next_ref[...], g), 0.0)

    t = blk * tm + lax.broadcasted_iota(jnp.int32, (tm, 1), 0)
    for gi, w in enumerate(POOL_WINDOWS):
        cols = slice(gi * group, (gi + 1) * group)
        half = w // 2
        tot = ext_ref[POOL_HALO - half:POOL_HALO - half + tm, cols]
        for j in range(1, w):
            tot += ext_ref[POOL_HALO - half + j:POOL_HALO - half + j + tm, cols]
        cnt = (jnp.minimum(t + half, seq) - jnp.maximum(t - half, 0)).astype(F32)
        pooled = tot / cnt - ext_ref[POOL_HALO:POOL_HALO + tm, cols]
        m_ref[:, cols] = jnp.dot(pooled.astype(BF16), w_ref[gi], preferred_element_type=F32)
    o_ref[...] = x + _rms(m_ref[...] * scale_ref[...], post_g_ref[...])


def _pool_mixer(x, seq, pre_g, pool_w, pool_scale, post_g, *, tm=512):
    t, d = x.shape
    hb = tm // POOL_HALO
    last = t // POOL_HALO - 1
    row = lambda i: (i, 0)
    const = lambda i: (0, 0)
    return pl.pallas_call(
        functools.partial(_pool_kernel, seq=seq),
        grid=(t // tm,),
        in_specs=[
            pl.BlockSpec((tm, d), row),
            pl.BlockSpec((POOL_HALO, d), lambda i: (jnp.maximum(i * hb - 1, 0), 0)),
            pl.BlockSpec((POOL_HALO, d), lambda i: (jnp.minimum((i + 1) * hb, last), 0)),
            pl.BlockSpec((1, d), const),
            pl.BlockSpec(pool_w.shape, lambda i: (0, 0, 0)),
            pl.BlockSpec((1, d), const),
            pl.BlockSpec((1, d), const),
        ],
        out_specs=pl.BlockSpec((tm, d), row),
        out_shape=jax.ShapeDtypeStruct((t, d), F32),
        scratch_shapes=[pltpu.VMEM((tm + 2 * POOL_HALO, d), F32), pltpu.VMEM((tm, d), F32)],
        compiler_params=_cparams("parallel"),
        name="pool_mixer",
    )(x, x, x, pre_g, pool_w, pool_scale, post_g)


def _rope_tables(seq):
    rows = seq // GRID_W
    row = jnp.repeat(jnp.arange(rows, dtype=F32), GRID_W)
    col = jnp.tile(jnp.arange(GRID_W, dtype=F32), rows)

    def angles(rot_dim):
        half = rot_dim // 2
        freqs = ROPE_BASE ** (-jnp.arange(0, half, 2, dtype=F32) / half)
        return jnp.concatenate([row[:, None] * freqs, col[:, None] * freqs], axis=-1)

    ang_a = angles(A_HEAD_DIM)
    cos_a = jnp.concatenate([jnp.cos(ang_a)] * 2, axis=-1)
    sin_a = jnp.concatenate([-jnp.sin(ang_a), jnp.sin(ang_a)], axis=-1)
    ang_b = angles(B_ROPE_DIM)
    z = jnp.zeros_like(ang_b)
    cos_b = jnp.concatenate([jnp.cos(ang_b), z, jnp.cos(ang_b), z], axis=-1)
    sin_b = jnp.concatenate([-jnp.sin(ang_b), z, jnp.sin(ang_b), z], axis=-1)
    return cos_a, sin_a, cos_b, sin_b


def _pair_split_perm(n):
    return np.concatenate([np.arange(0, n, 2), np.arange(1, n, 2)])


def _spread_rope_cols(w):
    half = B_ROPE_DIM // 2
    z = jnp.zeros((w.shape[0], half), w.dtype)
    return jnp.concatenate([w[:, 0::2], z, w[:, 1::2], z], axis=-1)


def _prep_mixer_weights(w_in, gq, gk, w_uq, w_ukv, w_out):
    d = w_in.shape[0]
    perm = _pair_split_perm(A_HEAD_DIM)
    nq, nkv = A_HEADS * A_HEAD_DIM, A_KV_HEADS * A_HEAD_DIM
    wq = w_in[:, :nq].reshape(d, A_HEADS, A_HEAD_DIM)[:, :, perm].reshape(d, nq)
    wk = w_in[:, nq:nq + nkv].reshape(d, A_KV_HEADS, A_HEAD_DIM)[:, :, perm].reshape(d, nkv)
    rest = w_in[:, nq + nkv:-B_ROPE_DIM]
    w_in_p = jnp.concatenate([wq, wk, rest, _spread_rope_cols(w_in[:, -B_ROPE_DIM:])], axis=-1).astype(BF16)

    uq = w_uq.reshape(B_Q_LORA, B_HEADS, B_NOPE_DIM + B_ROPE_DIM)
    uq_rope = _spread_rope_cols(uq[:, :, B_NOPE_DIM:].reshape(B_Q_LORA * B_HEADS, B_ROPE_DIM))
    uq_p = jnp.concatenate([uq[:, :, :B_NOPE_DIM], uq_rope.reshape(B_Q_LORA, B_HEADS, LANES)], axis=-1)
    uq_p = uq_p.reshape(B_Q_LORA, B_HEADS * B_QK_PAD).astype(BF16)

    ukv = w_ukv.reshape(B_KV_LORA, B_HEADS, B_NOPE_DIM + B_V_DIM)
    ukv_p = jnp.concatenate([ukv[:, :, :B_NOPE_DIM].reshape(B_KV_LORA, -1),
                             ukv[:, :, B_NOPE_DIM:].reshape(B_KV_LORA, -1)], axis=-1).astype(BF16)
    na = A_HEADS * A_HEAD_DIM
    return (w_in_p, gq[perm][None, :], gk[perm][None, :], uq_p, ukv_p,
            w_out[:na].astype(BF16), w_out[na:].astype(BF16))


def _trunk(x, ffn_params, mix0, mix1, tabs):
    b, seq, d = x.shape
    x = x.reshape(b * seq, d)
    (pre0, w_in, gq, gk, gcq, w_uq, gckv, w_ukv, wa, wb, post0) = mix0
    (pre1, pool_w, pool_scale, post1) = mix1

    x = _ffn_half(x, *ffn_params[0])
    qa, ka, va, qb, kb, vb = _mixer_inproj(x, seq, pre0, w_in, gq, gk, gcq, w_uq, gckv, w_ukv, tabs)
    shp = lambda a: a.reshape(b, seq, a.shape[-1])
    oa = _attention(shp(qa), shp(ka), shp(va), kv_heads=A_KV_HEADS, group=A_HEADS // A_KV_HEADS,
                    dk=A_HEAD_DIM, dv=A_HEAD_DIM, tq=256)
    ob = _attention(shp(qb), shp(kb), shp(vb), kv_heads=B_HEADS, group=1, dk=B_QK_PAD, dv=B_V_DIM, tq=1024)
    x = _mixer_outproj(x, oa.reshape(b * seq, -1), ob.reshape(b * seq, -1), wa, wb, post0)
    x = _ffn_half(x, *ffn_params[1])
    x = _ffn_half(x, *ffn_params[2])
    x = _pool_mixer(x, seq, pre1, pool_w, pool_scale, post1)
    x = _ffn_half(x, *ffn_params[3])
    return x.reshape(b, seq, d)


def kernel(x_prompt, x_sample, l0_ffn1_pre_g, l0_ffn1_w_gate, l0_ffn1_w_up, l0_ffn1_w_down, l0_ffn1_post_g, l0_mix_pre_g, l0_w_in, l0_a_q_norm_g, l0_a_k_norm_g, l0_b_cq_norm_g, l0_b_w_uq, l0_b_ckv_norm_g, l0_b_w_ukv, l0_w_out, l0_mix_post_g, l0_ffn2_pre_g, l0_ffn2_w_gate, l0_ffn2_w_up, l0_ffn2_w_down, l0_ffn2_post_g, l1_ffn1_pre_g, l1_ffn1_w_gate, l1_ffn1_w_up, l1_ffn1_w_down, l1_ffn1_post_g, l1_mix_pre_g, l1_pool_w, l1_pool_scale, l1_mix_post_g, l1_ffn2_pre_g, l1_ffn2_w_gate, l1_ffn2_w_up, l1_ffn2_w_down, l1_ffn2_post_g):
    vec = lambda g: g[None, :]

    def ffn(pre, wg, wu, wd, post):
        d, dff = wg.shape
        nf = dff // FFN_DFF_CHUNK
        chunks = lambda w: w.astype(BF16).reshape(d, nf, FFN_DFF_CHUNK).transpose(1, 0, 2)
        wgu = jnp.concatenate([chunks(wg), chunks(wu)], axis=-1)
        return vec(pre), wgu, wd.astype(BF16), vec(post)

    ffn_params = (
        ffn(l0_ffn1_pre_g, l0_ffn1_w_gate, l0_ffn1_w_up, l0_ffn1_w_down, l0_ffn1_post_g),
        ffn(l0_ffn2_pre_g, l0_ffn2_w_gate, l0_ffn2_w_up, l0_ffn2_w_down, l0_ffn2_post_g),
        ffn(l1_ffn1_pre_g, l1_ffn1_w_gate, l1_ffn1_w_up, l1_ffn1_w_down, l1_ffn1_post_g),
        ffn(l1_ffn2_pre_g, l1_ffn2_w_gate, l1_ffn2_w_up, l1_ffn2_w_down, l1_ffn2_post_g),
    )
    w_in, gq, gk, w_uq, w_ukv, wa, wb = _prep_mixer_weights(
        l0_w_in, l0_a_q_norm_g, l0_a_k_norm_g, l0_b_w_uq, l0_b_w_ukv, l0_w_out)
    mix0 = (vec(l0_mix_pre_g), w_in, gq, gk, vec(l0_b_cq_norm_g), w_uq, vec(l0_b_ckv_norm_g), w_ukv,
            wa, wb, vec(l0_mix_post_g))
    mix1 = (vec(l1_mix_pre_g), l1_pool_w.astype(BF16), vec(l1_pool_scale), vec(l1_mix_post_g))

    outs = []
    for x in (x_prompt, x_sample):
        tabs = _rope_tables(x.shape[1])
        outs.append(_trunk(x, ffn_params, mix0, mix1, tabs))
    return tuple(outs)
```

```python
import functools
import math

import jax
import jax.numpy as jnp
import numpy as np
from jax import lax
from jax.experimental import pallas as pl
from jax.experimental.pallas import tpu as pltpu

F32 = jnp.float32
BF16 = jnp.bfloat16

NORM_EPS = 1e-6
ROPE_BASE = 10000.0
GRID_W = 64
A_HEADS = 8
A_KV_HEADS = 2
A_HEAD_DIM = 128
B_HEADS = 8
B_Q_LORA = 512
B_KV_LORA = 256
B_NOPE_DIM = 128
B_ROPE_DIM = 64
B_V_DIM = 128
POOL_WINDOWS = (2, 4, 8, 16)
LANES = 128
LOG2_E = math.log2(math.e)
ATTN_CHUNK_UNROLL = 16
B_QK_PAD = 2 * LANES
ROW_CHUNK = 32
ROW_CHUNK_UNROLL = 4
PROJ_SUB_ROWS = 256
FFN_SUB_ROWS = 256
POOL_HALO = 8

VMEM_LIMIT_BYTES = 60 * 1024 * 1024


def _cparams(*semantics):
    return pltpu.CompilerParams(dimension_semantics=semantics, vmem_limit_bytes=VMEM_LIMIT_BYTES)


def _rms(x, g):
    return x * lax.rsqrt(jnp.mean(x * x, axis=-1, keepdims=True) + NORM_EPS) * g


def _silu(x):
    return x * (1.0 / (1.0 + jnp.exp(-x)))


def _for_row_chunks(nrows, fn):
    def body(i, carry):
        fn(pl.ds(pl.multiple_of(i * ROW_CHUNK, ROW_CHUNK), ROW_CHUNK))
        return carry

    lax.fori_loop(0, nrows // ROW_CHUNK, body, 0, unroll=ROW_CHUNK_UNROLL)


def _ffn_step(x_ref, pre_g_ref, wg_ref, wu_ref, wd_ref, post_g_ref, o_ref, h_ref, *, first, last):
    tm = x_ref.shape[0]
    for r in range(tm // FFN_SUB_ROWS):
        base = r * FFN_SUB_ROWS
        rows = slice(base, base + FFN_SUB_ROWS)
        if first:
            for c in range(FFN_SUB_ROWS // ROW_CHUNK):
                chunk = slice(base + c * ROW_CHUNK, base + (c + 1) * ROW_CHUNK)
                h_ref[chunk, :] = _rms(x_ref[chunk, :], pre_g_ref[...]).astype(BF16)
        h = h_ref[rows, :]
        gate = jnp.dot(h, wg_ref[...], preferred_element_type=F32)
        up = jnp.dot(h, wu_ref[...], preferred_element_type=F32)
        act = (_silu(gate) * up).astype(BF16)
        down = jnp.dot(act, wd_ref[...], preferred_element_type=F32)
        if first:
            o_ref[rows, :] = down
        else:
            o_ref[rows, :] += down
        if last:
            for c in range(FFN_SUB_ROWS // ROW_CHUNK):
                chunk = slice(base + c * ROW_CHUNK, base + (c + 1) * ROW_CHUNK)
                o_ref[chunk, :] = x_ref[chunk, :] + 0.5 * _rms(o_ref[chunk, :], post_g_ref[...])


def _ffn_kernel(*refs):
    f = pl.program_id(1)
    nf = pl.num_programs(1)
    pl.when(f == 0)(functools.partial(_ffn_step, *refs, first=True, last=False))
    pl.when(jnp.logical_and(f > 0, f < nf - 1))(functools.partial(_ffn_step, *refs, first=False, last=False))
    pl.when(f == nf - 1)(functools.partial(_ffn_step, *refs, first=False, last=True))


def _ffn_half(x, pre_g, wg, wu, wd, post_g, *, tm=1024, tf=512):
    t, d = x.shape
    dff = wg.shape[1]
    row = lambda i, f: (i, 0)
    const = lambda i, f: (0, 0)
    return pl.pallas_call(
        _ffn_kernel,
        grid=(t // tm, dff // tf),
        in_specs=[
            pl.BlockSpec((tm, d), row),
            pl.BlockSpec((1, d), const),
            pl.BlockSpec((d, tf), lambda i, f: (0, f)),
            pl.BlockSpec((d, tf), lambda i, f: (0, f)),
            pl.BlockSpec((tf, d), lambda i, f: (f, 0)),
            pl.BlockSpec((1, d), const),
        ],
        out_specs=pl.BlockSpec((tm, d), row),
        out_shape=jax.ShapeDtypeStruct((t, d), F32),
        scratch_shapes=[pltpu.VMEM((tm, d), BF16)],
        compiler_params=_cparams("parallel", "arbitrary"),
        name="ffn_half",
    )(x, pre_g, wg, wu, wd, post_g)


def _rope(x, cos, sin):
    return x * cos + pltpu.roll(x, LANES // 2, 1) * sin


def _inproj_kernel(x_ref, pre_g_ref, w_in_ref, gq_ref, gk_ref, gcq_ref, w_uq_ref, gckv_ref, w_ukv_ref,
                   cos_a_ref, sin_a_ref, cos_b_ref, sin_b_ref, *out_refs):
    for r in range(x_ref.shape[0] // PROJ_SUB_ROWS):
        rows = pl.ds(r * PROJ_SUB_ROWS, PROJ_SUB_ROWS)
        at = lambda ref: ref.at[rows]
        _inproj_rows(at(x_ref), pre_g_ref, w_in_ref, gq_ref, gk_ref, gcq_ref, w_uq_ref, gckv_ref, w_ukv_ref,
                     at(cos_a_ref), at(sin_a_ref), at(cos_b_ref), at(sin_b_ref), *map(at, out_refs))


def _inproj_rows(x_ref, pre_g_ref, w_in_ref, gq_ref, gk_ref, gcq_ref, w_uq_ref, gckv_ref, w_ukv_ref,
                 cos_a_ref, sin_a_ref, cos_b_ref, sin_b_ref,
                 qa_ref, ka_ref, va_ref, qb_ref, kb_ref, vb_ref):
    hn = _rms(x_ref[...], pre_g_ref[...]).astype(BF16)
    y = jnp.dot(hn, w_in_ref[...], preferred_element_type=F32)
    cos_a, sin_a = cos_a_ref[...], sin_a_ref[...]
    cos_b, sin_b = cos_b_ref[...], sin_b_ref[...]
    a_scale = A_HEAD_DIM ** -0.5 * LOG2_E
    b_scale = (B_NOPE_DIM + B_ROPE_DIM) ** -0.5 * LOG2_E
    ones = jnp.ones((x_ref.shape[0], LANES), BF16)
    q_cols = A_HEADS * A_HEAD_DIM
    kv_cols = A_KV_HEADS * A_HEAD_DIM

    for h in range(A_HEADS):
        sl = slice(h * LANES, (h + 1) * LANES)
        qa_ref[:, sl] = (_rope(_rms(y[:, sl], gq_ref[...]), cos_a, sin_a) * a_scale).astype(BF16)
    for h in range(A_KV_HEADS):
        sl = slice(h * LANES, (h + 1) * LANES)
        ka_ref[:, sl] = _rope(_rms(y[:, q_cols + h * LANES:q_cols + (h + 1) * LANES], gk_ref[...]),
                              cos_a, sin_a).astype(BF16)
    off = q_cols + kv_cols
    for h in range(A_KV_HEADS):
        va_ref[:, 2 * h * LANES:(2 * h + 1) * LANES] = y[:, off + h * LANES:off + (h + 1) * LANES].astype(BF16)
        va_ref[:, (2 * h + 1) * LANES:(2 * h + 2) * LANES] = ones
    off += kv_cols

    cq = _rms(y[:, off:off + B_Q_LORA], gcq_ref[...]).astype(BF16)
    off += B_Q_LORA
    ckv = _rms(y[:, off:off + B_KV_LORA], gckv_ref[...]).astype(BF16)
    off += B_KV_LORA
    k_rope = _rope(y[:, off:off + LANES], cos_b, sin_b).astype(BF16)

    qf = jnp.dot(cq, w_uq_ref[...], preferred_element_type=F32)
    kv = jnp.dot(ckv, w_ukv_ref[...], preferred_element_type=F32)
    for h in range(B_HEADS):
        base = h * B_QK_PAD
        qb_ref[:, base:base + LANES] = (qf[:, base:base + LANES] * b_scale).astype(BF16)
        qb_ref[:, base + LANES:base + B_QK_PAD] = (
            _rope(qf[:, base + LANES:base + B_QK_PAD], cos_b, sin_b) * b_scale).astype(BF16)
        kb_ref[:, base:base + LANES] = kv[:, h * LANES:(h + 1) * LANES].astype(BF16)
        kb_ref[:, base + LANES:base + B_QK_PAD] = k_rope
        v_col = B_HEADS * B_NOPE_DIM + h * LANES
        vb_ref[:, 2 * h * LANES:(2 * h + 1) * LANES] = kv[:, v_col:v_col + LANES].astype(BF16)
        vb_ref[:, (2 * h + 1) * LANES:(2 * h + 2) * LANES] = ones


def _mixer_inproj(x, seq, pre_g, w_in, gq, gk, gcq, w_uq, gckv, w_ukv, tabs, *, tm=512):
    t, d = x.shape
    nblk = seq // tm
    row = lambda i: (i, 0)
    const = lambda i: (0, 0)
    pos = lambda i: (i % nblk, 0)
    full = lambda a: pl.BlockSpec(a.shape, const, pipeline_mode=pl.Buffered(1))
    widths = (A_HEADS * A_HEAD_DIM, A_KV_HEADS * A_HEAD_DIM, 2 * A_KV_HEADS * A_HEAD_DIM,
              B_HEADS * B_QK_PAD, B_HEADS * B_QK_PAD, 2 * B_HEADS * B_V_DIM)
    return pl.pallas_call(
        _inproj_kernel,
        grid=(t // tm,),
        in_specs=[pl.BlockSpec((tm, d), row), full(pre_g), full(w_in), full(gq), full(gk), full(gcq),
                  full(w_uq), full(gckv), full(w_ukv)] + [pl.BlockSpec((tm, LANES), pos)] * 4,
        out_specs=[pl.BlockSpec((tm, w), row) for w in widths],
        out_shape=[jax.ShapeDtypeStruct((t, w), BF16) for w in widths],
        compiler_params=_cparams("parallel"),
        name="mixer_inproj",
    )(x, pre_g, w_in, gq, gk, gcq, w_uq, gckv, w_ukv, *tabs)


def _attn_kernel(q_ref, k_ref, v_ref, o_ref, *, group, dk, dv, tk, rb):
    tq = q_ref.shape[0]
    seq = k_ref.shape[0]
    per_head = tq // rb
    nblocks = group * per_head

    def q_block(i):
        j, r = divmod(i, per_head)
        return q_ref[r * rb:(r + 1) * rb, j * dk:(j + 1) * dk]

    def body(c, carry):
        ms, accs = carry
        start = pl.multiple_of(c * tk, tk)
        k = k_ref[pl.ds(start, tk), :]
        v = v_ref[pl.ds(start, tk), :]
        new_ms, new_accs = [], []
        for i in range(nblocks):
            s = lax.dot_general(q_block(i), k, (((1,), (1,)), ((), ())), preferred_element_type=F32)
            m_new = jnp.maximum(ms[i], jnp.max(s, axis=-1, keepdims=True))
            alpha = jnp.exp2(ms[i] - m_new)
            p = jnp.exp2(s - m_new).astype(BF16)
            new_accs.append(alpha * accs[i] + jnp.dot(p, v, preferred_element_type=F32))
            new_ms.append(m_new)
        return tuple(new_ms), tuple(new_accs)

    init = (tuple(jnp.full((rb, 1), -jnp.inf, F32) for _ in range(nblocks)),
            tuple(jnp.zeros((rb, 2 * dv), F32) for _ in range(nblocks)))
    _, accs = lax.fori_loop(0, seq // tk, body, init, unroll=ATTN_CHUNK_UNROLL)
    for i in range(nblocks):
        j, r = divmod(i, per_head)
        out = accs[i][:, :dv] * (1.0 / accs[i][:, dv:])
        o_ref[r * rb:(r + 1) * rb, j * dv:(j + 1) * dv] = out.astype(o_ref.dtype)


def _attention(q, k, v, *, kv_heads, group, dk, dv, tq, tk=512, rb=256):
    b, seq, _ = q.shape
    tq, tk = min(tq, seq), min(tk, seq)
    kern = functools.partial(_attn_kernel, group=group, dk=dk, dv=dv, tk=tk, rb=min(rb, tq))
    return pl.pallas_call(
        kern,
        grid=(b, kv_heads, seq // tq),
        in_specs=[
            pl.BlockSpec((None, tq, group * dk), lambda bi, h, i: (bi, i, h)),
            pl.BlockSpec((None, seq, dk), lambda bi, h, i: (bi, 0, h)),
            pl.BlockSpec((None, seq, 2 * dv), lambda bi, h, i: (bi, 0, h)),
        ],
        out_specs=pl.BlockSpec((None, tq, group * dv), lambda bi, h, i: (bi, i, h)),
        out_shape=jax.ShapeDtypeStruct((b, seq, kv_heads * group * dv), BF16),
        compiler_params=_cparams("parallel", "parallel", "arbitrary"),
        name=f"attention_g{group}_dk{dk}",
    )(q, k, v)


def _outproj_kernel(x_ref, oa_ref, ob_ref, wa_ref, wb_ref, post_g_ref, o_ref):
    for r in range(x_ref.shape[0] // PROJ_SUB_ROWS):
        rows = slice(r * PROJ_SUB_ROWS, (r + 1) * PROJ_SUB_ROWS)
        m = jnp.dot(oa_ref[rows, :], wa_ref[...], preferred_element_type=F32)
        m += jnp.dot(ob_ref[rows, :], wb_ref[...], preferred_element_type=F32)
        o_ref[rows, :] = x_ref[rows, :] + _rms(m, post_g_ref[...])


def _mixer_outproj(x, oa, ob, wa, wb, post_g, *, tm=512):
    t, d = x.shape
    row = lambda i: (i, 0)
    const = lambda i: (0, 0)
    return pl.pallas_call(
        _outproj_kernel,
        grid=(t // tm,),
        in_specs=[pl.BlockSpec((tm, d), row), pl.BlockSpec((tm, oa.shape[1]), row),
                  pl.BlockSpec((tm, ob.shape[1]), row),
                  pl.BlockSpec(wa.shape, const, pipeline_mode=pl.Buffered(1)),
                  pl.BlockSpec(wb.shape, const, pipeline_mode=pl.Buffered(1)), pl.BlockSpec((1, d), const)],
        out_specs=pl.BlockSpec((tm, d), row),
        out_shape=jax.ShapeDtypeStruct((t, d), F32),
        compiler_params=_cparams("parallel"),
        name="mixer_outproj",
    )(x, oa, ob, wa, wb, post_g)


def _pool_kernel(x_ref, prev_ref, next_ref, pre_g_ref, w_ref, scale_ref, post_g_ref, o_ref, ext_ref, m_ref,
                 *, seq):
    tm, d = x_ref.shape
    group = d // len(POOL_WINDOWS)
    nblk = seq // tm
    blk = pl.program_id(0) % nblk
    g = pre_g_ref[...]
    x = x_ref[...]
    ext_ref[0:POOL_HALO, :] = jnp.where(blk > 0, _rms(prev_ref[...], g), 0.0)
    ext_ref[POOL_HALO:POOL_HALO + tm, :] = _rms(x, g)
    ext_ref[POOL_HALO + tm:, :] = jnp.where(blk < nblk - 1, _rms(next_ref[...], g), 0.0)

    t = blk * tm + lax.broadcasted_iota(jnp.int32, (tm, 1), 0)
    n_ext = tm + 2 * POOL_HALO
    for gi, w in enumerate(POOL_WINDOWS):
        cols = slice(gi * group, (gi + 1) * group)
        half = w // 2
        run = ext_ref[:, cols]
        span = 1
        while span < w:
            run = run + pltpu.roll(run, span, 0)
            span *= 2
        lead = half - 1
        if lead:
            run = pltpu.roll(run, n_ext - lead, 0)
        tot = run[POOL_HALO:POOL_HALO + tm]
        cnt = (jnp.minimum(t + half, seq) - jnp.maximum(t - half, 0)).astype(F32)
        pooled = tot * (1.0 / cnt) - ext_ref[POOL_HALO:POOL_HALO + tm, cols]
        m_ref[:, cols] = jnp.dot(pooled.astype(BF16), w_ref[gi], preferred_element_type=F32)
    o_ref[...] = x + _rms(m_ref[...] * scale_ref[...], post_g_ref[...])


def _pool_mixer(x, seq, pre_g, pool_w, pool_scale, post_g, *, tm=512):
    t, d = x.shape
    hb = tm // POOL_HALO
    last = t // POOL_HALO - 1
    row = lambda i: (i, 0)
    const = lambda i: (0, 0)
    return pl.pallas_call(
        functools.partial(_pool_kernel, seq=seq),
        grid=(t // tm,),
        in_specs=[
            pl.BlockSpec((tm, d), row),
            pl.BlockSpec((POOL_HALO, d), lambda i: (jnp.maximum(i * hb - 1, 0), 0)),
            pl.BlockSpec((POOL_HALO, d), lambda i: (jnp.minimum((i + 1) * hb, last), 0)),
            pl.BlockSpec((1, d), const),
            pl.BlockSpec(pool_w.shape, lambda i: (0, 0, 0)),
            pl.BlockSpec((1, d), const),
            pl.BlockSpec((1, d), const),
        ],
        out_specs=pl.BlockSpec((tm, d), row),
        out_shape=jax.ShapeDtypeStruct((t, d), F32),
        scratch_shapes=[pltpu.VMEM((tm + 2 * POOL_HALO, d), F32), pltpu.VMEM((tm, d), F32)],
        compiler_params=_cparams("parallel"),
        name="pool_mixer",
    )(x, x, x, pre_g, pool_w, pool_scale, post_g)


def _rope_tables(seq):
    rows = seq // GRID_W
    row = jnp.repeat(jnp.arange(rows, dtype=F32), GRID_W)
    col = jnp.tile(jnp.arange(GRID_W, dtype=F32), rows)

    def angles(rot_dim):
        half = rot_dim // 2
        freqs = ROPE_BASE ** (-jnp.arange(0, half, 2, dtype=F32) / half)
        return jnp.concatenate([row[:, None] * freqs, col[:, None] * freqs], axis=-1)

    ang_a = angles(A_HEAD_DIM)
    cos_a = jnp.concatenate([jnp.cos(ang_a)] * 2, axis=-1)
    sin_a = jnp.concatenate([-jnp.sin(ang_a), jnp.sin(ang_a)], axis=-1)
    ang_b = angles(B_ROPE_DIM)
    z = jnp.zeros_like(ang_b)
    cos_b = jnp.concatenate([jnp.cos(ang_b), z, jnp.cos(ang_b), z], axis=-1)
    sin_b = jnp.concatenate([-jnp.sin(ang_b), z, jnp.sin(ang_b), z], axis=-1)
    return cos_a, sin_a, cos_b, sin_b


def _pair_split_perm(n):
    return np.concatenate([np.arange(0, n, 2), np.arange(1, n, 2)])


def _spread_rope_cols(w):
    half = B_ROPE_DIM // 2
    z = jnp.zeros((w.shape[0], half), w.dtype)
    return jnp.concatenate([w[:, 0::2], z, w[:, 1::2], z], axis=-1)


def _prep_mixer_weights(w_in, gq, gk, w_uq, w_ukv, w_out):
    d = w_in.shape[0]
    perm = _pair_split_perm(A_HEAD_DIM)
    nq, nkv = A_HEADS * A_HEAD_DIM, A_KV_HEADS * A_HEAD_DIM
    wq = w_in[:, :nq].reshape(d, A_HEADS, A_HEAD_DIM)[:, :, perm].reshape(d, nq)
    wk = w_in[:, nq:nq + nkv].reshape(d, A_KV_HEADS, A_HEAD_DIM)[:, :, perm].reshape(d, nkv)
    rest = w_in[:, nq + nkv:-B_ROPE_DIM]
    w_in_p = jnp.concatenate([wq, wk, rest, _spread_rope_cols(w_in[:, -B_ROPE_DIM:])], axis=-1).astype(BF16)

    uq = w_uq.reshape(B_Q_LORA, B_HEADS, B_NOPE_DIM + B_ROPE_DIM)
    uq_rope = _spread_rope_cols(uq[:, :, B_NOPE_DIM:].reshape(B_Q_LORA * B_HEADS, B_ROPE_DIM))
    uq_p = jnp.concatenate([uq[:, :, :B_NOPE_DIM], uq_rope.reshape(B_Q_LORA, B_HEADS, LANES)], axis=-1)
    uq_p = uq_p.reshape(B_Q_LORA, B_HEADS * B_QK_PAD).astype(BF16)

    ukv = w_ukv.reshape(B_KV_LORA, B_HEADS, B_NOPE_DIM + B_V_DIM)
    ukv_p = jnp.concatenate([ukv[:, :, :B_NOPE_DIM].reshape(B_KV_LORA, -1),
                             ukv[:, :, B_NOPE_DIM:].reshape(B_KV_LORA, -1)], axis=-1).astype(BF16)
    na = A_HEADS * A_HEAD_DIM
    return (w_in_p, gq[perm][None, :], gk[perm][None, :], uq_p, ukv_p,
            w_out[:na].astype(BF16), w_out[na:].astype(BF16))


def _trunk(x, ffn_params, mix0, mix1, tabs):
    b, seq, d = x.shape
    x = x.reshape(b * seq, d)
    (pre0, w_in, gq, gk, gcq, w_uq, gckv, w_ukv, wa, wb, post0) = mix0
    (pre1, pool_w, pool_scale, post1) = mix1

    x = _ffn_half(x, *ffn_params[0])
    qa, ka, va, qb, kb, vb = _mixer_inproj(x, seq, pre0, w_in, gq, gk, gcq, w_uq, gckv, w_ukv, tabs)
    shp = lambda a: a.reshape(b, seq, a.shape[-1])
    oa = _attention(shp(qa), shp(ka), shp(va), kv_heads=A_KV_HEADS, group=A_HEADS // A_KV_HEADS,
                    dk=A_HEAD_DIM, dv=A_HEAD_DIM, tq=256)
    ob = _attention(shp(qb), shp(kb), shp(vb), kv_heads=B_HEADS, group=1, dk=B_QK_PAD, dv=B_V_DIM, tq=1024)
    x = _mixer_outproj(x, oa.reshape(b * seq, -1), ob.reshape(b * seq, -1), wa, wb, post0)
    x = _ffn_half(x, *ffn_params[1])
    x = _ffn_half(x, *ffn_params[2])
    x = _pool_mixer(x, seq, pre1, pool_w, pool_scale, post1)
    x = _ffn_half(x, *ffn_params[3])
    return x.reshape(b, seq, d)


def kernel(x_prompt, x_sample, l0_ffn1_pre_g, l0_ffn1_w_gate, l0_ffn1_w_up, l0_ffn1_w_down, l0_ffn1_post_g, l0_mix_pre_g, l0_w_in, l0_a_q_norm_g, l0_a_k_norm_g, l0_b_cq_norm_g, l0_b_w_uq, l0_b_ckv_norm_g, l0_b_w_ukv, l0_w_out, l0_mix_post_g, l0_ffn2_pre_g, l0_ffn2_w_gate, l0_ffn2_w_up, l0_ffn2_w_down, l0_ffn2_post_g, l1_ffn1_pre_g, l1_ffn1_w_gate, l1_ffn1_w_up, l1_ffn1_w_down, l1_ffn1_post_g, l1_mix_pre_g, l1_pool_w, l1_pool_scale, l1_mix_post_g, l1_ffn2_pre_g, l1_ffn2_w_gate, l1_ffn2_w_up, l1_ffn2_w_down, l1_ffn2_post_g):
    vec = lambda g: g[None, :]
    ffn = lambda pre, wg, wu, wd, post: (vec(pre), wg.astype(BF16), wu.astype(BF16), wd.astype(BF16), vec(post))
    ffn_params = (
        ffn(l0_ffn1_pre_g, l0_ffn1_w_gate, l0_ffn1_w_up, l0_ffn1_w_down, l0_ffn1_post_g),
        ffn(l0_ffn2_pre_g, l0_ffn2_w_gate, l0_ffn2_w_up, l0_ffn2_w_down, l0_ffn2_post_g),
        ffn(l1_ffn1_pre_g, l1_ffn1_w_gate, l1_ffn1_w_up, l1_ffn1_w_down, l1_ffn1_post_g),
        ffn(l1_ffn2_pre_g, l1_ffn2_w_gate, l1_ffn2_w_up, l1_ffn2_w_down, l1_ffn2_post_g),
    )
    w_in, gq, gk, w_uq, w_ukv, wa, wb = _prep_mixer_weights(
        l0_w_in, l0_a_q_norm_g, l0_a_k_norm_g, l0_b_w_uq, l0_b_w_ukv, l0_w_out)
    mix0 = (vec(l0_mix_pre_g), w_in, gq, gk, vec(l0_b_cq_norm_g), w_uq, vec(l0_b_ckv_norm_g), w_ukv,
            wa, wb, vec(l0_mix_post_g))
    mix1 = (vec(l1_mix_pre_g), l1_pool_w.astype(BF16), vec(l1_pool_scale), vec(l1_mix_post_g))

    outs = []
    for x in (x_prompt, x_sample):
        tabs = _rope_tables(x.shape[1])
        outs.append(_trunk(x, ffn_params, mix0, mix1, tabs))
    return tuple(outs)
```

```python
import functools
import math

import jax
import jax.numpy as jnp
import numpy as np
from jax import lax
from jax.experimental import pallas as pl
from jax.experimental.pallas import tpu as pltpu

F32 = jnp.float32
BF16 = jnp.bfloat16

NORM_EPS = 1e-6
ROPE_BASE = 10000.0
GRID_W = 64
A_HEADS = 8
A_KV_HEADS = 2
A_HEAD_DIM = 128
B_HEADS = 8
B_Q_LORA = 512
B_KV_LORA = 256
B_NOPE_DIM = 128
B_ROPE_DIM = 64
B_V_DIM = 128
POOL_WINDOWS = (2, 4, 8, 16)
LANES = 128
LOG2_E = math.log2(math.e)
ATTN_CHUNK_UNROLL = 16
B_QK_PAD = 2 * LANES
ROW_CHUNK = 32
ROW_CHUNK_UNROLL = 4
PROJ_SUB_ROWS = 256
FFN_SUB_ROWS = 512
POOL_HALO = 8

VMEM_LIMIT_BYTES = 60 * 1024 * 1024


def _cparams(*semantics):
    return pltpu.CompilerParams(dimension_semantics=semantics, vmem_limit_bytes=VMEM_LIMIT_BYTES)


def _rms(x, g):
    return x * lax.rsqrt(jnp.mean(x * x, axis=-1, keepdims=True) + NORM_EPS) * g


def _silu(x):
    return x * (1.0 / (1.0 + jnp.exp(-x)))


def _for_row_chunks(nrows, fn):
    def body(i, carry):
        fn(pl.ds(pl.multiple_of(i * ROW_CHUNK, ROW_CHUNK), ROW_CHUNK))
        return carry

    lax.fori_loop(0, nrows // ROW_CHUNK, body, 0, unroll=ROW_CHUNK_UNROLL)


def _ffn_step(x_ref, pre_g_ref, wg_ref, wu_ref, wd_ref, post_g_ref, o_ref, h_ref, *, first, last):
    tm = x_ref.shape[0]
    for r in range(tm // FFN_SUB_ROWS):
        base = r * FFN_SUB_ROWS
        rows = slice(base, base + FFN_SUB_ROWS)
        if first:
            for c in range(FFN_SUB_ROWS // ROW_CHUNK):
                chunk = slice(base + c * ROW_CHUNK, base + (c + 1) * ROW_CHUNK)
                h_ref[chunk, :] = _rms(x_ref[chunk, :], pre_g_ref[...]).astype(BF16)
        h = h_ref[rows, :]
        gate = jnp.dot(h, wg_ref[...], preferred_element_type=F32)
        up = jnp.dot(h, wu_ref[...], preferred_element_type=F32)
        act = (_silu(gate) * up).astype(BF16)
        down = jnp.dot(act, wd_ref[...], preferred_element_type=F32)
        if first:
            o_ref[rows, :] = down
        else:
            o_ref[rows, :] += down
        if last:
            for c in range(FFN_SUB_ROWS // ROW_CHUNK):
                chunk = slice(base + c * ROW_CHUNK, base + (c + 1) * ROW_CHUNK)
                o_ref[chunk, :] = x_ref[chunk, :] + 0.5 * _rms(o_ref[chunk, :], post_g_ref[...])


def _ffn_kernel(*refs):
    f = pl.program_id(1)
    nf = pl.num_programs(1)
    pl.when(f == 0)(functools.partial(_ffn_step, *refs, first=True, last=False))
    pl.when(jnp.logical_and(f > 0, f < nf - 1))(functools.partial(_ffn_step, *refs, first=False, last=False))
    pl.when(f == nf - 1)(functools.partial(_ffn_step, *refs, first=False, last=True))


def _ffn_half(x, pre_g, wg, wu, wd, post_g, *, tm=1024, tf=512):
    t, d = x.shape
    dff = wg.shape[1]
    row = lambda i, f: (i, 0)
    const = lambda i, f: (0, 0)
    return pl.pallas_call(
        _ffn_kernel,
        grid=(t // tm, dff // tf),
        in_specs=[
            pl.BlockSpec((tm, d), row),
            pl.BlockSpec((1, d), const),
            pl.BlockSpec((d, tf), lambda i, f: (0, f)),
            pl.BlockSpec((d, tf), lambda i, f: (0, f)),
            pl.BlockSpec((tf, d), lambda i, f: (f, 0)),
            pl.BlockSpec((1, d), const),
        ],
        out_specs=pl.BlockSpec((tm, d), row),
        out_shape=jax.ShapeDtypeStruct((t, d), F32),
        scratch_shapes=[pltpu.VMEM((tm, d), BF16)],
        compiler_params=_cparams("parallel", "arbitrary"),
        name="ffn_half",
    )(x, pre_g, wg, wu, wd, post_g)


def _rope(x, cos, sin):
    return x * cos + pltpu.roll(x, LANES // 2, 1) * sin


def _inproj_kernel(x_ref, pre_g_ref, w_in_ref, gq_ref, gk_ref, gcq_ref, w_uq_ref, gckv_ref, w_ukv_ref,
                   cos_a_ref, sin_a_ref, cos_b_ref, sin_b_ref, *out_refs):
    for r in range(x_ref.shape[0] // PROJ_SUB_ROWS):
        rows = pl.ds(r * PROJ_SUB_ROWS, PROJ_SUB_ROWS)
        at = lambda ref: ref.at[rows]
        _inproj_rows(at(x_ref), pre_g_ref, w_in_ref, gq_ref, gk_ref, gcq_ref, w_uq_ref, gckv_ref, w_ukv_ref,
                     at(cos_a_ref), at(sin_a_ref), at(cos_b_ref), at(sin_b_ref), *map(at, out_refs))


def _inproj_rows(x_ref, pre_g_ref, w_in_ref, gq_ref, gk_ref, gcq_ref, w_uq_ref, gckv_ref, w_ukv_ref,
                 cos_a_ref, sin_a_ref, cos_b_ref, sin_b_ref,
                 qa_ref, ka_ref, va_ref, qb_ref, kb_ref, vb_ref):
    hn = _rms(x_ref[...], pre_g_ref[...]).astype(BF16)
    y = jnp.dot(hn, w_in_ref[...], preferred_element_type=F32)
    cos_a, sin_a = cos_a_ref[...], sin_a_ref[...]
    cos_b, sin_b = cos_b_ref[...], sin_b_ref[...]
    a_scale = A_HEAD_DIM ** -0.5 * LOG2_E
    b_scale = (B_NOPE_DIM + B_ROPE_DIM) ** -0.5 * LOG2_E
    ones = jnp.ones((x_ref.shape[0], LANES), BF16)
    q_cols = A_HEADS * A_HEAD_DIM
    kv_cols = A_KV_HEADS * A_HEAD_DIM

    for h in range(A_HEADS):
        sl = slice(h * LANES, (h + 1) * LANES)
        qa_ref[:, sl] = (_rope(_rms(y[:, sl], gq_ref[...]), cos_a, sin_a) * a_scale).astype(BF16)
    for h in range(A_KV_HEADS):
        sl = slice(h * LANES, (h + 1) * LANES)
        ka_ref[:, sl] = _rope(_rms(y[:, q_cols + h * LANES:q_cols + (h + 1) * LANES], gk_ref[...]),
                              cos_a, sin_a).astype(BF16)
    off = q_cols + kv_cols
    for h in range(A_KV_HEADS):
        va_ref[:, 2 * h * LANES:(2 * h + 1) * LANES] = y[:, off + h * LANES:off + (h + 1) * LANES].astype(BF16)
        va_ref[:, (2 * h + 1) * LANES:(2 * h + 2) * LANES] = ones
    off += kv_cols

    cq = _rms(y[:, off:off + B_Q_LORA], gcq_ref[...]).astype(BF16)
    off += B_Q_LORA
    ckv = _rms(y[:, off:off + B_KV_LORA], gckv_ref[...]).astype(BF16)
    off += B_KV_LORA
    k_rope = _rope(y[:, off:off + LANES], cos_b, sin_b).astype(BF16)

    qf = jnp.dot(cq, w_uq_ref[...], preferred_element_type=F32)
    kv = jnp.dot(ckv, w_ukv_ref[...], preferred_element_type=F32)
    for h in range(B_HEADS):
        base = h * B_QK_PAD
        qb_ref[:, base:base + LANES] = (qf[:, base:base + LANES] * b_scale).astype(BF16)
        qb_ref[:, base + LANES:base + B_QK_PAD] = (
            _rope(qf[:, base + LANES:base + B_QK_PAD], cos_b, sin_b) * b_scale).astype(BF16)
        kb_ref[:, base:base + LANES] = kv[:, h * LANES:(h + 1) * LANES].astype(BF16)
        kb_ref[:, base + LANES:base + B_QK_PAD] = k_rope
        v_col = B_HEADS * B_NOPE_DIM + h * LANES
        vb_ref[:, 2 * h * LANES:(2 * h + 1) * LANES] = kv[:, v_col:v_col + LANES].astype(BF16)
        vb_ref[:, (2 * h + 1) * LANES:(2 * h + 2) * LANES] = ones


def _mixer_inproj(x, seq, pre_g, w_in, gq, gk, gcq, w_uq, gckv, w_ukv, tabs, *, tm=512):
    t, d = x.shape
    nblk = seq // tm
    row = lambda i: (i, 0)
    const = lambda i: (0, 0)
    pos = lambda i: (i % nblk, 0)
    full = lambda a: pl.BlockSpec(a.shape, const, pipeline_mode=pl.Buffered(1))
    widths = (A_HEADS * A_HEAD_DIM, A_KV_HEADS * A_HEAD_DIM, 2 * A_KV_HEADS * A_HEAD_DIM,
              B_HEADS * B_QK_PAD, B_HEADS * B_QK_PAD, 2 * B_HEADS * B_V_DIM)
    return pl.pallas_call(
        _inproj_kernel,
        grid=(t // tm,),
        in_specs=[pl.BlockSpec((tm, d), row), full(pre_g), full(w_in), full(gq), full(gk), full(gcq),
                  full(w_uq), full(gckv), full(w_ukv)] + [pl.BlockSpec((tm, LANES), pos)] * 4,
        out_specs=[pl.BlockSpec((tm, w), row) for w in widths],
        out_shape=[jax.ShapeDtypeStruct((t, w), BF16) for w in widths],
        compiler_params=_cparams("parallel"),
        name="mixer_inproj",
    )(x, pre_g, w_in, gq, gk, gcq, w_uq, gckv, w_ukv, *tabs)


def _attn_kernel(q_ref, k_ref, v_ref, o_ref, *, group, dk, dv, tk, rb):
    tq = q_ref.shape[0]
    seq = k_ref.shape[0]
    per_head = tq // rb
    nblocks = group * per_head

    def q_block(i):
        j, r = divmod(i, per_head)
        return q_ref[r * rb:(r + 1) * rb, j * dk:(j + 1) * dk]

    def body(c, carry):
        ms, accs = carry
        start = pl.multiple_of(c * tk, tk)
        k = k_ref[pl.ds(start, tk), :]
        v = v_ref[pl.ds(start, tk), :]
        new_ms, new_accs = [], []
        for i in range(nblocks):
            s = lax.dot_general(q_block(i), k, (((1,), (1,)), ((), ())), preferred_element_type=F32)
            m_new = jnp.maximum(ms[i], jnp.max(s, axis=-1, keepdims=True))
            alpha = jnp.exp2(ms[i] - m_new)
            p = jnp.exp2(s - m_new).astype(BF16)
            new_accs.append(alpha * accs[i] + jnp.dot(p, v, preferred_element_type=F32))
            new_ms.append(m_new)
        return tuple(new_ms), tuple(new_accs)

    init = (tuple(jnp.full((rb, 1), -jnp.inf, F32) for _ in range(nblocks)),
            tuple(jnp.zeros((rb, 2 * dv), F32) for _ in range(nblocks)))
    _, accs = lax.fori_loop(0, seq // tk, body, init, unroll=ATTN_CHUNK_UNROLL)
    for i in range(nblocks):
        j, r = divmod(i, per_head)
        out = accs[i][:, :dv] * (1.0 / accs[i][:, dv:])
        o_ref[r * rb:(r + 1) * rb, j * dv:(j + 1) * dv] = out.astype(o_ref.dtype)


def _attention(q, k, v, *, kv_heads, group, dk, dv, tq, tk=512, rb=256):
    b, seq, _ = q.shape
    tq, tk = min(tq, seq), min(tk, seq)
    kern = functools.partial(_attn_kernel, group=group, dk=dk, dv=dv, tk=tk, rb=min(rb, tq))
    return pl.pallas_call(
        kern,
        grid=(b, kv_heads, seq // tq),
        in_specs=[
            pl.BlockSpec((None, tq, group * dk), lambda bi, h, i: (bi, i, h)),
            pl.BlockSpec((None, seq, dk), lambda bi, h, i: (bi, 0, h)),
            pl.BlockSpec((None, seq, 2 * dv), lambda bi, h, i: (bi, 0, h)),
        ],
        out_specs=pl.BlockSpec((None, tq, group * dv), lambda bi, h, i: (bi, i, h)),
        out_shape=jax.ShapeDtypeStruct((b, seq, kv_heads * group * dv), BF16),
        compiler_params=_cparams("parallel", "parallel", "arbitrary"),
        name=f"attention_g{group}_dk{dk}",
    )(q, k, v)


def _outproj_kernel(x_ref, oa_ref, ob_ref, wa_ref, wb_ref, post_g_ref, o_ref):
    for r in range(x_ref.shape[0] // PROJ_SUB_ROWS):
        rows = slice(r * PROJ_SUB_ROWS, (r + 1) * PROJ_SUB_ROWS)
        m = jnp.dot(oa_ref[rows, :], wa_ref[...], preferred_element_type=F32)
        m += jnp.dot(ob_ref[rows, :], wb_ref[...], preferred_element_type=F32)
        o_ref[rows, :] = x_ref[rows, :] + _rms(m, post_g_ref[...])


def _mixer_outproj(x, oa, ob, wa, wb, post_g, *, tm=512):
    t, d = x.shape
    row = lambda i: (i, 0)
    const = lambda i: (0, 0)
    return pl.pallas_call(
        _outproj_kernel,
        grid=(t // tm,),
        in_specs=[pl.BlockSpec((tm, d), row), pl.BlockSpec((tm, oa.shape[1]), row),
                  pl.BlockSpec((tm, ob.shape[1]), row),
                  pl.BlockSpec(wa.shape, const, pipeline_mode=pl.Buffered(1)),
                  pl.BlockSpec(wb.shape, const, pipeline_mode=pl.Buffered(1)), pl.BlockSpec((1, d), const)],
        out_specs=pl.BlockSpec((tm, d), row),
        out_shape=jax.ShapeDtypeStruct((t, d), F32),
        compiler_params=_cparams("parallel"),
        name="mixer_outproj",
    )(x, oa, ob, wa, wb, post_g)


def _pool_kernel(x_ref, prev_ref, next_ref, pre_g_ref, w_ref, scale_ref, post_g_ref, o_ref, ext_ref, m_ref,
                 *, seq):
    tm, d = x_ref.shape
    group = d // len(POOL_WINDOWS)
    nblk = seq // tm
    blk = pl.program_id(0) % nblk
    g = pre_g_ref[...]
    x = x_ref[...]
    ext_ref[0:POOL_HALO, :] = jnp.where(blk > 0, _rms(prev_ref[...], g), 0.0)
    ext_ref[POOL_HALO:POOL_HALO + tm, :] = _rms(x, g)
    ext_ref[POOL_HALO + tm:, :] = jnp.where(blk < nblk - 1, _rms(next_ref[...], g), 0.0)

    t = blk * tm + lax.broadcasted_iota(jnp.int32, (tm, 1), 0)
    n_ext = tm + 2 * POOL_HALO
    for gi, w in enumerate(POOL_WINDOWS):
        cols = slice(gi * group, (gi + 1) * group)
        half = w // 2
        run = ext_ref[:, cols]
        span = 1
        while span < w:
            run = run + pltpu.roll(run, span, 0)
            span *= 2
        lead = half - 1
        if lead:
            run = pltpu.roll(run, n_ext - lead, 0)
        tot = run[POOL_HALO:POOL_HALO + tm]
        cnt = (jnp.minimum(t + half, seq) - jnp.maximum(t - half, 0)).astype(F32)
        pooled = tot * (1.0 / cnt) - ext_ref[POOL_HALO:POOL_HALO + tm, cols]
        m_ref[:, cols] = jnp.dot(pooled.astype(BF16), w_ref[gi], preferred_element_type=F32)
    o_ref[...] = x + _rms(m_ref[...] * scale_ref[...], post_g_ref[...])


def _pool_mixer(x, seq, pre_g, pool_w, pool_scale, post_g, *, tm=512):
    t, d = x.shape
    hb = tm // POOL_HALO
    last = t // POOL_HALO - 1
    row = lambda i: (i, 0)
    const = lambda i: (0, 0)
    return pl.pallas_call(
        functools.partial(_pool_kernel, seq=seq),
        grid=(t // tm,),
        in_specs=[
            pl.BlockSpec((tm, d), row),
            pl.BlockSpec((POOL_HALO, d), lambda i: (jnp.maximum(i * hb - 1, 0), 0)),
            pl.BlockSpec((POOL_HALO, d), lambda i: (jnp.minimum((i + 1) * hb, last), 0)),
            pl.BlockSpec((1, d), const),
            pl.BlockSpec(pool_w.shape, lambda i: (0, 0, 0)),
            pl.BlockSpec((1, d), const),
            pl.BlockSpec((1, d), const),
        ],
        out_specs=pl.BlockSpec((tm, d), row),
        out_shape=jax.ShapeDtypeStruct((t, d), F32),
        scratch_shapes=[pltpu.VMEM((tm + 2 * POOL_HALO, d), F32), pltpu.VMEM((tm, d), F32)],
        compiler_params=_cparams("parallel"),
        name="pool_mixer",
    )(x, x, x, pre_g, pool_w, pool_scale, post_g)


def _rope_tables(seq):
    rows = seq // GRID_W
    row = jnp.repeat(jnp.arange(rows, dtype=F32), GRID_W)
    col = jnp.tile(jnp.arange(GRID_W, dtype=F32), rows)

    def angles(rot_dim):
        half = rot_dim // 2
        freqs = ROPE_BASE ** (-jnp.arange(0, half, 2, dtype=F32) / half)
        return jnp.concatenate([row[:, None] * freqs, col[:, None] * freqs], axis=-1)

    ang_a = angles(A_HEAD_DIM)
    cos_a = jnp.concatenate([jnp.cos(ang_a)] * 2, axis=-1)
    sin_a = jnp.concatenate([-jnp.sin(ang_a), jnp.sin(ang_a)], axis=-1)
    ang_b = angles(B_ROPE_DIM)
    z = jnp.zeros_like(ang_b)
    cos_b = jnp.concatenate([jnp.cos(ang_b), z, jnp.cos(ang_b), z], axis=-1)
    sin_b = jnp.concatenate([-jnp.sin(ang_b), z, jnp.sin(ang_b), z], axis=-1)
    return cos_a, sin_a, cos_b, sin_b


def _pair_split_perm(n):
    return np.concatenate([np.arange(0, n, 2), np.arange(1, n, 2)])


def _spread_rope_cols(w):
    half = B_ROPE_DIM // 2
    z = jnp.zeros((w.shape[0], half), w.dtype)
    return jnp.concatenate([w[:, 0::2], z, w[:, 1::2], z], axis=-1)


def _prep_mixer_weights(w_in, gq, gk, w_uq, w_ukv, w_out):
    d = w_in.shape[0]
    perm = _pair_split_perm(A_HEAD_DIM)
    nq, nkv = A_HEADS * A_HEAD_DIM, A_KV_HEADS * A_HEAD_DIM
    wq = w_in[:, :nq].reshape(d, A_HEADS, A_HEAD_DIM)[:, :, perm].reshape(d, nq)
    wk = w_in[:, nq:nq + nkv].reshape(d, A_KV_HEADS, A_HEAD_DIM)[:, :, perm].reshape(d, nkv)
    rest = w_in[:, nq + nkv:-B_ROPE_DIM]
    w_in_p = jnp.concatenate([wq, wk, rest, _spread_rope_cols(w_in[:, -B_ROPE_DIM:])], axis=-1).astype(BF16)

    uq = w_uq.reshape(B_Q_LORA, B_HEADS, B_NOPE_DIM + B_ROPE_DIM)
    uq_rope = _spread_rope_cols(uq[:, :, B_NOPE_DIM:].reshape(B_Q_LORA * B_HEADS, B_ROPE_DIM))
    uq_p = jnp.concatenate([uq[:, :, :B_NOPE_DIM], uq_rope.reshape(B_Q_LORA, B_HEADS, LANES)], axis=-1)
    uq_p = uq_p.reshape(B_Q_LORA, B_HEADS * B_QK_PAD).astype(BF16)

    ukv = w_ukv.reshape(B_KV_LORA, B_HEADS, B_NOPE_DIM + B_V_DIM)
    ukv_p = jnp.concatenate([ukv[:, :, :B_NOPE_DIM].reshape(B_KV_LORA, -1),
                             ukv[:, :, B_NOPE_DIM:].reshape(B_KV_LORA, -1)], axis=-1).astype(BF16)
    na = A_HEADS * A_HEAD_DIM
    return (w_in_p, gq[perm][None, :], gk[perm][None, :], uq_p, ukv_p,
            w_out[:na].astype(BF16), w_out[na:].astype(BF16))


def _trunk(x, ffn_params, mix0, mix1, tabs):
    b, seq, d = x.shape
    x = x.reshape(b * seq, d)
    (pre0, w_in, gq, gk, gcq, w_uq, gckv, w_ukv, wa, wb, post0) = mix0
    (pre1, pool_w, pool_scale, post1) = mix1

    x = _ffn_half(x, *ffn_params[0])
    qa, ka, va, qb, kb, vb = _mixer_inproj(x, seq, pre0, w_in, gq, gk, gcq, w_uq, gckv, w_ukv, tabs)
    shp = lambda a: a.reshape(b, seq, a.shape[-1])
    oa = _attention(shp(qa), shp(ka), shp(va), kv_heads=A_KV_HEADS, group=A_HEADS // A_KV_HEADS,
                    dk=A_HEAD_DIM, dv=A_HEAD_DIM, tq=512)
    ob = _attention(shp(qb), shp(kb), shp(vb), kv_heads=B_HEADS, group=1, dk=B_QK_PAD, dv=B_V_DIM, tq=2048)
    x = _mixer_outproj(x, oa.reshape(b * seq, -1), ob.reshape(b * seq, -1), wa, wb, post0)
    x = _ffn_half(x, *ffn_params[1])
    x = _ffn_half(x, *ffn_params[2])
    x = _pool_mixer(x, seq, pre1, pool_w, pool_scale, post1)
    x = _ffn_half(x, *ffn_params[3])
    return x.reshape(b, seq, d)


def kernel(x_prompt, x_sample, l0_ffn1_pre_g, l0_ffn1_w_gate, l0_ffn1_w_up, l0_ffn1_w_down, l0_ffn1_post_g, l0_mix_pre_g, l0_w_in, l0_a_q_norm_g, l0_a_k_norm_g, l0_b_cq_norm_g, l0_b_w_uq, l0_b_ckv_norm_g, l0_b_w_ukv, l0_w_out, l0_mix_post_g, l0_ffn2_pre_g, l0_ffn2_w_gate, l0_ffn2_w_up, l0_ffn2_w_down, l0_ffn2_post_g, l1_ffn1_pre_g, l1_ffn1_w_gate, l1_ffn1_w_up, l1_ffn1_w_down, l1_ffn1_post_g, l1_mix_pre_g, l1_pool_w, l1_pool_scale, l1_mix_post_g, l1_ffn2_pre_g, l1_ffn2_w_gate, l1_ffn2_w_up, l1_ffn2_w_down, l1_ffn2_post_g):
    vec = lambda g: g[None, :]
    ffn = lambda pre, wg, wu, wd, post: (vec(pre), wg.astype(BF16), wu.astype(BF16), wd.astype(BF16), vec(post))
    ffn_params = (
        ffn(l0_ffn1_pre_g, l0_ffn1_w_gate, l0_ffn1_w_up, l0_ffn1_w_down, l0_ffn1_post_g),
        ffn(l0_ffn2_pre_g, l0_ffn2_w_gate, l0_ffn2_w_up, l0_ffn2_w_down, l0_ffn2_post_g),
        ffn(l1_ffn1_pre_g, l1_ffn1_w_gate, l1_ffn1_w_up, l1_ffn1_w_down, l1_ffn1_post_g),
        ffn(l1_ffn2_pre_g, l1_ffn2_w_gate, l1_ffn2_w_up, l1_ffn2_w_down, l1_ffn2_post_g),
    )
    w_in, gq, gk, w_uq, w_ukv, wa, wb = _prep_mixer_weights(
        l0_w_in, l0_a_q_norm_g, l0_a_k_norm_g, l0_b_w_uq, l0_b_w_ukv, l0_w_out)
    mix0 = (vec(l0_mix_pre_g), w_in, gq, gk, vec(l0_b_cq_norm_g), w_uq, vec(l0_b_ckv_norm_g), w_ukv,
            wa, wb, vec(l0_mix_post_g))
    mix1 = (vec(l1_mix_pre_g), l1_pool_w.astype(BF16), vec(l1_pool_scale), vec(l1_mix_post_g))

    outs = []
    for x in (x_prompt, x_sample):
        tabs = _rope_tables(x.shape[1])
        outs.append(_trunk(x, ffn_params, mix0, mix1, tabs))
    return tuple(outs)
```

```python
import functools
import math

import jax
import jax.numpy as jnp
import numpy as np
from jax import lax
from jax.experimental import pallas as pl
from jax.experimental.pallas import tpu as pltpu

F32 = jnp.float32
BF16 = jnp.bfloat16

NORM_EPS = 1e-6
ROPE_BASE = 10000.0
GRID_W = 64
A_HEADS = 8
A_KV_HEADS = 2
A_HEAD_DIM = 128
B_HEADS = 8
B_Q_LORA = 512
B_KV_LORA = 256
B_NOPE_DIM = 128
B_ROPE_DIM = 64
B_V_DIM = 128
POOL_WINDOWS = (2, 4, 8, 16)
LANES = 128
LOG2_E = math.log2(math.e)
ATTN_CHUNK_UNROLL = 16
B_QK_PAD = 2 * LANES
ROW_CHUNK = 32
ROW_CHUNK_UNROLL = 4
PROJ_SUB_ROWS = 256
FFN_SUB_ROWS = 512
POOL_HALO = 8

VMEM_LIMIT_BYTES = 60 * 1024 * 1024


def _cparams(*semantics):
    return pltpu.CompilerParams(dimension_semantics=semantics, vmem_limit_bytes=VMEM_LIMIT_BYTES)


def _rms(x, g):
    return x * lax.rsqrt(jnp.mean(x * x, axis=-1, keepdims=True) + NORM_EPS) * g


def _silu(x):
    return x * (1.0 / (1.0 + jnp.exp(-x)))


def _for_row_chunks(nrows, fn):
    def body(i, carry):
        fn(pl.ds(pl.multiple_of(i * ROW_CHUNK, ROW_CHUNK), ROW_CHUNK))
        return carry

    lax.fori_loop(0, nrows // ROW_CHUNK, body, 0, unroll=ROW_CHUNK_UNROLL)


def _ffn_step(x_ref, pre_g_ref, wg_ref, wu_ref, wd_ref, post_g_ref, o_ref, h_ref, *, first, last):
    tm = x_ref.shape[0]
    sub = FFN_SUB_ROWS if (first or last) else tm
    for r in range(tm // sub):
        base = r * sub
        rows = slice(base, base + sub)
        if first:
            for c in range(sub // ROW_CHUNK):
                chunk = slice(base + c * ROW_CHUNK, base + (c + 1) * ROW_CHUNK)
                h_ref[chunk, :] = _rms(x_ref[chunk, :], pre_g_ref[...]).astype(BF16)
        h = h_ref[rows, :]
        gate = jnp.dot(h, wg_ref[...], preferred_element_type=F32)
        up = jnp.dot(h, wu_ref[...], preferred_element_type=F32)
        act = (_silu(gate) * up).astype(BF16)
        down = jnp.dot(act, wd_ref[...], preferred_element_type=F32)
        if first:
            o_ref[rows, :] = down
        else:
            o_ref[rows, :] += down
        if last:
            for c in range(sub // ROW_CHUNK):
                chunk = slice(base + c * ROW_CHUNK, base + (c + 1) * ROW_CHUNK)
                o_ref[chunk, :] = x_ref[chunk, :] + 0.5 * _rms(o_ref[chunk, :], post_g_ref[...])


def _ffn_kernel(*refs):
    f = pl.program_id(1)
    nf = pl.num_programs(1)
    pl.when(f == 0)(functools.partial(_ffn_step, *refs, first=True, last=False))
    pl.when(jnp.logical_and(f > 0, f < nf - 1))(functools.partial(_ffn_step, *refs, first=False, last=False))
    pl.when(f == nf - 1)(functools.partial(_ffn_step, *refs, first=False, last=True))


def _ffn_half(x, pre_g, wg, wu, wd, post_g, *, tm=1024, tf=512):
    t, d = x.shape
    dff = wg.shape[1]
    row = lambda i, f: (i, 0)
    const = lambda i, f: (0, 0)
    return pl.pallas_call(
        _ffn_kernel,
        grid=(t // tm, dff // tf),
        in_specs=[
            pl.BlockSpec((tm, d), row),
            pl.BlockSpec((1, d), const),
            pl.BlockSpec((d, tf), lambda i, f: (0, f)),
            pl.BlockSpec((d, tf), lambda i, f: (0, f)),
            pl.BlockSpec((tf, d), lambda i, f: (f, 0)),
            pl.BlockSpec((1, d), const),
        ],
        out_specs=pl.BlockSpec((tm, d), row),
        out_shape=jax.ShapeDtypeStruct((t, d), F32),
        scratch_shapes=[pltpu.VMEM((tm, d), BF16)],
        compiler_params=_cparams("parallel", "arbitrary"),
        name="ffn_half",
    )(x, pre_g, wg, wu, wd, post_g)


def _rope(x, cos, sin):
    return x * cos + pltpu.roll(x, LANES // 2, 1) * sin


def _inproj_kernel(x_ref, pre_g_ref, w_in_ref, gq_ref, gk_ref, gcq_ref, w_uq_ref, gckv_ref, w_ukv_ref,
                   cos_a_ref, sin_a_ref, cos_b_ref, sin_b_ref, *out_refs):
    for r in range(x_ref.shape[0] // PROJ_SUB_ROWS):
        rows = pl.ds(r * PROJ_SUB_ROWS, PROJ_SUB_ROWS)
        at = lambda ref: ref.at[rows]
        _inproj_rows(at(x_ref), pre_g_ref, w_in_ref, gq_ref, gk_ref, gcq_ref, w_uq_ref, gckv_ref, w_ukv_ref,
                     at(cos_a_ref), at(sin_a_ref), at(cos_b_ref), at(sin_b_ref), *map(at, out_refs))


def _inproj_rows(x_ref, pre_g_ref, w_in_ref, gq_ref, gk_ref, gcq_ref, w_uq_ref, gckv_ref, w_ukv_ref,
                 cos_a_ref, sin_a_ref, cos_b_ref, sin_b_ref,
                 qa_ref, ka_ref, va_ref, qb_ref, kb_ref, vb_ref):
    hn = _rms(x_ref[...], pre_g_ref[...]).astype(BF16)
    y = jnp.dot(hn, w_in_ref[...], preferred_element_type=F32)
    cos_a, sin_a = cos_a_ref[...], sin_a_ref[...]
    cos_b, sin_b = cos_b_ref[...], sin_b_ref[...]
    a_scale = A_HEAD_DIM ** -0.5 * LOG2_E
    b_scale = (B_NOPE_DIM + B_ROPE_DIM) ** -0.5 * LOG2_E
    ones = jnp.ones((x_ref.shape[0], LANES), BF16)
    q_cols = A_HEADS * A_HEAD_DIM
    kv_cols = A_KV_HEADS * A_HEAD_DIM

    for h in range(A_HEADS):
        sl = slice(h * LANES, (h + 1) * LANES)
        qa_ref[:, sl] = (_rope(_rms(y[:, sl], gq_ref[...]), cos_a, sin_a) * a_scale).astype(BF16)
    for h in range(A_KV_HEADS):
        sl = slice(h * LANES, (h + 1) * LANES)
        ka_ref[:, sl] = _rope(_rms(y[:, q_cols + h * LANES:q_cols + (h + 1) * LANES], gk_ref[...]),
                              cos_a, sin_a).astype(BF16)
    off = q_cols + kv_cols
    for h in range(A_KV_HEADS):
        va_ref[:, 2 * h * LANES:(2 * h + 1) * LANES] = y[:, off + h * LANES:off + (h + 1) * LANES].astype(BF16)
        va_ref[:, (2 * h + 1) * LANES:(2 * h + 2) * LANES] = ones
    off += kv_cols

    cq = _rms(y[:, off:off + B_Q_LORA], gcq_ref[...]).astype(BF16)
    off += B_Q_LORA
    ckv = _rms(y[:, off:off + B_KV_LORA], gckv_ref[...]).astype(BF16)
    off += B_KV_LORA
    k_rope = _rope(y[:, off:off + LANES], cos_b, sin_b).astype(BF16)

    qf = jnp.dot(cq, w_uq_ref[...], preferred_element_type=F32)
    kv = jnp.dot(ckv, w_ukv_ref[...], preferred_element_type=F32)
    for h in range(B_HEADS):
        base = h * B_QK_PAD
        qb_ref[:, base:base + LANES] = (qf[:, base:base + LANES] * b_scale).astype(BF16)
        qb_ref[:, base + LANES:base + B_QK_PAD] = (
            _rope(qf[:, base + LANES:base + B_QK_PAD], cos_b, sin_b) * b_scale).astype(BF16)
        kb_ref[:, base:base + LANES] = kv[:, h * LANES:(h + 1) * LANES].astype(BF16)
        kb_ref[:, base + LANES:base + B_QK_PAD] = k_rope
        v_col = B_HEADS * B_NOPE_DIM + h * LANES
        vb_ref[:, 2 * h * LANES:(2 * h + 1) * LANES] = kv[:, v_col:v_col + LANES].astype(BF16)
        vb_ref[:, (2 * h + 1) * LANES:(2 * h + 2) * LANES] = ones


def _mixer_inproj(x, seq, pre_g, w_in, gq, gk, gcq, w_uq, gckv, w_ukv, tabs, *, tm=512):
    t, d = x.shape
    nblk = seq // tm
    row = lambda i: (i, 0)
    const = lambda i: (0, 0)
    pos = lambda i: (i % nblk, 0)
    full = lambda a: pl.BlockSpec(a.shape, const, pipeline_mode=pl.Buffered(1))
    widths = (A_HEADS * A_HEAD_DIM, A_KV_HEADS * A_HEAD_DIM, 2 * A_KV_HEADS * A_HEAD_DIM,
              B_HEADS * B_QK_PAD, B_HEADS * B_QK_PAD, 2 * B_HEADS * B_V_DIM)
    return pl.pallas_call(
        _inproj_kernel,
        grid=(t // tm,),
        in_specs=[pl.BlockSpec((tm, d), row), full(pre_g), full(w_in), full(gq), full(gk), full(gcq),
                  full(w_uq), full(gckv), full(w_ukv)] + [pl.BlockSpec((tm, LANES), pos)] * 4,
        out_specs=[pl.BlockSpec((tm, w), row) for w in widths],
        out_shape=[jax.ShapeDtypeStruct((t, w), BF16) for w in widths],
        compiler_params=_cparams("parallel"),
        name="mixer_inproj",
    )(x, pre_g, w_in, gq, gk, gcq, w_uq, gckv, w_ukv, *tabs)


def _attn_kernel(q_ref, k_ref, v_ref, o_ref, *, group, dk, dv, tk, rb):
    tq = q_ref.shape[0]
    seq = k_ref.shape[0]
    per_head = tq // rb
    nblocks = group * per_head

    def q_block(i):
        j, r = divmod(i, per_head)
        return q_ref[r * rb:(r + 1) * rb, j * dk:(j + 1) * dk]

    def body(c, carry):
        ms, accs = carry
        start = pl.multiple_of(c * tk, tk)
        k = k_ref[pl.ds(start, tk), :]
        v = v_ref[pl.ds(start, tk), :]
        new_ms, new_accs = [], []
        for i in range(nblocks):
            s = lax.dot_general(q_block(i), k, (((1,), (1,)), ((), ())), preferred_element_type=F32)
            m_new = jnp.maximum(ms[i], jnp.max(s, axis=-1, keepdims=True))
            alpha = jnp.exp2(ms[i] - m_new)
            p = jnp.exp2(s - m_new).astype(BF16)
            new_accs.append(alpha * accs[i] + jnp.dot(p, v, preferred_element_type=F32))
            new_ms.append(m_new)
        return tuple(new_ms), tuple(new_accs)

    init = (tuple(jnp.full((rb, 1), -jnp.inf, F32) for _ in range(nblocks)),
            tuple(jnp.zeros((rb, 2 * dv), F32) for _ in range(nblocks)))
    _, accs = lax.fori_loop(0, seq // tk, body, init, unroll=ATTN_CHUNK_UNROLL)
    for i in range(nblocks):
        j, r = divmod(i, per_head)
        out = accs[i][:, :dv] * (1.0 / accs[i][:, dv:])
        o_ref[r * rb:(r + 1) * rb, j * dv:(j + 1) * dv] = out.astype(o_ref.dtype)


def _attention(q, k, v, *, kv_heads, group, dk, dv, tq, tk=512, rb=256):
    b, seq, _ = q.shape
    tq, tk = min(tq, seq), min(tk, seq)
    kern = functools.partial(_attn_kernel, group=group, dk=dk, dv=dv, tk=tk, rb=min(rb, tq))
    return pl.pallas_call(
        kern,
        grid=(b, kv_heads, seq // tq),
        in_specs=[
            pl.BlockSpec((None, tq, group * dk), lambda bi, h, i: (bi, i, h)),
            pl.BlockSpec((None, seq, dk), lambda bi, h, i: (bi, 0, h)),
            pl.BlockSpec((None, seq, 2 * dv), lambda bi, h, i: (bi, 0, h)),
        ],
        out_specs=pl.BlockSpec((None, tq, group * dv), lambda bi, h, i: (bi, i, h)),
        out_shape=jax.ShapeDtypeStruct((b, seq, kv_heads * group * dv), BF16),
        compiler_params=_cparams("parallel", "parallel", "arbitrary"),
        name=f"attention_g{group}_dk{dk}",
    )(q, k, v)


def _outproj_kernel(x_ref, oa_ref, ob_ref, wa_ref, wb_ref, post_g_ref, o_ref):
    for r in range(x_ref.shape[0] // PROJ_SUB_ROWS):
        rows = slice(r * PROJ_SUB_ROWS, (r + 1) * PROJ_SUB_ROWS)
        m = jnp.dot(oa_ref[rows, :], wa_ref[...], preferred_element_type=F32)
        m += jnp.dot(ob_ref[rows, :], wb_ref[...], preferred_element_type=F32)
        o_ref[rows, :] = x_ref[rows, :] + _rms(m, post_g_ref[...])


def _mixer_outproj(x, oa, ob, wa, wb, post_g, *, tm=512):
    t, d = x.shape
    row = lambda i: (i, 0)
    const = lambda i: (0, 0)
    return pl.pallas_call(
        _outproj_kernel,
        grid=(t // tm,),
        in_specs=[pl.BlockSpec((tm, d), row), pl.BlockSpec((tm, oa.shape[1]), row),
                  pl.BlockSpec((tm, ob.shape[1]), row),
                  pl.BlockSpec(wa.shape, const, pipeline_mode=pl.Buffered(1)),
                  pl.BlockSpec(wb.shape, const, pipeline_mode=pl.Buffered(1)), pl.BlockSpec((1, d), const)],
        out_specs=pl.BlockSpec((tm, d), row),
        out_shape=jax.ShapeDtypeStruct((t, d), F32),
        compiler_params=_cparams("parallel"),
        name="mixer_outproj",
    )(x, oa, ob, wa, wb, post_g)


def _pool_kernel(x_ref, prev_ref, next_ref, pre_g_ref, w_ref, scale_ref, post_g_ref, o_ref, ext_ref, m_ref,
                 *, seq):
    tm, d = x_ref.shape
    group = d // len(POOL_WINDOWS)
    nblk = seq // tm
    blk = pl.program_id(0) % nblk
    g = pre_g_ref[...]
    x = x_ref[...]
    ext_ref[0:POOL_HALO, :] = jnp.where(blk > 0, _rms(prev_ref[...], g), 0.0)
    ext_ref[POOL_HALO:POOL_HALO + tm, :] = _rms(x, g)
    ext_ref[POOL_HALO + tm:, :] = jnp.where(blk < nblk - 1, _rms(next_ref[...], g), 0.0)

    t = blk * tm + lax.broadcasted_iota(jnp.int32, (tm, 1), 0)
    n_ext = tm + 2 * POOL_HALO
    for gi, w in enumerate(POOL_WINDOWS):
        cols = slice(gi * group, (gi + 1) * group)
        half = w // 2
        run = ext_ref[:, cols]
        span = 1
        while span < w:
            run = run + pltpu.roll(run, span, 0)
            span *= 2
        lead = half - 1
        if lead:
            run = pltpu.roll(run, n_ext - lead, 0)
        tot = run[POOL_HALO:POOL_HALO + tm]
        cnt = (jnp.minimum(t + half, seq) - jnp.maximum(t - half, 0)).astype(F32)
        pooled = tot * (1.0 / cnt) - ext_ref[POOL_HALO:POOL_HALO + tm, cols]
        m_ref[:, cols] = jnp.dot(pooled.astype(BF16), w_ref[gi], preferred_element_type=F32)
    o_ref[...] = x + _rms(m_ref[...] * scale_ref[...], post_g_ref[...])


def _pool_mixer(x, seq, pre_g, pool_w, pool_scale, post_g, *, tm=512):
    t, d = x.shape
    hb = tm // POOL_HALO
    last = t // POOL_HALO - 1
    row = lambda i: (i, 0)
    const = lambda i: (0, 0)
    return pl.pallas_call(
        functools.partial(_pool_kernel, seq=seq),
        grid=(t // tm,),
        in_specs=[
            pl.BlockSpec((tm, d), row),
            pl.BlockSpec((POOL_HALO, d), lambda i: (jnp.maximum(i * hb - 1, 0), 0)),
            pl.BlockSpec((POOL_HALO, d), lambda i: (jnp.minimum((i + 1) * hb, last), 0)),
            pl.BlockSpec((1, d), const),
            pl.BlockSpec(pool_w.shape, lambda i: (0, 0, 0)),
            pl.BlockSpec((1, d), const),
            pl.BlockSpec((1, d), const),
        ],
        out_specs=pl.BlockSpec((tm, d), row),
        out_shape=jax.ShapeDtypeStruct((t, d), F32),
        scratch_shapes=[pltpu.VMEM((tm + 2 * POOL_HALO, d), F32), pltpu.VMEM((tm, d), F32)],
        compiler_params=_cparams("parallel"),
        name="pool_mixer",
    )(x, x, x, pre_g, pool_w, pool_scale, post_g)


def _rope_tables(seq):
    rows = seq // GRID_W
    row = jnp.repeat(jnp.arange(rows, dtype=F32), GRID_W)
    col = jnp.tile(jnp.arange(GRID_W, dtype=F32), rows)

    def angles(rot_dim):
        half = rot_dim // 2
        freqs = ROPE_BASE ** (-jnp.arange(0, half, 2, dtype=F32) / half)
        return jnp.concatenate([row[:, None] * freqs, col[:, None] * freqs], axis=-1)

    ang_a = angles(A_HEAD_DIM)
    cos_a = jnp.concatenate([jnp.cos(ang_a)] * 2, axis=-1)
    sin_a = jnp.concatenate([-jnp.sin(ang_a), jnp.sin(ang_a)], axis=-1)
    ang_b = angles(B_ROPE_DIM)
    z = jnp.zeros_like(ang_b)
    cos_b = jnp.concatenate([jnp.cos(ang_b), z, jnp.cos(ang_b), z], axis=-1)
    sin_b = jnp.concatenate([-jnp.sin(ang_b), z, jnp.sin(ang_b), z], axis=-1)
    return cos_a, sin_a, cos_b, sin_b


def _pair_split_perm(n):
    return np.concatenate([np.arange(0, n, 2), np.arange(1, n, 2)])


def _spread_rope_cols(w):
    half = B_ROPE_DIM // 2
    z = jnp.zeros((w.shape[0], half), w.dtype)
    return jnp.concatenate([w[:, 0::2], z, w[:, 1::2], z], axis=-1)


def _prep_mixer_weights(w_in, gq, gk, w_uq, w_ukv, w_out):
    d = w_in.shape[0]
    perm = _pair_split_perm(A_HEAD_DIM)
    nq, nkv = A_HEADS * A_HEAD_DIM, A_KV_HEADS * A_HEAD_DIM
    wq = w_in[:, :nq].reshape(d, A_HEADS, A_HEAD_DIM)[:, :, perm].reshape(d, nq)
    wk = w_in[:, nq:nq + nkv].reshape(d, A_KV_HEADS, A_HEAD_DIM)[:, :, perm].reshape(d, nkv)
    rest = w_in[:, nq + nkv:-B_ROPE_DIM]
    w_in_p = jnp.concatenate([wq, wk, rest, _spread_rope_cols(w_in[:, -B_ROPE_DIM:])], axis=-1).astype(BF16)

    uq = w_uq.reshape(B_Q_LORA, B_HEADS, B_NOPE_DIM + B_ROPE_DIM)
    uq_rope = _spread_rope_cols(uq[:, :, B_NOPE_DIM:].reshape(B_Q_LORA * B_HEADS, B_ROPE_DIM))
    uq_p = jnp.concatenate([uq[:, :, :B_NOPE_DIM], uq_rope.reshape(B_Q_LORA, B_HEADS, LANES)], axis=-1)
    uq_p = uq_p.reshape(B_Q_LORA, B_HEADS * B_QK_PAD).astype(BF16)

    ukv = w_ukv.reshape(B_KV_LORA, B_HEADS, B_NOPE_DIM + B_V_DIM)
    ukv_p = jnp.concatenate([ukv[:, :, :B_NOPE_DIM].reshape(B_KV_LORA, -1),
                             ukv[:, :, B_NOPE_DIM:].reshape(B_KV_LORA, -1)], axis=-1).astype(BF16)
    na = A_HEADS * A_HEAD_DIM
    return (w_in_p, gq[perm][None, :], gk[perm][None, :], uq_p, ukv_p,
            w_out[:na].astype(BF16), w_out[na:].astype(BF16))


def _trunk(x, ffn_params, mix0, mix1, tabs):
    b, seq, d = x.shape
    x = x.reshape(b * seq, d)
    (pre0, w_in, gq, gk, gcq, w_uq, gckv, w_ukv, wa, wb, post0) = mix0
    (pre1, pool_w, pool_scale, post1) = mix1

    x = _ffn_half(x, *ffn_params[0])
    qa, ka, va, qb, kb, vb = _mixer_inproj(x, seq, pre0, w_in, gq, gk, gcq, w_uq, gckv, w_ukv, tabs)
    shp = lambda a: a.reshape(b, seq, a.shape[-1])
    oa = _attention(shp(qa), shp(ka), shp(va), kv_heads=A_KV_HEADS, group=A_HEADS // A_KV_HEADS,
                    dk=A_HEAD_DIM, dv=A_HEAD_DIM, tq=512)
    ob = _attention(shp(qb), shp(kb), shp(vb), kv_heads=B_HEADS, group=1, dk=B_QK_PAD, dv=B_V_DIM, tq=2048)
    x = _mixer_outproj(x, oa.reshape(b * seq, -1), ob.reshape(b * seq, -1), wa, wb, post0)
    x = _ffn_half(x, *ffn_params[1])
    x = _ffn_half(x, *ffn_params[2])
    x = _pool_mixer(x, seq, pre1, pool_w, pool_scale, post1)
    x = _ffn_half(x, *ffn_params[3])
    return x.reshape(b, seq, d)


def kernel(x_prompt, x_sample, l0_ffn1_pre_g, l0_ffn1_w_gate, l0_ffn1_w_up, l0_ffn1_w_down, l0_ffn1_post_g, l0_mix_pre_g, l0_w_in, l0_a_q_norm_g, l0_a_k_norm_g, l0_b_cq_norm_g, l0_b_w_uq, l0_b_ckv_norm_g, l0_b_w_ukv, l0_w_out, l0_mix_post_g, l0_ffn2_pre_g, l0_ffn2_w_gate, l0_ffn2_w_up, l0_ffn2_w_down, l0_ffn2_post_g, l1_ffn1_pre_g, l1_ffn1_w_gate, l1_ffn1_w_up, l1_ffn1_w_down, l1_ffn1_post_g, l1_mix_pre_g, l1_pool_w, l1_pool_scale, l1_mix_post_g, l1_ffn2_pre_g, l1_ffn2_w_gate, l1_ffn2_w_up, l1_ffn2_w_down, l1_ffn2_post_g):
    vec = lambda g: g[None, :]
    ffn = lambda pre, wg, wu, wd, post: (vec(pre), wg.astype(BF16), wu.astype(BF16), wd.astype(BF16), vec(post))
    ffn_params = (
        ffn(l0_ffn1_pre_g, l0_ffn1_w_gate, l0_ffn1_w_up, l0_ffn1_w_down, l0_ffn1_post_g),
        ffn(l0_ffn2_pre_g, l0_ffn2_w_gate, l0_ffn2_w_up, l0_ffn2_w_down, l0_ffn2_post_g),
        ffn(l1_ffn1_pre_g, l1_ffn1_w_gate, l1_ffn1_w_up, l1_ffn1_w_down, l1_ffn1_post_g),
        ffn(l1_ffn2_pre_g, l1_ffn2_w_gate, l1_ffn2_w_up, l1_ffn2_w_down, l1_ffn2_post_g),
    )
    w_in, gq, gk, w_uq, w_ukv, wa, wb = _prep_mixer_weights(
        l0_w_in, l0_a_q_norm_g, l0_a_k_norm_g, l0_b_w_uq, l0_b_w_ukv, l0_w_out)
    mix0 = (vec(l0_mix_pre_g), w_in, gq, gk, vec(l0_b_cq_norm_g), w_uq, vec(l0_b_ckv_norm_g), w_ukv,
            wa, wb, vec(l0_mix_post_g))
    mix1 = (vec(l1_mix_pre_g), l1_pool_w.astype(BF16), vec(l1_pool_scale), vec(l1_mix_post_g))

    outs = []
    for x in (x_prompt, x_sample):
        tabs = _rope_tables(x.shape[1])
        outs.append(_trunk(x, ffn_params, mix0, mix1, tabs))
    return tuple(outs)
```

```python
import functools
import math

import jax
import jax.numpy as jnp
import numpy as np
from jax import lax
from jax.experimental import pallas as pl
from jax.experimental.pallas import tpu as pltpu

F32 = jnp.float32
BF16 = jnp.bfloat16

NORM_EPS = 1e-6
ROPE_BASE = 10000.0
GRID_W = 64
A_HEADS = 8
A_KV_HEADS = 2
A_HEAD_DIM = 128
B_HEADS = 8
B_Q_LORA = 512
B_KV_LORA = 256
B_NOPE_DIM = 128
B_ROPE_DIM = 64
B_V_DIM = 128
POOL_WINDOWS = (2, 4, 8, 16)
LANES = 128
LOG2_E = math.log2(math.e)
ATTN_CHUNK_UNROLL = 16
B_QK_PAD = 2 * LANES
ROW_CHUNK = 32
ROW_CHUNK_UNROLL = 4
PROJ_SUB_ROWS = 512
FFN_SUB_ROWS = 512
POOL_HALO = 8

VMEM_LIMIT_BYTES = 60 * 1024 * 1024


def _cparams(*semantics):
    return pltpu.CompilerParams(dimension_semantics=semantics, vmem_limit_bytes=VMEM_LIMIT_BYTES)


def _rms(x, g):
    return x * lax.rsqrt(jnp.mean(x * x, axis=-1, keepdims=True) + NORM_EPS) * g


def _silu(x):
    return x * (1.0 / (1.0 + jnp.exp(-x)))


def _for_row_chunks(nrows, fn):
    def body(i, carry):
        fn(pl.ds(pl.multiple_of(i * ROW_CHUNK, ROW_CHUNK), ROW_CHUNK))
        return carry

    lax.fori_loop(0, nrows // ROW_CHUNK, body, 0, unroll=ROW_CHUNK_UNROLL)


def _ffn_step(x_ref, pre_g_ref, wg_ref, wu_ref, wd_ref, post_g_ref, o_ref, h_ref, *, first, last):
    tm = x_ref.shape[0]
    sub = FFN_SUB_ROWS if (first or last) else tm
    for r in range(tm // sub):
        base = r * sub
        rows = slice(base, base + sub)
        if first:
            for c in range(sub // ROW_CHUNK):
                chunk = slice(base + c * ROW_CHUNK, base + (c + 1) * ROW_CHUNK)
                h_ref[chunk, :] = _rms(x_ref[chunk, :], pre_g_ref[...]).astype(BF16)
        h = h_ref[rows, :]
        gate = jnp.dot(h, wg_ref[...], preferred_element_type=F32)
        up = jnp.dot(h, wu_ref[...], preferred_element_type=F32)
        act = (_silu(gate) * up).astype(BF16)
        down = jnp.dot(act, wd_ref[...], preferred_element_type=F32)
        if first:
            o_ref[rows, :] = down
        else:
            o_ref[rows, :] += down
        if last:
            for c in range(sub // ROW_CHUNK):
                chunk = slice(base + c * ROW_CHUNK, base + (c + 1) * ROW_CHUNK)
                o_ref[chunk, :] = x_ref[chunk, :] + 0.5 * _rms(o_ref[chunk, :], post_g_ref[...])


def _ffn_kernel(*refs):
    f = pl.program_id(1)
    nf = pl.num_programs(1)
    pl.when(f == 0)(functools.partial(_ffn_step, *refs, first=True, last=False))
    pl.when(jnp.logical_and(f > 0, f < nf - 1))(functools.partial(_ffn_step, *refs, first=False, last=False))
    pl.when(f == nf - 1)(functools.partial(_ffn_step, *refs, first=False, last=True))


def _ffn_half(x, pre_g, wg, wu, wd, post_g, *, tm=1024, tf=512):
    t, d = x.shape
    dff = wg.shape[1]
    row = lambda i, f: (i, 0)
    const = lambda i, f: (0, 0)
    return pl.pallas_call(
        _ffn_kernel,
        grid=(t // tm, dff // tf),
        in_specs=[
            pl.BlockSpec((tm, d), row),
            pl.BlockSpec((1, d), const),
            pl.BlockSpec((d, tf), lambda i, f: (0, f)),
            pl.BlockSpec((d, tf), lambda i, f: (0, f)),
            pl.BlockSpec((tf, d), lambda i, f: (f, 0)),
            pl.BlockSpec((1, d), const),
        ],
        out_specs=pl.BlockSpec((tm, d), row),
        out_shape=jax.ShapeDtypeStruct((t, d), F32),
        scratch_shapes=[pltpu.VMEM((tm, d), BF16)],
        compiler_params=_cparams("parallel", "arbitrary"),
        name="ffn_half",
    )(x, pre_g, wg, wu, wd, post_g)


def _rope(x, cos, sin):
    return x * cos + pltpu.roll(x, LANES // 2, 1) * sin


def _inproj_kernel(x_ref, pre_g_ref, w_in_ref, gq_ref, gk_ref, gcq_ref, w_uq_ref, gckv_ref, w_ukv_ref,
                   cos_a_ref, sin_a_ref, cos_b_ref, sin_b_ref, *out_refs):
    for r in range(x_ref.shape[0] // PROJ_SUB_ROWS):
        rows = pl.ds(r * PROJ_SUB_ROWS, PROJ_SUB_ROWS)
        at = lambda ref: ref.at[rows]
        _inproj_rows(at(x_ref), pre_g_ref, w_in_ref, gq_ref, gk_ref, gcq_ref, w_uq_ref, gckv_ref, w_ukv_ref,
                     at(cos_a_ref), at(sin_a_ref), at(cos_b_ref), at(sin_b_ref), *map(at, out_refs))


def _inproj_rows(x_ref, pre_g_ref, w_in_ref, gq_ref, gk_ref, gcq_ref, w_uq_ref, gckv_ref, w_ukv_ref,
                 cos_a_ref, sin_a_ref, cos_b_ref, sin_b_ref,
                 qa_ref, ka_ref, va_ref, qb_ref, kb_ref, vb_ref):
    hn = _rms(x_ref[...], pre_g_ref[...]).astype(BF16)
    y = jnp.dot(hn, w_in_ref[...], preferred_element_type=F32)
    cos_a, sin_a = cos_a_ref[...], sin_a_ref[...]
    cos_b, sin_b = cos_b_ref[...], sin_b_ref[...]
    a_scale = A_HEAD_DIM ** -0.5 * LOG2_E
    b_scale = (B_NOPE_DIM + B_ROPE_DIM) ** -0.5 * LOG2_E
    ones = jnp.ones((x_ref.shape[0], LANES), BF16)
    q_cols = A_HEADS * A_HEAD_DIM
    kv_cols = A_KV_HEADS * A_HEAD_DIM

    for h in range(A_HEADS):
        sl = slice(h * LANES, (h + 1) * LANES)
        qa_ref[:, sl] = (_rope(_rms(y[:, sl], gq_ref[...]), cos_a, sin_a) * a_scale).astype(BF16)
    for h in range(A_KV_HEADS):
        sl = slice(h * LANES, (h + 1) * LANES)
        ka_ref[:, sl] = _rope(_rms(y[:, q_cols + h * LANES:q_cols + (h + 1) * LANES], gk_ref[...]),
                              cos_a, sin_a).astype(BF16)
    off = q_cols + kv_cols
    for h in range(A_KV_HEADS):
        va_ref[:, 2 * h * LANES:(2 * h + 1) * LANES] = y[:, off + h * LANES:off + (h + 1) * LANES].astype(BF16)
        va_ref[:, (2 * h + 1) * LANES:(2 * h + 2) * LANES] = ones
    off += kv_cols

    cq = _rms(y[:, off:off + B_Q_LORA], gcq_ref[...]).astype(BF16)
    off += B_Q_LORA
    ckv = _rms(y[:, off:off + B_KV_LORA], gckv_ref[...]).astype(BF16)
    off += B_KV_LORA
    k_rope = _rope(y[:, off:off + LANES], cos_b, sin_b).astype(BF16)

    qf = jnp.dot(cq, w_uq_ref[...], preferred_element_type=F32)
    kv = jnp.dot(ckv, w_ukv_ref[...], preferred_element_type=F32)
    for h in range(B_HEADS):
        base = h * B_QK_PAD
        qb_ref[:, base:base + LANES] = (qf[:, base:base + LANES] * b_scale).astype(BF16)
        qb_ref[:, base + LANES:base + B_QK_PAD] = (
            _rope(qf[:, base + LANES:base + B_QK_PAD], cos_b, sin_b) * b_scale).astype(BF16)
        kb_ref[:, base:base + LANES] = kv[:, h * LANES:(h + 1) * LANES].astype(BF16)
        kb_ref[:, base + LANES:base + B_QK_PAD] = k_rope
        v_col = B_HEADS * B_NOPE_DIM + h * LANES
        vb_ref[:, 2 * h * LANES:(2 * h + 1) * LANES] = kv[:, v_col:v_col + LANES].astype(BF16)
        vb_ref[:, (2 * h + 1) * LANES:(2 * h + 2) * LANES] = ones


def _mixer_inproj(x, seq, pre_g, w_in, gq, gk, gcq, w_uq, gckv, w_ukv, tabs, *, tm=512):
    t, d = x.shape
    nblk = seq // tm
    row = lambda i: (i, 0)
    const = lambda i: (0, 0)
    pos = lambda i: (i % nblk, 0)
    full = lambda a: pl.BlockSpec(a.shape, const, pipeline_mode=pl.Buffered(1))
    widths = (A_HEADS * A_HEAD_DIM, A_KV_HEADS * A_HEAD_DIM, 2 * A_KV_HEADS * A_HEAD_DIM,
              B_HEADS * B_QK_PAD, B_HEADS * B_QK_PAD, 2 * B_HEADS * B_V_DIM)
    return pl.pallas_call(
        _inproj_kernel,
        grid=(t // tm,),
        in_specs=[pl.BlockSpec((tm, d), row), full(pre_g), full(w_in), full(gq), full(gk), full(gcq),
                  full(w_uq), full(gckv), full(w_ukv)] + [pl.BlockSpec((tm, LANES), pos)] * 4,
        out_specs=[pl.BlockSpec((tm, w), row) for w in widths],
        out_shape=[jax.ShapeDtypeStruct((t, w), BF16) for w in widths],
        compiler_params=_cparams("parallel"),
        name="mixer_inproj",
    )(x, pre_g, w_in, gq, gk, gcq, w_uq, gckv, w_ukv, *tabs)


def _attn_kernel(q_ref, k_ref, v_ref, o_ref, *, group, dk, dv, tk, rb):
    tq = q_ref.shape[0]
    seq = k_ref.shape[0]
    per_head = tq // rb
    nblocks = group * per_head

    def q_block(i):
        j, r = divmod(i, per_head)
        return q_ref[r * rb:(r + 1) * rb, j * dk:(j + 1) * dk]

    def body(c, carry):
        ms, accs = carry
        start = pl.multiple_of(c * tk, tk)
        k = k_ref[pl.ds(start, tk), :]
        v = v_ref[pl.ds(start, tk), :]
        new_ms, new_accs = [], []
        for i in range(nblocks):
            s = lax.dot_general(q_block(i), k, (((1,), (1,)), ((), ())), preferred_element_type=F32)
            m_new = jnp.maximum(ms[i], jnp.max(s, axis=-1, keepdims=True))
            alpha = jnp.exp2(ms[i] - m_new)
            p = jnp.exp2(s - m_new).astype(BF16)
            new_accs.append(alpha * accs[i] + jnp.dot(p, v, preferred_element_type=F32))
            new_ms.append(m_new)
        return tuple(new_ms), tuple(new_accs)

    init = (tuple(jnp.full((rb, 1), -jnp.inf, F32) for _ in range(nblocks)),
            tuple(jnp.zeros((rb, 2 * dv), F32) for _ in range(nblocks)))
    _, accs = lax.fori_loop(0, seq // tk, body, init, unroll=ATTN_CHUNK_UNROLL)
    for i in range(nblocks):
        j, r = divmod(i, per_head)
        out = accs[i][:, :dv] * (1.0 / accs[i][:, dv:])
        o_ref[r * rb:(r + 1) * rb, j * dv:(j + 1) * dv] = out.astype(o_ref.dtype)


def _attention(q, k, v, *, kv_heads, group, dk, dv, tq, tk=512, rb=512):
    b, seq, _ = q.shape
    tq, tk = min(tq, seq), min(tk, seq)
    kern = functools.partial(_attn_kernel, group=group, dk=dk, dv=dv, tk=tk, rb=min(rb, tq))
    return pl.pallas_call(
        kern,
        grid=(b, kv_heads, seq // tq),
        in_specs=[
            pl.BlockSpec((None, tq, group * dk), lambda bi, h, i: (bi, i, h)),
            pl.BlockSpec((None, seq, dk), lambda bi, h, i: (bi, 0, h)),
            pl.BlockSpec((None, seq, 2 * dv), lambda bi, h, i: (bi, 0, h)),
        ],
        out_specs=pl.BlockSpec((None, tq, group * dv), lambda bi, h, i: (bi, i, h)),
        out_shape=jax.ShapeDtypeStruct((b, seq, kv_heads * group * dv), BF16),
        compiler_params=_cparams("parallel", "parallel", "arbitrary"),
        name=f"attention_g{group}_dk{dk}",
    )(q, k, v)


def _outproj_kernel(x_ref, oa_ref, ob_ref, wa_ref, wb_ref, post_g_ref, o_ref):
    for r in range(x_ref.shape[0] // PROJ_SUB_ROWS):
        rows = slice(r * PROJ_SUB_ROWS, (r + 1) * PROJ_SUB_ROWS)
        m = jnp.dot(oa_ref[rows, :], wa_ref[...], preferred_element_type=F32)
        m += jnp.dot(ob_ref[rows, :], wb_ref[...], preferred_element_type=F32)
        o_ref[rows, :] = x_ref[rows, :] + _rms(m, post_g_ref[...])


def _mixer_outproj(x, oa, ob, wa, wb, post_g, *, tm=512):
    t, d = x.shape
    row = lambda i: (i, 0)
    const = lambda i: (0, 0)
    return pl.pallas_call(
        _outproj_kernel,
        grid=(t // tm,),
        in_specs=[pl.BlockSpec((tm, d), row), pl.BlockSpec((tm, oa.shape[1]), row),
                  pl.BlockSpec((tm, ob.shape[1]), row),
                  pl.BlockSpec(wa.shape, const, pipeline_mode=pl.Buffered(1)),
                  pl.BlockSpec(wb.shape, const, pipeline_mode=pl.Buffered(1)), pl.BlockSpec((1, d), const)],
        out_specs=pl.BlockSpec((tm, d), row),
        out_shape=jax.ShapeDtypeStruct((t, d), F32),
        compiler_params=_cparams("parallel"),
        name="mixer_outproj",
    )(x, oa, ob, wa, wb, post_g)


def _pool_kernel(x_ref, prev_ref, next_ref, pre_g_ref, w_ref, scale_ref, post_g_ref, o_ref, ext_ref, m_ref,
                 *, seq):
    tm, d = x_ref.shape
    group = d // len(POOL_WINDOWS)
    nblk = seq // tm
    blk = pl.program_id(0) % nblk
    g = pre_g_ref[...]
    x = x_ref[...]
    ext_ref[0:POOL_HALO, :] = jnp.where(blk > 0, _rms(prev_ref[...], g), 0.0)
    ext_ref[POOL_HALO:POOL_HALO + tm, :] = _rms(x, g)
    ext_ref[POOL_HALO + tm:, :] = jnp.where(blk < nblk - 1, _rms(next_ref[...], g), 0.0)

    t = blk * tm + lax.broadcasted_iota(jnp.int32, (tm, 1), 0)
    n_ext = tm + 2 * POOL_HALO
    for gi, w in enumerate(POOL_WINDOWS):
        cols = slice(gi * group, (gi + 1) * group)
        half = w // 2
        run = ext_ref[:, cols]
        span = 1
        while span < w:
            run = run + pltpu.roll(run, span, 0)
            span *= 2
        lead = half - 1
        if lead:
            run = pltpu.roll(run, n_ext - lead, 0)
        tot = run[POOL_HALO:POOL_HALO + tm]
        cnt = (jnp.minimum(t + half, seq) - jnp.maximum(t - half, 0)).astype(F32)
        pooled = tot * (1.0 / cnt) - ext_ref[POOL_HALO:POOL_HALO + tm, cols]
        m_ref[:, cols] = jnp.dot(pooled.astype(BF16), w_ref[gi], preferred_element_type=F32)
    o_ref[...] = x + _rms(m_ref[...] * scale_ref[...], post_g_ref[...])


def _pool_mixer(x, seq, pre_g, pool_w, pool_scale, post_g, *, tm=512):
    t, d = x.shape
    hb = tm // POOL_HALO
    last = t // POOL_HALO - 1
    row = lambda i: (i, 0)
    const = lambda i: (0, 0)
    return pl.pallas_call(
        functools.partial(_pool_kernel, seq=seq),
        grid=(t // tm,),
        in_specs=[
            pl.BlockSpec((tm, d), row),
            pl.BlockSpec((POOL_HALO, d), lambda i: (jnp.maximum(i * hb - 1, 0), 0)),
            pl.BlockSpec((POOL_HALO, d), lambda i: (jnp.minimum((i + 1) * hb, last), 0)),
            pl.BlockSpec((1, d), const),
            pl.BlockSpec(pool_w.shape, lambda i: (0, 0, 0)),
            pl.BlockSpec((1, d), const),
            pl.BlockSpec((1, d), const),
        ],
        out_specs=pl.BlockSpec((tm, d), row),
        out_shape=jax.ShapeDtypeStruct((t, d), F32),
        scratch_shapes=[pltpu.VMEM((tm + 2 * POOL_HALO, d), F32), pltpu.VMEM((tm, d), F32)],
        compiler_params=_cparams("parallel"),
        name="pool_mixer",
    )(x, x, x, pre_g, pool_w, pool_scale, post_g)


def _rope_tables(seq):
    rows = seq // GRID_W
    row = jnp.repeat(jnp.arange(rows, dtype=F32), GRID_W)
    col = jnp.tile(jnp.arange(GRID_W, dtype=F32), rows)

    def angles(rot_dim):
        half = rot_dim // 2
        freqs = ROPE_BASE ** (-jnp.arange(0, half, 2, dtype=F32) / half)
        return jnp.concatenate([row[:, None] * freqs, col[:, None] * freqs], axis=-1)

    ang_a = angles(A_HEAD_DIM)
    cos_a = jnp.concatenate([jnp.cos(ang_a)] * 2, axis=-1)
    sin_a = jnp.concatenate([-jnp.sin(ang_a), jnp.sin(ang_a)], axis=-1)
    ang_b = angles(B_ROPE_DIM)
    z = jnp.zeros_like(ang_b)
    cos_b = jnp.concatenate([jnp.cos(ang_b), z, jnp.cos(ang_b), z], axis=-1)
    sin_b = jnp.concatenate([-jnp.sin(ang_b), z, jnp.sin(ang_b), z], axis=-1)
    return cos_a, sin_a, cos_b, sin_b


def _pair_split_perm(n):
    return np.concatenate([np.arange(0, n, 2), np.arange(1, n, 2)])


def _spread_rope_cols(w):
    half = B_ROPE_DIM // 2
    z = jnp.zeros((w.shape[0], half), w.dtype)
    return jnp.concatenate([w[:, 0::2], z, w[:, 1::2], z], axis=-1)


def _prep_mixer_weights(w_in, gq, gk, w_uq, w_ukv, w_out):
    d = w_in.shape[0]
    perm = _pair_split_perm(A_HEAD_DIM)
    nq, nkv = A_HEADS * A_HEAD_DIM, A_KV_HEADS * A_HEAD_DIM
    wq = w_in[:, :nq].reshape(d, A_HEADS, A_HEAD_DIM)[:, :, perm].reshape(d, nq)
    wk = w_in[:, nq:nq + nkv].reshape(d, A_KV_HEADS, A_HEAD_DIM)[:, :, perm].reshape(d, nkv)
    rest = w_in[:, nq + nkv:-B_ROPE_DIM]
    w_in_p = jnp.concatenate([wq, wk, rest, _spread_rope_cols(w_in[:, -B_ROPE_DIM:])], axis=-1).astype(BF16)

    uq = w_uq.reshape(B_Q_LORA, B_HEADS, B_NOPE_DIM + B_ROPE_DIM)
    uq_rope = _spread_rope_cols(uq[:, :, B_NOPE_DIM:].reshape(B_Q_LORA * B_HEADS, B_ROPE_DIM))
    uq_p = jnp.concatenate([uq[:, :, :B_NOPE_DIM], uq_rope.reshape(B_Q_LORA, B_HEADS, LANES)], axis=-1)
    uq_p = uq_p.reshape(B_Q_LORA, B_HEADS * B_QK_PAD).astype(BF16)

    ukv = w_ukv.reshape(B_KV_LORA, B_HEADS, B_NOPE_DIM + B_V_DIM)
    ukv_p = jnp.concatenate([ukv[:, :, :B_NOPE_DIM].reshape(B_KV_LORA, -1),
                             ukv[:, :, B_NOPE_DIM:].reshape(B_KV_LORA, -1)], axis=-1).astype(BF16)
    na = A_HEADS * A_HEAD_DIM
    return (w_in_p, gq[perm][None, :], gk[perm][None, :], uq_p, ukv_p,
            w_out[:na].astype(BF16), w_out[na:].astype(BF16))


def _trunk(x, ffn_params, mix0, mix1, tabs):
    b, seq, d = x.shape
    x = x.reshape(b * seq, d)
    (pre0, w_in, gq, gk, gcq, w_uq, gckv, w_ukv, wa, wb, post0) = mix0
    (pre1, pool_w, pool_scale, post1) = mix1

    x = _ffn_half(x, *ffn_params[0])
    qa, ka, va, qb, kb, vb = _mixer_inproj(x, seq, pre0, w_in, gq, gk, gcq, w_uq, gckv, w_ukv, tabs)
    shp = lambda a: a.reshape(b, seq, a.shape[-1])
    oa = _attention(shp(qa), shp(ka), shp(va), kv_heads=A_KV_HEADS, group=A_HEADS // A_KV_HEADS,
                    dk=A_HEAD_DIM, dv=A_HEAD_DIM, tq=512)
    ob = _attention(shp(qb), shp(kb), shp(vb), kv_heads=B_HEADS, group=1, dk=B_QK_PAD, dv=B_V_DIM, tq=2048)
    x = _mixer_outproj(x, oa.reshape(b * seq, -1), ob.reshape(b * seq, -1), wa, wb, post0)
    x = _ffn_half(x, *ffn_params[1])
    x = _ffn_half(x, *ffn_params[2])
    x = _pool_mixer(x, seq, pre1, pool_w, pool_scale, post1)
    x = _ffn_half(x, *ffn_params[3])
    return x.reshape(b, seq, d)


def kernel(x_prompt, x_sample, l0_ffn1_pre_g, l0_ffn1_w_gate, l0_ffn1_w_up, l0_ffn1_w_down, l0_ffn1_post_g, l0_mix_pre_g, l0_w_in, l0_a_q_norm_g, l0_a_k_norm_g, l0_b_cq_norm_g, l0_b_w_uq, l0_b_ckv_norm_g, l0_b_w_ukv, l0_w_out, l0_mix_post_g, l0_ffn2_pre_g, l0_ffn2_w_gate, l0_ffn2_w_up, l0_ffn2_w_down, l0_ffn2_post_g, l1_ffn1_pre_g, l1_ffn1_w_gate, l1_ffn1_w_up, l1_ffn1_w_down, l1_ffn1_post_g, l1_mix_pre_g, l1_pool_w, l1_pool_scale, l1_mix_post_g, l1_ffn2_pre_g, l1_ffn2_w_gate, l1_ffn2_w_up, l1_ffn2_w_down, l1_ffn2_post_g):
    vec = lambda g: g[None, :]
    ffn = lambda pre, wg, wu, wd, post: (vec(pre), wg.astype(BF16), wu.astype(BF16), wd.astype(BF16), vec(post))
    ffn_params = (
        ffn(l0_ffn1_pre_g, l0_ffn1_w_gate, l0_ffn1_w_up, l0_ffn1_w_down, l0_ffn1_post_g),
        ffn(l0_ffn2_pre_g, l0_ffn2_w_gate, l0_ffn2_w_up, l0_ffn2_w_down, l0_ffn2_post_g),
        ffn(l1_ffn1_pre_g, l1_ffn1_w_gate, l1_ffn1_w_up, l1_ffn1_w_down, l1_ffn1_post_g),
        ffn(l1_ffn2_pre_g, l1_ffn2_w_gate, l1_ffn2_w_up, l1_ffn2_w_down, l1_ffn2_post_g),
    )
    w_in, gq, gk, w_uq, w_ukv, wa, wb = _prep_mixer_weights(
        l0_w_in, l0_a_q_norm_g, l0_a_k_norm_g, l0_b_w_uq, l0_b_w_ukv, l0_w_out)
    mix0 = (vec(l0_mix_pre_g), w_in, gq, gk, vec(l0_b_cq_norm_g), w_uq, vec(l0_b_ckv_norm_g), w_ukv,
            wa, wb, vec(l0_mix_post_g))
    mix1 = (vec(l1_mix_pre_g), l1_pool_w.astype(BF16), vec(l1_pool_scale), vec(l1_mix_post_g))

    outs = []
    for x in (x_prompt, x_sample):
        tabs = _rope_tables(x.shape[1])
        outs.append(_trunk(x, ffn_params, mix0, mix1, tabs))
    return tuple(outs)
```

```python
import functools
import math

import jax
import jax.numpy as jnp
import numpy as np
from jax import lax
from jax.experimental import pallas as pl
from jax.experimental.pallas import tpu as pltpu

F32 = jnp.float32
BF16 = jnp.bfloat16

NORM_EPS = 1e-6
ROPE_BASE = 10000.0
GRID_W = 64
A_HEADS = 8
A_KV_HEADS = 2
A_HEAD_DIM = 128
B_HEADS = 8
B_Q_LORA = 512
B_KV_LORA = 256
B_NOPE_DIM = 128
B_ROPE_DIM = 64
B_V_DIM = 128
POOL_WINDOWS = (2, 4, 8, 16)
LANES = 128
LOG2_E = math.log2(math.e)
ATTN_CHUNK_UNROLL = 16
B_QK_PAD = 2 * LANES
ROW_CHUNK = 32
PROJ_SUB_ROWS = 512
FFN_SUB_ROWS = 512
POOL_HALO = 8

VMEM_LIMIT_BYTES = 60 * 1024 * 1024


def _cparams(*semantics):
    return pltpu.CompilerParams(dimension_semantics=semantics, vmem_limit_bytes=VMEM_LIMIT_BYTES)


def _rms(x, g):
    return x * lax.rsqrt(jnp.mean(x * x, axis=-1, keepdims=True) + NORM_EPS) * g


def _silu(x):
    return x * (1.0 / (1.0 + jnp.exp(-x)))


def _ffn_step(x_ref, pre_g_ref, wg_ref, wu_ref, wd_ref, post_g_ref, o_ref, h_ref, *, first, last):
    tm = x_ref.shape[0]
    sub = FFN_SUB_ROWS if (first or last) else tm
    for r in range(tm // sub):
        base = r * sub
        rows = slice(base, base + sub)
        if first:
            for c in range(sub // ROW_CHUNK):
                chunk = slice(base + c * ROW_CHUNK, base + (c + 1) * ROW_CHUNK)
                h_ref[chunk, :] = _rms(x_ref[chunk, :], pre_g_ref[...]).astype(BF16)
        h = h_ref[rows, :]
        gate = jnp.dot(h, wg_ref[...], preferred_element_type=F32)
        up = jnp.dot(h, wu_ref[...], preferred_element_type=F32)
        act = (_silu(gate) * up).astype(BF16)
        down = jnp.dot(act, wd_ref[...], preferred_element_type=F32)
        if first:
            o_ref[rows, :] = down
        else:
            o_ref[rows, :] += down
        if last:
            for c in range(sub // ROW_CHUNK):
                chunk = slice(base + c * ROW_CHUNK, base + (c + 1) * ROW_CHUNK)
                o_ref[chunk, :] = x_ref[chunk, :] + 0.5 * _rms(o_ref[chunk, :], post_g_ref[...])


def _ffn_kernel(*refs):
    f = pl.program_id(1)
    nf = pl.num_programs(1)
    pl.when(f == 0)(functools.partial(_ffn_step, *refs, first=True, last=False))
    pl.when(jnp.logical_and(f > 0, f < nf - 1))(functools.partial(_ffn_step, *refs, first=False, last=False))
    pl.when(f == nf - 1)(functools.partial(_ffn_step, *refs, first=False, last=True))


def _ffn_half(x, pre_g, wg, wu, wd, post_g, *, tm=1024, tf=512):
    t, d = x.shape
    dff = wg.shape[1]
    assert t % tm == 0 and dff % tf == 0 and dff // tf >= 2 and tm % FFN_SUB_ROWS == 0
    row = lambda i, f: (i, 0)
    const = lambda i, f: (0, 0)
    return pl.pallas_call(
        _ffn_kernel,
        grid=(t // tm, dff // tf),
        in_specs=[
            pl.BlockSpec((tm, d), row),
            pl.BlockSpec((1, d), const),
            pl.BlockSpec((d, tf), lambda i, f: (0, f)),
            pl.BlockSpec((d, tf), lambda i, f: (0, f)),
            pl.BlockSpec((tf, d), lambda i, f: (f, 0)),
            pl.BlockSpec((1, d), const),
        ],
        out_specs=pl.BlockSpec((tm, d), row),
        out_shape=jax.ShapeDtypeStruct((t, d), F32),
        scratch_shapes=[pltpu.VMEM((tm, d), BF16)],
        compiler_params=_cparams("parallel", "arbitrary"),
        name="ffn_half",
    )(x, pre_g, wg, wu, wd, post_g)


def _rope(x, cos, sin):
    return x * cos + pltpu.roll(x, LANES // 2, 1) * sin


def _inproj_kernel(x_ref, pre_g_ref, w_in_ref, gq_ref, gk_ref, gcq_ref, w_uq_ref, gckv_ref, w_ukv_ref,
                   cos_a_ref, sin_a_ref, cos_b_ref, sin_b_ref, *out_refs):
    for r in range(x_ref.shape[0] // PROJ_SUB_ROWS):
        rows = pl.ds(r * PROJ_SUB_ROWS, PROJ_SUB_ROWS)
        at = lambda ref: ref.at[rows]
        _inproj_rows(at(x_ref), pre_g_ref, w_in_ref, gq_ref, gk_ref, gcq_ref, w_uq_ref, gckv_ref, w_ukv_ref,
                     at(cos_a_ref), at(sin_a_ref), at(cos_b_ref), at(sin_b_ref), *map(at, out_refs))


def _inproj_rows(x_ref, pre_g_ref, w_in_ref, gq_ref, gk_ref, gcq_ref, w_uq_ref, gckv_ref, w_ukv_ref,
                 cos_a_ref, sin_a_ref, cos_b_ref, sin_b_ref,
                 qa_ref, ka_ref, va_ref, qb_ref, kb_ref, vb_ref):
    hn = _rms(x_ref[...], pre_g_ref[...]).astype(BF16)
    y = jnp.dot(hn, w_in_ref[...], preferred_element_type=F32)
    cos_a, sin_a = cos_a_ref[...], sin_a_ref[...]
    cos_b, sin_b = cos_b_ref[...], sin_b_ref[...]
    a_scale = A_HEAD_DIM ** -0.5 * LOG2_E
    b_scale = (B_NOPE_DIM + B_ROPE_DIM) ** -0.5 * LOG2_E
    ones = jnp.ones((x_ref.shape[0], LANES), BF16)
    q_cols = A_HEADS * A_HEAD_DIM
    kv_cols = A_KV_HEADS * A_HEAD_DIM

    for h in range(A_HEADS):
        sl = slice(h * LANES, (h + 1) * LANES)
        qa_ref[:, sl] = (_rope(_rms(y[:, sl], gq_ref[...]), cos_a, sin_a) * a_scale).astype(BF16)
    for h in range(A_KV_HEADS):
        sl = slice(h * LANES, (h + 1) * LANES)
        ka_ref[:, sl] = _rope(_rms(y[:, q_cols + h * LANES:q_cols + (h + 1) * LANES], gk_ref[...]),
                              cos_a, sin_a).astype(BF16)
    off = q_cols + kv_cols
    for h in range(A_KV_HEADS):
        va_ref[:, 2 * h * LANES:(2 * h + 1) * LANES] = y[:, off + h * LANES:off + (h + 1) * LANES].astype(BF16)
        va_ref[:, (2 * h + 1) * LANES:(2 * h + 2) * LANES] = ones
    off += kv_cols

    cq = _rms(y[:, off:off + B_Q_LORA], gcq_ref[...]).astype(BF16)
    off += B_Q_LORA
    ckv = _rms(y[:, off:off + B_KV_LORA], gckv_ref[...]).astype(BF16)
    off += B_KV_LORA
    k_rope = _rope(y[:, off:off + LANES], cos_b, sin_b).astype(BF16)

    qf = jnp.dot(cq, w_uq_ref[...], preferred_element_type=F32)
    kv = jnp.dot(ckv, w_ukv_ref[...], preferred_element_type=F32)
    for h in range(B_HEADS):
        base = h * B_QK_PAD
        qb_ref[:, base:base + LANES] = (qf[:, base:base + LANES] * b_scale).astype(BF16)
        qb_ref[:, base + LANES:base + B_QK_PAD] = (
            _rope(qf[:, base + LANES:base + B_QK_PAD], cos_b, sin_b) * b_scale).astype(BF16)
        kb_ref[:, base:base + LANES] = kv[:, h * LANES:(h + 1) * LANES].astype(BF16)
        kb_ref[:, base + LANES:base + B_QK_PAD] = k_rope
        v_col = B_HEADS * B_NOPE_DIM + h * LANES
        vb_ref[:, 2 * h * LANES:(2 * h + 1) * LANES] = kv[:, v_col:v_col + LANES].astype(BF16)
        vb_ref[:, (2 * h + 1) * LANES:(2 * h + 2) * LANES] = ones


def _mixer_inproj(x, seq, pre_g, w_in, gq, gk, gcq, w_uq, gckv, w_ukv, tabs, *, tm=512):
    t, d = x.shape
    assert seq % tm == 0 and tm % PROJ_SUB_ROWS == 0
    nblk = seq // tm
    row = lambda i: (i, 0)
    const = lambda i: (0, 0)
    pos = lambda i: (i % nblk, 0)
    full = lambda a: pl.BlockSpec(a.shape, const, pipeline_mode=pl.Buffered(1))
    widths = (A_HEADS * A_HEAD_DIM, A_KV_HEADS * A_HEAD_DIM, 2 * A_KV_HEADS * A_HEAD_DIM,
              B_HEADS * B_QK_PAD, B_HEADS * B_QK_PAD, 2 * B_HEADS * B_V_DIM)
    return pl.pallas_call(
        _inproj_kernel,
        grid=(t // tm,),
        in_specs=[pl.BlockSpec((tm, d), row), full(pre_g), full(w_in), full(gq), full(gk), full(gcq),
                  full(w_uq), full(gckv), full(w_ukv)] + [pl.BlockSpec((tm, LANES), pos)] * 4,
        out_specs=[pl.BlockSpec((tm, w), row) for w in widths],
        out_shape=[jax.ShapeDtypeStruct((t, w), BF16) for w in widths],
        compiler_params=_cparams("parallel"),
        name="mixer_inproj",
    )(x, pre_g, w_in, gq, gk, gcq, w_uq, gckv, w_ukv, *tabs)


def _attn_kernel(q_ref, k_ref, v_ref, o_ref, *, group, dk, dv, tk, rb):
    tq = q_ref.shape[0]
    seq = k_ref.shape[0]
    per_head = tq // rb
    nblocks = group * per_head

    def q_block(i):
        j, r = divmod(i, per_head)
        return q_ref[r * rb:(r + 1) * rb, j * dk:(j + 1) * dk]

    def body(c, carry):
        ms, accs = carry
        start = pl.multiple_of(c * tk, tk)
        k = k_ref[pl.ds(start, tk), :]
        v = v_ref[pl.ds(start, tk), :]
        new_ms, new_accs = [], []
        for i in range(nblocks):
            s = lax.dot_general(q_block(i), k, (((1,), (1,)), ((), ())), preferred_element_type=F32)
            m_new = jnp.maximum(ms[i], jnp.max(s, axis=-1, keepdims=True))
            alpha = jnp.exp2(ms[i] - m_new)
            p = jnp.exp2(s - m_new).astype(BF16)
            new_accs.append(alpha * accs[i] + jnp.dot(p, v, preferred_element_type=F32))
            new_ms.append(m_new)
        return tuple(new_ms), tuple(new_accs)

    init = (tuple(jnp.full((rb, 1), -jnp.inf, F32) for _ in range(nblocks)),
            tuple(jnp.zeros((rb, 2 * dv), F32) for _ in range(nblocks)))
    _, accs = lax.fori_loop(0, seq // tk, body, init, unroll=ATTN_CHUNK_UNROLL)
    for i in range(nblocks):
        j, r = divmod(i, per_head)
        out = accs[i][:, :dv] * (1.0 / accs[i][:, dv:])
        o_ref[r * rb:(r + 1) * rb, j * dv:(j + 1) * dv] = out.astype(o_ref.dtype)


def _attention(q, k, v, *, kv_heads, group, dk, dv, tq, tk=512, rb=512):
    b, seq, _ = q.shape
    tq, tk = min(tq, seq), min(tk, seq)
    rb = min(rb, tq)
    assert seq % tq == 0 and seq % tk == 0 and tq % rb == 0
    kern = functools.partial(_attn_kernel, group=group, dk=dk, dv=dv, tk=tk, rb=rb)
    return pl.pallas_call(
        kern,
        grid=(b, kv_heads, seq // tq),
        in_specs=[
            pl.BlockSpec((None, tq, group * dk), lambda bi, h, i: (bi, i, h)),
            pl.BlockSpec((None, seq, dk), lambda bi, h, i: (bi, 0, h)),
            pl.BlockSpec((None, seq, 2 * dv), lambda bi, h, i: (bi, 0, h)),
        ],
        out_specs=pl.BlockSpec((None, tq, group * dv), lambda bi, h, i: (bi, i, h)),
        out_shape=jax.ShapeDtypeStruct((b, seq, kv_heads * group * dv), BF16),
        compiler_params=_cparams("parallel", "parallel", "arbitrary"),
        name=f"attention_g{group}_dk{dk}",
    )(q, k, v)


def _outproj_kernel(x_ref, oa_ref, ob_ref, wa_ref, wb_ref, post_g_ref, o_ref):
    for r in range(x_ref.shape[0] // PROJ_SUB_ROWS):
        rows = slice(r * PROJ_SUB_ROWS, (r + 1) * PROJ_SUB_ROWS)
        m = jnp.dot(oa_ref[rows, :], wa_ref[...], preferred_element_type=F32)
        m += jnp.dot(ob_ref[rows, :], wb_ref[...], preferred_element_type=F32)
        o_ref[rows, :] = x_ref[rows, :] + _rms(m, post_g_ref[...])


def _mixer_outproj(x, oa, ob, wa, wb, post_g, *, tm=512):
    t, d = x.shape
    row = lambda i: (i, 0)
    const = lambda i: (0, 0)
    return pl.pallas_call(
        _outproj_kernel,
        grid=(t // tm,),
        in_specs=[pl.BlockSpec((tm, d), row), pl.BlockSpec((tm, oa.shape[1]), row),
                  pl.BlockSpec((tm, ob.shape[1]), row),
                  pl.BlockSpec(wa.shape, const, pipeline_mode=pl.Buffered(1)),
                  pl.BlockSpec(wb.shape, const, pipeline_mode=pl.Buffered(1)), pl.BlockSpec((1, d), const)],
        out_specs=pl.BlockSpec((tm, d), row),
        out_shape=jax.ShapeDtypeStruct((t, d), F32),
        compiler_params=_cparams("parallel"),
        name="mixer_outproj",
    )(x, oa, ob, wa, wb, post_g)


def _pool_kernel(x_ref, prev_ref, next_ref, pre_g_ref, w_ref, scale_ref, post_g_ref, o_ref, ext_ref, m_ref,
                 *, seq):
    tm, d = x_ref.shape
    group = d // len(POOL_WINDOWS)
    nblk = seq // tm
    blk = pl.program_id(0) % nblk
    g = pre_g_ref[...]
    x = x_ref[...]
    ext_ref[0:POOL_HALO, :] = jnp.where(blk > 0, _rms(prev_ref[...], g), 0.0)
    ext_ref[POOL_HALO:POOL_HALO + tm, :] = _rms(x, g)
    ext_ref[POOL_HALO + tm:, :] = jnp.where(blk < nblk - 1, _rms(next_ref[...], g), 0.0)

    t = blk * tm + lax.broadcasted_iota(jnp.int32, (tm, 1), 0)
    n_ext = tm + 2 * POOL_HALO
    for gi, w in enumerate(POOL_WINDOWS):
        cols = slice(gi * group, (gi + 1) * group)
        half = w // 2
        run = ext_ref[:, cols]
        span = 1
        while span < w:
            run = run + pltpu.roll(run, span, 0)
            span *= 2
        lead = half - 1
        if lead:
            run = pltpu.roll(run, n_ext - lead, 0)
        tot = run[POOL_HALO:POOL_HALO + tm]
        cnt = (jnp.minimum(t + half, seq) - jnp.maximum(t - half, 0)).astype(F32)
        pooled = tot * (1.0 / cnt) - ext_ref[POOL_HALO:POOL_HALO + tm, cols]
        m_ref[:, cols] = jnp.dot(pooled.astype(BF16), w_ref[gi], preferred_element_type=F32)
    o_ref[...] = x + _rms(m_ref[...] * scale_ref[...], post_g_ref[...])


def _pool_mixer(x, seq, pre_g, pool_w, pool_scale, post_g, *, tm=512):
    t, d = x.shape
    assert seq % tm == 0 and max(POOL_WINDOWS) // 2 <= POOL_HALO
    hb = tm // POOL_HALO
    last = t // POOL_HALO - 1
    row = lambda i: (i, 0)
    const = lambda i: (0, 0)
    return pl.pallas_call(
        functools.partial(_pool_kernel, seq=seq),
        grid=(t // tm,),
        in_specs=[
            pl.BlockSpec((tm, d), row),
            pl.BlockSpec((POOL_HALO, d), lambda i: (jnp.maximum(i * hb - 1, 0), 0)),
            pl.BlockSpec((POOL_HALO, d), lambda i: (jnp.minimum((i + 1) * hb, last), 0)),
            pl.BlockSpec((1, d), const),
            pl.BlockSpec(pool_w.shape, lambda i: (0, 0, 0)),
            pl.BlockSpec((1, d), const),
            pl.BlockSpec((1, d), const),
        ],
        out_specs=pl.BlockSpec((tm, d), row),
        out_shape=jax.ShapeDtypeStruct((t, d), F32),
        scratch_shapes=[pltpu.VMEM((tm + 2 * POOL_HALO, d), F32), pltpu.VMEM((tm, d), F32)],
        compiler_params=_cparams("parallel"),
        name="pool_mixer",
    )(x, x, x, pre_g, pool_w, pool_scale, post_g)


def _rope_tables(seq):
    rows = seq // GRID_W
    row = jnp.repeat(jnp.arange(rows, dtype=F32), GRID_W)
    col = jnp.tile(jnp.arange(GRID_W, dtype=F32), rows)

    def angles(rot_dim):
        half = rot_dim // 2
        freqs = ROPE_BASE ** (-jnp.arange(0, half, 2, dtype=F32) / half)
        return jnp.concatenate([row[:, None] * freqs, col[:, None] * freqs], axis=-1)

    ang_a = angles(A_HEAD_DIM)
    cos_a = jnp.concatenate([jnp.cos(ang_a)] * 2, axis=-1)
    sin_a = jnp.concatenate([-jnp.sin(ang_a), jnp.sin(ang_a)], axis=-1)
    ang_b = angles(B_ROPE_DIM)
    z = jnp.zeros_like(ang_b)
    cos_b = jnp.concatenate([jnp.cos(ang_b), z, jnp.cos(ang_b), z], axis=-1)
    sin_b = jnp.concatenate([-jnp.sin(ang_b), z, jnp.sin(ang_b), z], axis=-1)
    return cos_a, sin_a, cos_b, sin_b


def _pair_split_perm(n):
    return np.concatenate([np.arange(0, n, 2), np.arange(1, n, 2)])


def _spread_rope_cols(w):
    half = B_ROPE_DIM // 2
    z = jnp.zeros((w.shape[0], half), w.dtype)
    return jnp.concatenate([w[:, 0::2], z, w[:, 1::2], z], axis=-1)


def _prep_mixer_weights(w_in, gq, gk, w_uq, w_ukv, w_out):
    d = w_in.shape[0]
    perm = _pair_split_perm(A_HEAD_DIM)
    nq, nkv = A_HEADS * A_HEAD_DIM, A_KV_HEADS * A_HEAD_DIM
    wq = w_in[:, :nq].reshape(d, A_HEADS, A_HEAD_DIM)[:, :, perm].reshape(d, nq)
    wk = w_in[:, nq:nq + nkv].reshape(d, A_KV_HEADS, A_HEAD_DIM)[:, :, perm].reshape(d, nkv)
    rest = w_in[:, nq + nkv:-B_ROPE_DIM]
    w_in_p = jnp.concatenate([wq, wk, rest, _spread_rope_cols(w_in[:, -B_ROPE_DIM:])], axis=-1).astype(BF16)

    uq = w_uq.reshape(B_Q_LORA, B_HEADS, B_NOPE_DIM + B_ROPE_DIM)
    uq_rope = _spread_rope_cols(uq[:, :, B_NOPE_DIM:].reshape(B_Q_LORA * B_HEADS, B_ROPE_DIM))
    uq_p = jnp.concatenate([uq[:, :, :B_NOPE_DIM], uq_rope.reshape(B_Q_LORA, B_HEADS, LANES)], axis=-1)
    uq_p = uq_p.reshape(B_Q_LORA, B_HEADS * B_QK_PAD).astype(BF16)

    ukv = w_ukv.reshape(B_KV_LORA, B_HEADS, B_NOPE_DIM + B_V_DIM)
    ukv_p = jnp.concatenate([ukv[:, :, :B_NOPE_DIM].reshape(B_KV_LORA, -1),
                             ukv[:, :, B_NOPE_DIM:].reshape(B_KV_LORA, -1)], axis=-1).astype(BF16)
    na = A_HEADS * A_HEAD_DIM
    return (w_in_p, gq[perm][None, :], gk[perm][None, :], uq_p, ukv_p,
            w_out[:na].astype(BF16), w_out[na:].astype(BF16))


def _trunk(x, ffn_params, mix0, mix1, tabs):
    b, seq, d = x.shape
    x = x.reshape(b * seq, d)
    (pre0, w_in, gq, gk, gcq, w_uq, gckv, w_ukv, wa, wb, post0) = mix0
    (pre1, pool_w, pool_scale, post1) = mix1

    x = _ffn_half(x, *ffn_params[0])
    qa, ka, va, qb, kb, vb = _mixer_inproj(x, seq, pre0, w_in, gq, gk, gcq, w_uq, gckv, w_ukv, tabs)
    shp = lambda a: a.reshape(b, seq, a.shape[-1])
    oa = _attention(shp(qa), shp(ka), shp(va), kv_heads=A_KV_HEADS, group=A_HEADS // A_KV_HEADS,
                    dk=A_HEAD_DIM, dv=A_HEAD_DIM, tq=1024)
    ob = _attention(shp(qb), shp(kb), shp(vb), kv_heads=B_HEADS, group=1, dk=B_QK_PAD, dv=B_V_DIM, tq=4096)
    x = _mixer_outproj(x, oa.reshape(b * seq, -1), ob.reshape(b * seq, -1), wa, wb, post0)
    x = _ffn_half(x, *ffn_params[1])
    x = _ffn_half(x, *ffn_params[2])
    x = _pool_mixer(x, seq, pre1, pool_w, pool_scale, post1)
    x = _ffn_half(x, *ffn_params[3])
    return x.reshape(b, seq, d)


def kernel(x_prompt, x_sample, l0_ffn1_pre_g, l0_ffn1_w_gate, l0_ffn1_w_up, l0_ffn1_w_down, l0_ffn1_post_g, l0_mix_pre_g, l0_w_in, l0_a_q_norm_g, l0_a_k_norm_g, l0_b_cq_norm_g, l0_b_w_uq, l0_b_ckv_norm_g, l0_b_w_ukv, l0_w_out, l0_mix_post_g, l0_ffn2_pre_g, l0_ffn2_w_gate, l0_ffn2_w_up, l0_ffn2_w_down, l0_ffn2_post_g, l1_ffn1_pre_g, l1_ffn1_w_gate, l1_ffn1_w_up, l1_ffn1_w_down, l1_ffn1_post_g, l1_mix_pre_g, l1_pool_w, l1_pool_scale, l1_mix_post_g, l1_ffn2_pre_g, l1_ffn2_w_gate, l1_ffn2_w_up, l1_ffn2_w_down, l1_ffn2_post_g):
    vec = lambda g: g[None, :]
    ffn = lambda pre, wg, wu, wd, post: (vec(pre), wg.astype(BF16), wu.astype(BF16), wd.astype(BF16), vec(post))
    ffn_params = (
        ffn(l0_ffn1_pre_g, l0_ffn1_w_gate, l0_ffn1_w_up, l0_ffn1_w_down, l0_ffn1_post_g),
        ffn(l0_ffn2_pre_g, l0_ffn2_w_gate, l0_ffn2_w_up, l0_ffn2_w_down, l0_ffn2_post_g),
        ffn(l1_ffn1_pre_g, l1_ffn1_w_gate, l1_ffn1_w_up, l1_ffn1_w_down, l1_ffn1_post_g),
        ffn(l1_ffn2_pre_g, l1_ffn2_w_gate, l1_ffn2_w_up, l1_ffn2_w_down, l1_ffn2_post_g),
    )
    w_in, gq, gk, w_uq, w_ukv, wa, wb = _prep_mixer_weights(
        l0_w_in, l0_a_q_norm_g, l0_a_k_norm_g, l0_b_w_uq, l0_b_w_ukv, l0_w_out)
    mix0 = (vec(l0_mix_pre_g), w_in, gq, gk, vec(l0_b_cq_norm_g), w_uq, vec(l0_b_ckv_norm_g), w_ukv,
            wa, wb, vec(l0_mix_post_g))
    mix1 = (vec(l1_mix_pre_g), l1_pool_w.astype(BF16), vec(l1_pool_scale), vec(l1_mix_post_g))

    outs = []
    for x in (x_prompt, x_sample):
        tabs = _rope_tables(x.shape[1])
        outs.append(_trunk(x, ffn_params, mix0, mix1, tabs))
    return tuple(outs)
```

```python
import functools
import math

import jax
import jax.numpy as jnp
import numpy as np
from jax import lax
from jax.experimental import pallas as pl
from jax.experimental.pallas import tpu as pltpu

F32 = jnp.float32
BF16 = jnp.bfloat16

NORM_EPS = 1e-6
ROPE_BASE = 10000.0
GRID_W = 64
A_HEADS = 8
A_KV_HEADS = 2
A_HEAD_DIM = 128
B_HEADS = 8
B_Q_LORA = 512
B_KV_LORA = 256
B_NOPE_DIM = 128
B_ROPE_DIM = 64
B_V_DIM = 128
POOL_WINDOWS = (2, 4, 8, 16)
LANES = 128
LOG2_E = math.log2(math.e)
ATTN_CHUNK_UNROLL = 16
B_QK_PAD = 2 * LANES
ROW_CHUNK = 32
PROJ_SUB_ROWS = 512
FFN_SUB_ROWS = 512
POOL_HALO = 8

VMEM_LIMIT_BYTES = 60 * 1024 * 1024


def _cparams(*semantics):
    return pltpu.CompilerParams(dimension_semantics=semantics, vmem_limit_bytes=VMEM_LIMIT_BYTES)


def _rms(x, g):
    return x * lax.rsqrt(jnp.mean(x * x, axis=-1, keepdims=True) + NORM_EPS) * g


def _silu(x):
    return x * (1.0 / (1.0 + jnp.exp(-x)))


def _ffn_step(x_ref, pre_g_ref, wg_ref, wu_ref, wd_ref, post_g_ref, o_ref, h_ref, *, first, last):
    tm = x_ref.shape[0]
    sub = FFN_SUB_ROWS if (first or last) else tm
    for r in range(tm // sub):
        base = r * sub
        rows = slice(base, base + sub)
        if first:
            for c in range(sub // ROW_CHUNK):
                chunk = slice(base + c * ROW_CHUNK, base + (c + 1) * ROW_CHUNK)
                h_ref[chunk, :] = _rms(x_ref[chunk, :], pre_g_ref[...]).astype(BF16)
        h = h_ref[rows, :]
        gate = jnp.dot(h, wg_ref[...], preferred_element_type=F32)
        up = jnp.dot(h, wu_ref[...], preferred_element_type=F32)
        act = (_silu(gate) * up).astype(BF16)
        down = jnp.dot(act, wd_ref[...], preferred_element_type=F32)
        if first:
            o_ref[rows, :] = down
        else:
            o_ref[rows, :] += down
        if last:
            for c in range(sub // ROW_CHUNK):
                chunk = slice(base + c * ROW_CHUNK, base + (c + 1) * ROW_CHUNK)
                o_ref[chunk, :] = x_ref[chunk, :] + 0.5 * _rms(o_ref[chunk, :], post_g_ref[...])


def _ffn_kernel(*refs):
    f = pl.program_id(1)
    nf = pl.num_programs(1)
    pl.when(f == 0)(functools.partial(_ffn_step, *refs, first=True, last=False))
    pl.when(jnp.logical_and(f > 0, f < nf - 1))(functools.partial(_ffn_step, *refs, first=False, last=False))
    pl.when(f == nf - 1)(functools.partial(_ffn_step, *refs, first=False, last=True))


def _ffn_half(x, pre_g, wg, wu, wd, post_g, *, tm=1024, tf=512):
    t, d = x.shape
    dff = wg.shape[1]
    assert t % tm == 0 and dff % tf == 0 and dff // tf >= 2 and tm % FFN_SUB_ROWS == 0
    row = lambda i, f: (i, 0)
    const = lambda i, f: (0, 0)
    return pl.pallas_call(
        _ffn_kernel,
        grid=(t // tm, dff // tf),
        in_specs=[
            pl.BlockSpec((tm, d), row),
            pl.BlockSpec((1, d), const),
            pl.BlockSpec((d, tf), lambda i, f: (0, f)),
            pl.BlockSpec((d, tf), lambda i, f: (0, f)),
            pl.BlockSpec((tf, d), lambda i, f: (f, 0)),
            pl.BlockSpec((1, d), const),
        ],
        out_specs=pl.BlockSpec((tm, d), row),
        out_shape=jax.ShapeDtypeStruct((t, d), F32),
        scratch_shapes=[pltpu.VMEM((tm, d), BF16)],
        compiler_params=_cparams("parallel", "arbitrary"),
        name="ffn_half",
    )(x, pre_g, wg, wu, wd, post_g)


def _rope(x, cos, sin):
    return x * cos + pltpu.roll(x, LANES // 2, 1) * sin


def _inproj_kernel(x_ref, pre_g_ref, w_in_ref, gq_ref, gk_ref, gcq_ref, w_uq_ref, gckv_ref, w_ukv_ref,
                   cos_a_ref, sin_a_ref, cos_b_ref, sin_b_ref, *out_refs):
    for r in range(x_ref.shape[0] // PROJ_SUB_ROWS):
        rows = pl.ds(r * PROJ_SUB_ROWS, PROJ_SUB_ROWS)
        at = lambda ref: ref.at[rows]
        _inproj_rows(at(x_ref), pre_g_ref, w_in_ref, gq_ref, gk_ref, gcq_ref, w_uq_ref, gckv_ref, w_ukv_ref,
                     at(cos_a_ref), at(sin_a_ref), at(cos_b_ref), at(sin_b_ref), *map(at, out_refs))


def _inproj_rows(x_ref, pre_g_ref, w_in_ref, gq_ref, gk_ref, gcq_ref, w_uq_ref, gckv_ref, w_ukv_ref,
                 cos_a_ref, sin_a_ref, cos_b_ref, sin_b_ref,
                 qa_ref, ka_ref, va_ref, qb_ref, kb_ref, vb_ref):
    hn = _rms(x_ref[...], pre_g_ref[...]).astype(BF16)
    y = jnp.dot(hn, w_in_ref[...], preferred_element_type=F32)
    cos_a, sin_a = cos_a_ref[...], sin_a_ref[...]
    cos_b, sin_b = cos_b_ref[...], sin_b_ref[...]
    a_scale = A_HEAD_DIM ** -0.5 * LOG2_E
    b_scale = (B_NOPE_DIM + B_ROPE_DIM) ** -0.5 * LOG2_E
    ones = jnp.ones((x_ref.shape[0], LANES), BF16)
    q_cols = A_HEADS * A_HEAD_DIM
    kv_cols = A_KV_HEADS * A_HEAD_DIM

    for h in range(A_HEADS):
        sl = slice(h * LANES, (h + 1) * LANES)
        qa_ref[:, sl] = (_rope(_rms(y[:, sl], gq_ref[...]), cos_a, sin_a) * a_scale).astype(BF16)
    for h in range(A_KV_HEADS):
        sl = slice(h * LANES, (h + 1) * LANES)
        ka_ref[:, sl] = _rope(_rms(y[:, q_cols + h * LANES:q_cols + (h + 1) * LANES], gk_ref[...]),
                              cos_a, sin_a).astype(BF16)
    off = q_cols + kv_cols
    for h in range(A_KV_HEADS):
        va_ref[:, 2 * h * LANES:(2 * h + 1) * LANES] = y[:, off + h * LANES:off + (h + 1) * LANES].astype(BF16)
        va_ref[:, (2 * h + 1) * LANES:(2 * h + 2) * LANES] = ones
    off += kv_cols

    cq = _rms(y[:, off:off + B_Q_LORA], gcq_ref[...]).astype(BF16)
    off += B_Q_LORA
    ckv = _rms(y[:, off:off + B_KV_LORA], gckv_ref[...]).astype(BF16)
    off += B_KV_LORA
    k_rope = _rope(y[:, off:off + LANES], cos_b, sin_b).astype(BF16)

    qf = jnp.dot(cq, w_uq_ref[...], preferred_element_type=F32)
    kv = jnp.dot(ckv, w_ukv_ref[...], preferred_element_type=F32)
    for h in range(B_HEADS):
        base = h * B_QK_PAD
        qb_ref[:, base:base + LANES] = (qf[:, base:base + LANES] * b_scale).astype(BF16)
        qb_ref[:, base + LANES:base + B_QK_PAD] = (
            _rope(qf[:, base + LANES:base + B_QK_PAD], cos_b, sin_b) * b_scale).astype(BF16)
        kb_ref[:, base:base + LANES] = kv[:, h * LANES:(h + 1) * LANES].astype(BF16)
        kb_ref[:, base + LANES:base + B_QK_PAD] = k_rope
        v_col = B_HEADS * B_NOPE_DIM + h * LANES
        vb_ref[:, 2 * h * LANES:(2 * h + 1) * LANES] = kv[:, v_col:v_col + LANES].astype(BF16)
        vb_ref[:, (2 * h + 1) * LANES:(2 * h + 2) * LANES] = ones


def _mixer_inproj(x, seq, pre_g, w_in, gq, gk, gcq, w_uq, gckv, w_ukv, tabs, *, tm=512):
    t, d = x.shape
    assert seq % tm == 0 and tm % PROJ_SUB_ROWS == 0
    nblk = seq // tm
    row = lambda i: (i, 0)
    const = lambda i: (0, 0)
    pos = lambda i: (i % nblk, 0)
    full = lambda a: pl.BlockSpec(a.shape, const, pipeline_mode=pl.Buffered(1))
    widths = (A_HEADS * A_HEAD_DIM, A_KV_HEADS * A_HEAD_DIM, 2 * A_KV_HEADS * A_HEAD_DIM,
              B_HEADS * B_QK_PAD, B_HEADS * B_QK_PAD, 2 * B_HEADS * B_V_DIM)
    return pl.pallas_call(
        _inproj_kernel,
        grid=(t // tm,),
        in_specs=[pl.BlockSpec((tm, d), row), full(pre_g), full(w_in), full(gq), full(gk), full(gcq),
                  full(w_uq), full(gckv), full(w_ukv)] + [pl.BlockSpec((tm, LANES), pos)] * 4,
        out_specs=[pl.BlockSpec((tm, w), row) for w in widths],
        out_shape=[jax.ShapeDtypeStruct((t, w), BF16) for w in widths],
        compiler_params=_cparams("parallel"),
        name="mixer_inproj",
    )(x, pre_g, w_in, gq, gk, gcq, w_uq, gckv, w_ukv, *tabs)


def _attn_kernel(q_ref, k_ref, v_ref, o_ref, *, group, dk, dv, tk, rb):
    tq = q_ref.shape[0]
    seq = k_ref.shape[0]
    per_head = tq // rb
    nblocks = group * per_head

    def q_block(i):
        j, r = divmod(i, per_head)
        return q_ref[r * rb:(r + 1) * rb, j * dk:(j + 1) * dk]

    def body(c, carry):
        ms, accs = carry
        start = pl.multiple_of(c * tk, tk)
        k = k_ref[pl.ds(start, tk), :]
        v = v_ref[pl.ds(start, tk), :]
        new_ms, new_accs = [], []
        for i in range(nblocks):
            s = lax.dot_general(q_block(i), k, (((1,), (1,)), ((), ())), preferred_element_type=F32)
            m_new = jnp.maximum(ms[i], jnp.max(s, axis=-1, keepdims=True))
            alpha = jnp.exp2(ms[i] - m_new)
            p = jnp.exp2((s - m_new).astype(BF16))
            new_accs.append(alpha * accs[i] + jnp.dot(p, v, preferred_element_type=F32))
            new_ms.append(m_new)
        return tuple(new_ms), tuple(new_accs)

    init = (tuple(jnp.full((rb, 1), -jnp.inf, F32) for _ in range(nblocks)),
            tuple(jnp.zeros((rb, 2 * dv), F32) for _ in range(nblocks)))
    _, accs = lax.fori_loop(0, seq // tk, body, init, unroll=ATTN_CHUNK_UNROLL)
    for i in range(nblocks):
        j, r = divmod(i, per_head)
        out = accs[i][:, :dv] * (1.0 / accs[i][:, dv:])
        o_ref[r * rb:(r + 1) * rb, j * dv:(j + 1) * dv] = out.astype(o_ref.dtype)


def _attention(q, k, v, *, kv_heads, group, dk, dv, tq, tk=512, rb=512):
    b, seq, _ = q.shape
    tq, tk = min(tq, seq), min(tk, seq)
    rb = min(rb, tq)
    assert seq % tq == 0 and seq % tk == 0 and tq % rb == 0
    kern = functools.partial(_attn_kernel, group=group, dk=dk, dv=dv, tk=tk, rb=rb)
    return pl.pallas_call(
        kern,
        grid=(b, kv_heads, seq // tq),
        in_specs=[
            pl.BlockSpec((None, tq, group * dk), lambda bi, h, i: (bi, i, h)),
            pl.BlockSpec((None, seq, dk), lambda bi, h, i: (bi, 0, h)),
            pl.BlockSpec((None, seq, 2 * dv), lambda bi, h, i: (bi, 0, h)),
        ],
        out_specs=pl.BlockSpec((None, tq, group * dv), lambda bi, h, i: (bi, i, h)),
        out_shape=jax.ShapeDtypeStruct((b, seq, kv_heads * group * dv), BF16),
        compiler_params=_cparams("parallel", "parallel", "arbitrary"),
        name=f"attention_g{group}_dk{dk}",
    )(q, k, v)


def _outproj_kernel(x_ref, oa_ref, ob_ref, wa_ref, wb_ref, post_g_ref, o_ref):
    for r in range(x_ref.shape[0] // PROJ_SUB_ROWS):
        rows = slice(r * PROJ_SUB_ROWS, (r + 1) * PROJ_SUB_ROWS)
        m = jnp.dot(oa_ref[rows, :], wa_ref[...], preferred_element_type=F32)
        m += jnp.dot(ob_ref[rows, :], wb_ref[...], preferred_element_type=F32)
        o_ref[rows, :] = x_ref[rows, :] + _rms(m, post_g_ref[...])


def _mixer_outproj(x, oa, ob, wa, wb, post_g, *, tm=512):
    t, d = x.shape
    row = lambda i: (i, 0)
    const = lambda i: (0, 0)
    return pl.pallas_call(
        _outproj_kernel,
        grid=(t // tm,),
        in_specs=[pl.BlockSpec((tm, d), row), pl.BlockSpec((tm, oa.shape[1]), row),
                  pl.BlockSpec((tm, ob.shape[1]), row),
                  pl.BlockSpec(wa.shape, const, pipeline_mode=pl.Buffered(1)),
                  pl.BlockSpec(wb.shape, const, pipeline_mode=pl.Buffered(1)), pl.BlockSpec((1, d), const)],
        out_specs=pl.BlockSpec((tm, d), row),
        out_shape=jax.ShapeDtypeStruct((t, d), F32),
        compiler_params=_cparams("parallel"),
        name="mixer_outproj",
    )(x, oa, ob, wa, wb, post_g)


def _pool_kernel(x_ref, prev_ref, next_ref, pre_g_ref, w_ref, scale_ref, post_g_ref, o_ref, ext_ref, m_ref,
                 *, seq):
    tm, d = x_ref.shape
    group = d // len(POOL_WINDOWS)
    nblk = seq // tm
    blk = pl.program_id(0) % nblk
    g = pre_g_ref[...]
    x = x_ref[...]
    ext_ref[0:POOL_HALO, :] = jnp.where(blk > 0, _rms(prev_ref[...], g), 0.0)
    ext_ref[POOL_HALO:POOL_HALO + tm, :] = _rms(x, g)
    ext_ref[POOL_HALO + tm:, :] = jnp.where(blk < nblk - 1, _rms(next_ref[...], g), 0.0)

    t = blk * tm + lax.broadcasted_iota(jnp.int32, (tm, 1), 0)
    n_ext = tm + 2 * POOL_HALO
    for gi, w in enumerate(POOL_WINDOWS):
        cols = slice(gi * group, (gi + 1) * group)
        half = w // 2
        run = ext_ref[:, cols]
        span = 1
        while span < w:
            run = run + pltpu.roll(run, span, 0)
            span *= 2
        lead = half - 1
        if lead:
            run = pltpu.roll(run, n_ext - lead, 0)
        tot = run[POOL_HALO:POOL_HALO + tm]
        cnt = (jnp.minimum(t + half, seq) - jnp.maximum(t - half, 0)).astype(F32)
        pooled = tot * (1.0 / cnt) - ext_ref[POOL_HALO:POOL_HALO + tm, cols]
        m_ref[:, cols] = jnp.dot(pooled.astype(BF16), w_ref[gi], preferred_element_type=F32)
    o_ref[...] = x + _rms(m_ref[...] * scale_ref[...], post_g_ref[...])


def _pool_mixer(x, seq, pre_g, pool_w, pool_scale, post_g, *, tm=512):
    t, d = x.shape
    assert seq % tm == 0 and max(POOL_WINDOWS) // 2 <= POOL_HALO
    hb = tm // POOL_HALO
    last = t // POOL_HALO - 1
    row = lambda i: (i, 0)
    const = lambda i: (0, 0)
    return pl.pallas_call(
        functools.partial(_pool_kernel, seq=seq),
        grid=(t // tm,),
        in_specs=[
            pl.BlockSpec((tm, d), row),
            pl.BlockSpec((POOL_HALO, d), lambda i: (jnp.maximum(i * hb - 1, 0), 0)),
            pl.BlockSpec((POOL_HALO, d), lambda i: (jnp.minimum((i + 1) * hb, last), 0)),
            pl.BlockSpec((1, d), const),
            pl.BlockSpec(pool_w.shape, lambda i: (0, 0, 0)),
            pl.BlockSpec((1, d), const),
            pl.BlockSpec((1, d), const),
        ],
        out_specs=pl.BlockSpec((tm, d), row),
        out_shape=jax.ShapeDtypeStruct((t, d), F32),
        scratch_shapes=[pltpu.VMEM((tm + 2 * POOL_HALO, d), F32), pltpu.VMEM((tm, d), F32)],
        compiler_params=_cparams("parallel"),
        name="pool_mixer",
    )(x, x, x, pre_g, pool_w, pool_scale, post_g)


def _rope_tables(seq):
    rows = seq // GRID_W
    row = jnp.repeat(jnp.arange(rows, dtype=F32), GRID_W)
    col = jnp.tile(jnp.arange(GRID_W, dtype=F32), rows)

    def angles(rot_dim):
        half = rot_dim // 2
        freqs = ROPE_BASE ** (-jnp.arange(0, half, 2, dtype=F32) / half)
        return jnp.concatenate([row[:, None] * freqs, col[:, None] * freqs], axis=-1)

    ang_a = angles(A_HEAD_DIM)
    cos_a = jnp.concatenate([jnp.cos(ang_a)] * 2, axis=-1)
    sin_a = jnp.concatenate([-jnp.sin(ang_a), jnp.sin(ang_a)], axis=-1)
    ang_b = angles(B_ROPE_DIM)
    z = jnp.zeros_like(ang_b)
    cos_b = jnp.concatenate([jnp.cos(ang_b), z, jnp.cos(ang_b), z], axis=-1)
    sin_b = jnp.concatenate([-jnp.sin(ang_b), z, jnp.sin(ang_b), z], axis=-1)
    return cos_a, sin_a, cos_b, sin_b


def _pair_split_perm(n):
    return np.concatenate([np.arange(0, n, 2), np.arange(1, n, 2)])


def _spread_rope_cols(w):
    half = B_ROPE_DIM // 2
    z = jnp.zeros((w.shape[0], half), w.dtype)
    return jnp.concatenate([w[:, 0::2], z, w[:, 1::2], z], axis=-1)


def _prep_mixer_weights(w_in, gq, gk, w_uq, w_ukv, w_out):
    d = w_in.shape[0]
    w_in, w_uq, w_ukv = w_in.astype(BF16), w_uq.astype(BF16), w_ukv.astype(BF16)
    perm = _pair_split_perm(A_HEAD_DIM)
    nq, nkv = A_HEADS * A_HEAD_DIM, A_KV_HEADS * A_HEAD_DIM
    wq = w_in[:, :nq].reshape(d, A_HEADS, A_HEAD_DIM)[:, :, perm].reshape(d, nq)
    wk = w_in[:, nq:nq + nkv].reshape(d, A_KV_HEADS, A_HEAD_DIM)[:, :, perm].reshape(d, nkv)
    rest = w_in[:, nq + nkv:-B_ROPE_DIM]
    w_in_p = jnp.concatenate([wq, wk, rest, _spread_rope_cols(w_in[:, -B_ROPE_DIM:])], axis=-1)

    uq = w_uq.reshape(B_Q_LORA, B_HEADS, B_NOPE_DIM + B_ROPE_DIM)
    uq_rope = _spread_rope_cols(uq[:, :, B_NOPE_DIM:].reshape(B_Q_LORA * B_HEADS, B_ROPE_DIM))
    uq_p = jnp.concatenate([uq[:, :, :B_NOPE_DIM], uq_rope.reshape(B_Q_LORA, B_HEADS, LANES)], axis=-1)
    uq_p = uq_p.reshape(B_Q_LORA, B_HEADS * B_QK_PAD)

    ukv = w_ukv.reshape(B_KV_LORA, B_HEADS, B_NOPE_DIM + B_V_DIM)
    ukv_p = jnp.concatenate([ukv[:, :, :B_NOPE_DIM].reshape(B_KV_LORA, -1),
                             ukv[:, :, B_NOPE_DIM:].reshape(B_KV_LORA, -1)], axis=-1)
    na = A_HEADS * A_HEAD_DIM
    return (w_in_p, gq[perm][None, :], gk[perm][None, :], uq_p, ukv_p,
            w_out[:na].astype(BF16), w_out[na:].astype(BF16))


def _trunk(x, ffn_params, mix0, mix1, tabs):
    b, seq, d = x.shape
    x = x.reshape(b * seq, d)
    (pre0, w_in, gq, gk, gcq, w_uq, gckv, w_ukv, wa, wb, post0) = mix0
    (pre1, pool_w, pool_scale, post1) = mix1

    x = _ffn_half(x, *ffn_params[0])
    qa, ka, va, qb, kb, vb = _mixer_inproj(x, seq, pre0, w_in, gq, gk, gcq, w_uq, gckv, w_ukv, tabs)
    shp = lambda a: a.reshape(b, seq, a.shape[-1])
    oa = _attention(shp(qa), shp(ka), shp(va), kv_heads=A_KV_HEADS, group=A_HEADS // A_KV_HEADS,
                    dk=A_HEAD_DIM, dv=A_HEAD_DIM, tq=512)
    ob = _attention(shp(qb), shp(kb), shp(vb), kv_heads=B_HEADS, group=1, dk=B_QK_PAD, dv=B_V_DIM, tq=2048)
    x = _mixer_outproj(x, oa.reshape(b * seq, -1), ob.reshape(b * seq, -1), wa, wb, post0)
    x = _ffn_half(x, *ffn_params[1])
    x = _ffn_half(x, *ffn_params[2])
    x = _pool_mixer(x, seq, pre1, pool_w, pool_scale, post1)
    x = _ffn_half(x, *ffn_params[3])
    return x.reshape(b, seq, d)


def kernel(x_prompt, x_sample, l0_ffn1_pre_g, l0_ffn1_w_gate, l0_ffn1_w_up, l0_ffn1_w_down, l0_ffn1_post_g, l0_mix_pre_g, l0_w_in, l0_a_q_norm_g, l0_a_k_norm_g, l0_b_cq_norm_g, l0_b_w_uq, l0_b_ckv_norm_g, l0_b_w_ukv, l0_w_out, l0_mix_post_g, l0_ffn2_pre_g, l0_ffn2_w_gate, l0_ffn2_w_up, l0_ffn2_w_down, l0_ffn2_post_g, l1_ffn1_pre_g, l1_ffn1_w_gate, l1_ffn1_w_up, l1_ffn1_w_down, l1_ffn1_post_g, l1_mix_pre_g, l1_pool_w, l1_pool_scale, l1_mix_post_g, l1_ffn2_pre_g, l1_ffn2_w_gate, l1_ffn2_w_up, l1_ffn2_w_down, l1_ffn2_post_g):
    vec = lambda g: g[None, :]
    ffn = lambda pre, wg, wu, wd, post: (vec(pre), wg.astype(BF16), wu.astype(BF16), wd.astype(BF16), vec(post))
    ffn_params = (
        ffn(l0_ffn1_pre_g, l0_ffn1_w_gate, l0_ffn1_w_up, l0_ffn1_w_down, l0_ffn1_post_g),
        ffn(l0_ffn2_pre_g, l0_ffn2_w_gate, l0_ffn2_w_up, l0_ffn2_w_down, l0_ffn2_post_g),
        ffn(l1_ffn1_pre_g, l1_ffn1_w_gate, l1_ffn1_w_up, l1_ffn1_w_down, l1_ffn1_post_g),
        ffn(l1_ffn2_pre_g, l1_ffn2_w_gate, l1_ffn2_w_up, l1_ffn2_w_down, l1_ffn2_post_g),
    )
    w_in, gq, gk, w_uq, w_ukv, wa, wb = _prep_mixer_weights(
        l0_w_in, l0_a_q_norm_g, l0_a_k_norm_g, l0_b_w_uq, l0_b_w_ukv, l0_w_out)
    mix0 = (vec(l0_mix_pre_g), w_in, gq, gk, vec(l0_b_cq_norm_g), w_uq, vec(l0_b_ckv_norm_g), w_ukv,
            wa, wb, vec(l0_mix_post_g))
    mix1 = (vec(l1_mix_pre_g), l1_pool_w.astype(BF16), vec(l1_pool_scale), vec(l1_mix_post_g))

    outs = []
    for x in (x_prompt, x_sample):
        tabs = _rope_tables(x.shape[1])
        outs.append(_trunk(x, ffn_params, mix0, mix1, tabs))
    return tuple(outs)
```

```python
import functools
import math

import jax
import jax.numpy as jnp
import numpy as np
from jax import lax
from jax.experimental import pallas as pl
from jax.experimental.pallas import tpu as pltpu

F32 = jnp.float32
BF16 = jnp.bfloat16

NORM_EPS = 1e-6
ROPE_BASE = 10000.0
GRID_W = 64
A_HEADS = 8
A_KV_HEADS = 2
A_HEAD_DIM = 128
B_HEADS = 8
B_Q_LORA = 512
B_KV_LORA = 256
B_NOPE_DIM = 128
B_ROPE_DIM = 64
B_V_DIM = 128
POOL_WINDOWS = (2, 4, 8, 16)
LANES = 128
LOG2_E = math.log2(math.e)
ATTN_CHUNK_UNROLL = 16
B_QK_PAD = 2 * LANES
ROW_CHUNK = 32
PROJ_SUB_ROWS = 512
FFN_SUB_ROWS = 512
POOL_HALO = 8

VMEM_LIMIT_BYTES = 60 * 1024 * 1024


def _cparams(*semantics):
    return pltpu.CompilerParams(dimension_semantics=semantics, vmem_limit_bytes=VMEM_LIMIT_BYTES)


def _rms(x, g):
    return x * lax.rsqrt(jnp.mean(x * x, axis=-1, keepdims=True) + NORM_EPS) * g


def _silu(x):
    return x * (1.0 / (1.0 + jnp.exp(-x)))


def _ffn_step(x_ref, pre_g_ref, wg_ref, wu_ref, wd_ref, post_g_ref, o_ref, h_ref, *, first, last):
    tm = x_ref.shape[0]
    sub = FFN_SUB_ROWS if (first or last) else tm
    for r in range(tm // sub):
        base = r * sub
        rows = slice(base, base + sub)
        if first:
            for c in range(sub // ROW_CHUNK):
                chunk = slice(base + c * ROW_CHUNK, base + (c + 1) * ROW_CHUNK)
                h_ref[chunk, :] = _rms(x_ref[chunk, :], pre_g_ref[...]).astype(BF16)
        h = h_ref[rows, :]
        gate = jnp.dot(h, wg_ref[...], preferred_element_type=F32)
        up = jnp.dot(h, wu_ref[...], preferred_element_type=F32)
        act = (_silu(gate) * up).astype(BF16)
        down = jnp.dot(act, wd_ref[...], preferred_element_type=F32)
        if first:
            o_ref[rows, :] = down
        else:
            o_ref[rows, :] += down
        if last:
            for c in range(sub // ROW_CHUNK):
                chunk = slice(base + c * ROW_CHUNK, base + (c + 1) * ROW_CHUNK)
                o_ref[chunk, :] = x_ref[chunk, :] + 0.5 * _rms(o_ref[chunk, :], post_g_ref[...])


def _ffn_kernel(*refs):
    f = pl.program_id(1)
    nf = pl.num_programs(1)
    pl.when(f == 0)(functools.partial(_ffn_step, *refs, first=True, last=False))
    pl.when(jnp.logical_and(f > 0, f < nf - 1))(functools.partial(_ffn_step, *refs, first=False, last=False))
    pl.when(f == nf - 1)(functools.partial(_ffn_step, *refs, first=False, last=True))


def _ffn_half(x, pre_g, wg, wu, wd, post_g, *, tm=1024, tf=512):
    t, d = x.shape
    dff = wg.shape[1]
    assert t % tm == 0 and dff % tf == 0 and dff // tf >= 2 and tm % FFN_SUB_ROWS == 0
    row = lambda i, f: (i, 0)
    const = lambda i, f: (0, 0)
    return pl.pallas_call(
        _ffn_kernel,
        grid=(t // tm, dff // tf),
        in_specs=[
            pl.BlockSpec((tm, d), row),
            pl.BlockSpec((1, d), const),
            pl.BlockSpec((d, tf), lambda i, f: (0, f)),
            pl.BlockSpec((d, tf), lambda i, f: (0, f)),
            pl.BlockSpec((tf, d), lambda i, f: (f, 0)),
            pl.BlockSpec((1, d), const),
        ],
        out_specs=pl.BlockSpec((tm, d), row),
        out_shape=jax.ShapeDtypeStruct((t, d), F32),
        scratch_shapes=[pltpu.VMEM((tm, d), BF16)],
        compiler_params=_cparams("parallel", "arbitrary"),
        name="ffn_half",
    )(x, pre_g, wg, wu, wd, post_g)


def _rope(x, cos, sin):
    return x * cos + pltpu.roll(x, LANES // 2, 1) * sin


def _inproj_kernel(x_ref, pre_g_ref, w_in_ref, gq_ref, gk_ref, gcq_ref, w_uq_ref, gckv_ref, w_ukv_ref,
                   cos_a_ref, sin_a_ref, cos_b_ref, sin_b_ref, *out_refs):
    for r in range(x_ref.shape[0] // PROJ_SUB_ROWS):
        rows = pl.ds(r * PROJ_SUB_ROWS, PROJ_SUB_ROWS)
        at = lambda ref: ref.at[rows]
        _inproj_rows(at(x_ref), pre_g_ref, w_in_ref, gq_ref, gk_ref, gcq_ref, w_uq_ref, gckv_ref, w_ukv_ref,
                     at(cos_a_ref), at(sin_a_ref), at(cos_b_ref), at(sin_b_ref), *map(at, out_refs))


def _inproj_rows(x_ref, pre_g_ref, w_in_ref, gq_ref, gk_ref, gcq_ref, w_uq_ref, gckv_ref, w_ukv_ref,
                 cos_a_ref, sin_a_ref, cos_b_ref, sin_b_ref,
                 qa_ref, ka_ref, va_ref, qb_ref, kb_ref, vb_ref):
    hn = _rms(x_ref[...], pre_g_ref[...]).astype(BF16)
    y = jnp.dot(hn, w_in_ref[...], preferred_element_type=F32)
    cos_a, sin_a = cos_a_ref[...], sin_a_ref[...]
    cos_b, sin_b = cos_b_ref[...], sin_b_ref[...]
    a_scale = A_HEAD_DIM ** -0.5 * LOG2_E
    b_scale = (B_NOPE_DIM + B_ROPE_DIM) ** -0.5 * LOG2_E
    ones = jnp.ones((x_ref.shape[0], LANES), BF16)
    q_cols = A_HEADS * A_HEAD_DIM
    kv_cols = A_KV_HEADS * A_HEAD_DIM

    for h in range(A_HEADS):
        sl = slice(h * LANES, (h + 1) * LANES)
        qa_ref[:, sl] = (_rope(_rms(y[:, sl], gq_ref[...]), cos_a, sin_a) * a_scale).astype(BF16)
    for h in range(A_KV_HEADS):
        sl = slice(h * LANES, (h + 1) * LANES)
        ka_ref[:, sl] = _rope(_rms(y[:, q_cols + h * LANES:q_cols + (h + 1) * LANES], gk_ref[...]),
                              cos_a, sin_a).astype(BF16)
    off = q_cols + kv_cols
    for h in range(A_KV_HEADS):
        va_ref[:, 2 * h * LANES:(2 * h + 1) * LANES] = y[:, off + h * LANES:off + (h + 1) * LANES].astype(BF16)
        va_ref[:, (2 * h + 1) * LANES:(2 * h + 2) * LANES] = ones
    off += kv_cols

    cq = _rms(y[:, off:off + B_Q_LORA], gcq_ref[...]).astype(BF16)
    off += B_Q_LORA
    ckv = _rms(y[:, off:off + B_KV_LORA], gckv_ref[...]).astype(BF16)
    off += B_KV_LORA
    k_rope = _rope(y[:, off:off + LANES], cos_b, sin_b).astype(BF16)

    qf = jnp.dot(cq, w_uq_ref[...], preferred_element_type=F32)
    kv = jnp.dot(ckv, w_ukv_ref[...], preferred_element_type=F32)
    for h in range(B_HEADS):
        base = h * B_QK_PAD
        qb_ref[:, base:base + LANES] = (qf[:, base:base + LANES] * b_scale).astype(BF16)
        qb_ref[:, base + LANES:base + B_QK_PAD] = (
            _rope(qf[:, base + LANES:base + B_QK_PAD], cos_b, sin_b) * b_scale).astype(BF16)
        kb_ref[:, base:base + LANES] = kv[:, h * LANES:(h + 1) * LANES].astype(BF16)
        kb_ref[:, base + LANES:base + B_QK_PAD] = k_rope
        v_col = B_HEADS * B_NOPE_DIM + h * LANES
        vb_ref[:, 2 * h * LANES:(2 * h + 1) * LANES] = kv[:, v_col:v_col + LANES].astype(BF16)
        vb_ref[:, (2 * h + 1) * LANES:(2 * h + 2) * LANES] = ones


def _mixer_inproj(x, seq, pre_g, w_in, gq, gk, gcq, w_uq, gckv, w_ukv, tabs, *, tm=512):
    t, d = x.shape
    assert seq % tm == 0 and tm % PROJ_SUB_ROWS == 0
    nblk = seq // tm
    row = lambda i: (i, 0)
    const = lambda i: (0, 0)
    pos = lambda i: (i % nblk, 0)
    full = lambda a: pl.BlockSpec(a.shape, const, pipeline_mode=pl.Buffered(1))
    widths = (A_HEADS * A_HEAD_DIM, A_KV_HEADS * A_HEAD_DIM, 2 * A_KV_HEADS * A_HEAD_DIM,
              B_HEADS * B_QK_PAD, B_HEADS * B_QK_PAD, 2 * B_HEADS * B_V_DIM)
    return pl.pallas_call(
        _inproj_kernel,
        grid=(t // tm,),
        in_specs=[pl.BlockSpec((tm, d), row), full(pre_g), full(w_in), full(gq), full(gk), full(gcq),
                  full(w_uq), full(gckv), full(w_ukv)] + [pl.BlockSpec((tm, LANES), pos)] * 4,
        out_specs=[pl.BlockSpec((tm, w), row) for w in widths],
        out_shape=[jax.ShapeDtypeStruct((t, w), BF16) for w in widths],
        compiler_params=_cparams("parallel"),
        name="mixer_inproj",
    )(x, pre_g, w_in, gq, gk, gcq, w_uq, gckv, w_ukv, *tabs)


def _attn_kernel(q_ref, k_ref, v_ref, o_ref, *, group, dk, dv, tk, rb):
    tq = q_ref.shape[0]
    seq = k_ref.shape[0]
    per_head = tq // rb
    nblocks = group * per_head

    def q_block(i):
        j, r = divmod(i, per_head)
        return q_ref[r * rb:(r + 1) * rb, j * dk:(j + 1) * dk]

    def body(c, carry):
        ms, accs = carry
        start = pl.multiple_of(c * tk, tk)
        k = k_ref[pl.ds(start, tk), :]
        v = v_ref[pl.ds(start, tk), :]
        new_ms, new_accs = [], []
        for i in range(nblocks):
            s = lax.dot_general(q_block(i), k, (((1,), (1,)), ((), ())), preferred_element_type=F32)
            m_new = jnp.maximum(ms[i], jnp.max(s, axis=-1, keepdims=True))
            alpha = jnp.exp2(ms[i] - m_new)
            p = jnp.exp2(s - m_new).astype(BF16)
            new_accs.append(alpha * accs[i] + jnp.dot(p, v, preferred_element_type=F32))
            new_ms.append(m_new)
        return tuple(new_ms), tuple(new_accs)

    init = (tuple(jnp.full((rb, 1), -jnp.inf, F32) for _ in range(nblocks)),
            tuple(jnp.zeros((rb, 2 * dv), F32) for _ in range(nblocks)))
    _, accs = lax.fori_loop(0, seq // tk, body, init, unroll=ATTN_CHUNK_UNROLL)
    for i in range(nblocks):
        j, r = divmod(i, per_head)
        out = accs[i][:, :dv] * (1.0 / accs[i][:, dv:])
        o_ref[r * rb:(r + 1) * rb, j * dv:(j + 1) * dv] = out.astype(o_ref.dtype)


def _attention(q, k, v, *, kv_heads, group, dk, dv, tq, tk=512, rb=1024):
    b, seq, _ = q.shape
    tq, tk = min(tq, seq), min(tk, seq)
    rb = min(rb, tq)
    assert seq % tq == 0 and seq % tk == 0 and tq % rb == 0
    kern = functools.partial(_attn_kernel, group=group, dk=dk, dv=dv, tk=tk, rb=rb)
    return pl.pallas_call(
        kern,
        grid=(b, kv_heads, seq // tq),
        in_specs=[
            pl.BlockSpec((None, tq, group * dk), lambda bi, h, i: (bi, i, h)),
            pl.BlockSpec((None, seq, dk), lambda bi, h, i: (bi, 0, h)),
            pl.BlockSpec((None, seq, 2 * dv), lambda bi, h, i: (bi, 0, h)),
        ],
        out_specs=pl.BlockSpec((None, tq, group * dv), lambda bi, h, i: (bi, i, h)),
        out_shape=jax.ShapeDtypeStruct((b, seq, kv_heads * group * dv), BF16),
        compiler_params=_cparams("parallel", "parallel", "arbitrary"),
        name=f"attention_g{group}_dk{dk}",
    )(q, k, v)


def _outproj_kernel(x_ref, oa_ref, ob_ref, wa_ref, wb_ref, post_g_ref, o_ref):
    for r in range(x_ref.shape[0] // PROJ_SUB_ROWS):
        rows = slice(r * PROJ_SUB_ROWS, (r + 1) * PROJ_SUB_ROWS)
        m = jnp.dot(oa_ref[rows, :], wa_ref[...], preferred_element_type=F32)
        m += jnp.dot(ob_ref[rows, :], wb_ref[...], preferred_element_type=F32)
        o_ref[rows, :] = x_ref[rows, :] + _rms(m, post_g_ref[...])


def _mixer_outproj(x, oa, ob, wa, wb, post_g, *, tm=512):
    t, d = x.shape
    row = lambda i: (i, 0)
    const = lambda i: (0, 0)
    return pl.pallas_call(
        _outproj_kernel,
        grid=(t // tm,),
        in_specs=[pl.BlockSpec((tm, d), row), pl.BlockSpec((tm, oa.shape[1]), row),
                  pl.BlockSpec((tm, ob.shape[1]), row),
                  pl.BlockSpec(wa.shape, const, pipeline_mode=pl.Buffered(1)),
                  pl.BlockSpec(wb.shape, const, pipeline_mode=pl.Buffered(1)), pl.BlockSpec((1, d), const)],
        out_specs=pl.BlockSpec((tm, d), row),
        out_shape=jax.ShapeDtypeStruct((t, d), F32),
        compiler_params=_cparams("parallel"),
        name="mixer_outproj",
    )(x, oa, ob, wa, wb, post_g)


def _pool_kernel(x_ref, prev_ref, next_ref, pre_g_ref, w_ref, scale_ref, post_g_ref, o_ref, ext_ref, m_ref,
                 *, seq):
    tm, d = x_ref.shape
    group = d // len(POOL_WINDOWS)
    nblk = seq // tm
    blk = pl.program_id(0) % nblk
    g = pre_g_ref[...]
    x = x_ref[...]
    ext_ref[0:POOL_HALO, :] = jnp.where(blk > 0, _rms(prev_ref[...], g), 0.0)
    ext_ref[POOL_HALO:POOL_HALO + tm, :] = _rms(x, g)
    ext_ref[POOL_HALO + tm:, :] = jnp.where(blk < nblk - 1, _rms(next_ref[...], g), 0.0)

    t = blk * tm + lax.broadcasted_iota(jnp.int32, (tm, 1), 0)
    n_ext = tm + 2 * POOL_HALO
    for gi, w in enumerate(POOL_WINDOWS):
        cols = slice(gi * group, (gi + 1) * group)
        half = w // 2
        run = ext_ref[:, cols]
        span = 1
        while span < w:
            run = run + pltpu.roll(run, span, 0)
            span *= 2
        lead = half - 1
        if lead:
            run = pltpu.roll(run, n_ext - lead, 0)
        tot = run[POOL_HALO:POOL_HALO + tm]
        cnt = (jnp.minimum(t + half, seq) - jnp.maximum(t - half, 0)).astype(F32)
        pooled = tot * (1.0 / cnt) - ext_ref[POOL_HALO:POOL_HALO + tm, cols]
        m_ref[:, cols] = jnp.dot(pooled.astype(BF16), w_ref[gi], preferred_element_type=F32)
    o_ref[...] = x + _rms(m_ref[...] * scale_ref[...], post_g_ref[...])


def _pool_mixer(x, seq, pre_g, pool_w, pool_scale, post_g, *, tm=512):
    t, d = x.shape
    assert seq % tm == 0 and max(POOL_WINDOWS) // 2 <= POOL_HALO
    hb = tm // POOL_HALO
    last = t // POOL_HALO - 1
    row = lambda i: (i, 0)
    const = lambda i: (0, 0)
    return pl.pallas_call(
        functools.partial(_pool_kernel, seq=seq),
        grid=(t // tm,),
        in_specs=[
            pl.BlockSpec((tm, d), row),
            pl.BlockSpec((POOL_HALO, d), lambda i: (jnp.maximum(i * hb - 1, 0), 0)),
            pl.BlockSpec((POOL_HALO, d), lambda i: (jnp.minimum((i + 1) * hb, last), 0)),
            pl.BlockSpec((1, d), const),
            pl.BlockSpec(pool_w.shape, lambda i: (0, 0, 0)),
            pl.BlockSpec((1, d), const),
            pl.BlockSpec((1, d), const),
        ],
        out_specs=pl.BlockSpec((tm, d), row),
        out_shape=jax.ShapeDtypeStruct((t, d), F32),
        scratch_shapes=[pltpu.VMEM((tm + 2 * POOL_HALO, d), F32), pltpu.VMEM((tm, d), F32)],
        compiler_params=_cparams("parallel"),
        name="pool_mixer",
    )(x, x, x, pre_g, pool_w, pool_scale, post_g)


def _rope_tables(seq):
    rows = seq // GRID_W
    row = jnp.repeat(jnp.arange(rows, dtype=F32), GRID_W)
    col = jnp.tile(jnp.arange(GRID_W, dtype=F32), rows)

    def angles(rot_dim):
        half = rot_dim // 2
        freqs = ROPE_BASE ** (-jnp.arange(0, half, 2, dtype=F32) / half)
        return jnp.concatenate([row[:, None] * freqs, col[:, None] * freqs], axis=-1)

    ang_a = angles(A_HEAD_DIM)
    cos_a = jnp.concatenate([jnp.cos(ang_a)] * 2, axis=-1)
    sin_a = jnp.concatenate([-jnp.sin(ang_a), jnp.sin(ang_a)], axis=-1)
    ang_b = angles(B_ROPE_DIM)
    z = jnp.zeros_like(ang_b)
    cos_b = jnp.concatenate([jnp.cos(ang_b), z, jnp.cos(ang_b), z], axis=-1)
    sin_b = jnp.concatenate([-jnp.sin(ang_b), z, jnp.sin(ang_b), z], axis=-1)
    return cos_a, sin_a, cos_b, sin_b


def _pair_split_perm(n):
    return np.concatenate([np.arange(0, n, 2), np.arange(1, n, 2)])


def _spread_rope_cols(w):
    half = B_ROPE_DIM // 2
    z = jnp.zeros((w.shape[0], half), w.dtype)
    return jnp.concatenate([w[:, 0::2], z, w[:, 1::2], z], axis=-1)


def _prep_mixer_weights(w_in, gq, gk, w_uq, w_ukv, w_out):
    d = w_in.shape[0]
    w_in, w_uq, w_ukv = w_in.astype(BF16), w_uq.astype(BF16), w_ukv.astype(BF16)
    perm = _pair_split_perm(A_HEAD_DIM)
    nq, nkv = A_HEADS * A_HEAD_DIM, A_KV_HEADS * A_HEAD_DIM
    wq = w_in[:, :nq].reshape(d, A_HEADS, A_HEAD_DIM)[:, :, perm].reshape(d, nq)
    wk = w_in[:, nq:nq + nkv].reshape(d, A_KV_HEADS, A_HEAD_DIM)[:, :, perm].reshape(d, nkv)
    rest = w_in[:, nq + nkv:-B_ROPE_DIM]
    w_in_p = jnp.concatenate([wq, wk, rest, _spread_rope_cols(w_in[:, -B_ROPE_DIM:])], axis=-1)

    uq = w_uq.reshape(B_Q_LORA, B_HEADS, B_NOPE_DIM + B_ROPE_DIM)
    uq_rope = _spread_rope_cols(uq[:, :, B_NOPE_DIM:].reshape(B_Q_LORA * B_HEADS, B_ROPE_DIM))
    uq_p = jnp.concatenate([uq[:, :, :B_NOPE_DIM], uq_rope.reshape(B_Q_LORA, B_HEADS, LANES)], axis=-1)
    uq_p = uq_p.reshape(B_Q_LORA, B_HEADS * B_QK_PAD)

    ukv = w_ukv.reshape(B_KV_LORA, B_HEADS, B_NOPE_DIM + B_V_DIM)
    ukv_p = jnp.concatenate([ukv[:, :, :B_NOPE_DIM].reshape(B_KV_LORA, -1),
                             ukv[:, :, B_NOPE_DIM:].reshape(B_KV_LORA, -1)], axis=-1)
    na = A_HEADS * A_HEAD_DIM
    return (w_in_p, gq[perm][None, :], gk[perm][None, :], uq_p, ukv_p,
            w_out[:na].astype(BF16), w_out[na:].astype(BF16))


def _trunk(x, ffn_params, mix0, mix1, tabs):
    b, seq, d = x.shape
    x = x.reshape(b * seq, d)
    (pre0, w_in, gq, gk, gcq, w_uq, gckv, w_ukv, wa, wb, post0) = mix0
    (pre1, pool_w, pool_scale, post1) = mix1

    x = _ffn_half(x, *ffn_params[0])
    qa, ka, va, qb, kb, vb = _mixer_inproj(x, seq, pre0, w_in, gq, gk, gcq, w_uq, gckv, w_ukv, tabs)
    shp = lambda a: a.reshape(b, seq, a.shape[-1])
    oa = _attention(shp(qa), shp(ka), shp(va), kv_heads=A_KV_HEADS, group=A_HEADS // A_KV_HEADS,
                    dk=A_HEAD_DIM, dv=A_HEAD_DIM, tq=512)
    ob = _attention(shp(qb), shp(kb), shp(vb), kv_heads=B_HEADS, group=1, dk=B_QK_PAD, dv=B_V_DIM, tq=2048)
    x = _mixer_outproj(x, oa.reshape(b * seq, -1), ob.reshape(b * seq, -1), wa, wb, post0)
    x = _ffn_half(x, *ffn_params[1])
    x = _ffn_half(x, *ffn_params[2])
    x = _pool_mixer(x, seq, pre1, pool_w, pool_scale, post1)
    x = _ffn_half(x, *ffn_params[3])
    return x.reshape(b, seq, d)


def kernel(x_prompt, x_sample, l0_ffn1_pre_g, l0_ffn1_w_gate, l0_ffn1_w_up, l0_ffn1_w_down, l0_ffn1_post_g, l0_mix_pre_g, l0_w_in, l0_a_q_norm_g, l0_a_k_norm_g, l0_b_cq_norm_g, l0_b_w_uq, l0_b_ckv_norm_g, l0_b_w_ukv, l0_w_out, l0_mix_post_g, l0_ffn2_pre_g, l0_ffn2_w_gate, l0_ffn2_w_up, l0_ffn2_w_down, l0_ffn2_post_g, l1_ffn1_pre_g, l1_ffn1_w_gate, l1_ffn1_w_up, l1_ffn1_w_down, l1_ffn1_post_g, l1_mix_pre_g, l1_pool_w, l1_pool_scale, l1_mix_post_g, l1_ffn2_pre_g, l1_ffn2_w_gate, l1_ffn2_w_up, l1_ffn2_w_down, l1_ffn2_post_g):
    vec = lambda g: g[None, :]
    ffn = lambda pre, wg, wu, wd, post: (vec(pre), wg.astype(BF16), wu.astype(BF16), wd.astype(BF16), vec(post))
    ffn_params = (
        ffn(l0_ffn1_pre_g, l0_ffn1_w_gate, l0_ffn1_w_up, l0_ffn1_w_down, l0_ffn1_post_g),
        ffn(l0_ffn2_pre_g, l0_ffn2_w_gate, l0_ffn2_w_up, l0_ffn2_w_down, l0_ffn2_post_g),
        ffn(l1_ffn1_pre_g, l1_ffn1_w_gate, l1_ffn1_w_up, l1_ffn1_w_down, l1_ffn1_post_g),
        ffn(l1_ffn2_pre_g, l1_ffn2_w_gate, l1_ffn2_w_up, l1_ffn2_w_down, l1_ffn2_post_g),
    )
    w_in, gq, gk, w_uq, w_ukv, wa, wb = _prep_mixer_weights(
        l0_w_in, l0_a_q_norm_g, l0_a_k_norm_g, l0_b_w_uq, l0_b_w_ukv, l0_w_out)
    mix0 = (vec(l0_mix_pre_g), w_in, gq, gk, vec(l0_b_cq_norm_g), w_uq, vec(l0_b_ckv_norm_g), w_ukv,
            wa, wb, vec(l0_mix_post_g))
    mix1 = (vec(l1_mix_pre_g), l1_pool_w.astype(BF16), vec(l1_pool_scale), vec(l1_mix_post_g))

    outs = []
    for x in (x_prompt, x_sample):
        tabs = _rope_tables(x.shape[1])
        outs.append(_trunk(x, ffn_params, mix0, mix1, tabs))
    return tuple(outs)
```

```python
import functools
import math

import jax
import jax.numpy as jnp
import numpy as np
from jax import lax
from jax.experimental import pallas as pl
from jax.experimental.pallas import tpu as pltpu

F32 = jnp.float32
BF16 = jnp.bfloat16

NORM_EPS = 1e-6
ROPE_BASE = 10000.0
GRID_W = 64
A_HEADS = 8
A_KV_HEADS = 2
A_HEAD_DIM = 128
B_HEADS = 8
B_Q_LORA = 512
B_KV_LORA = 256
B_NOPE_DIM = 128
B_ROPE_DIM = 64
B_V_DIM = 128
POOL_WINDOWS = (2, 4, 8, 16)
LANES = 128
LOG2_E = math.log2(math.e)
ATTN_CHUNK_UNROLL = 16
B_QK_PAD = 2 * LANES
ROW_CHUNK = 32
PROJ_SUB_ROWS = 512
FFN_SUB_ROWS = 512
POOL_HALO = 8

VMEM_LIMIT_BYTES = 60 * 1024 * 1024


def _cparams(*semantics):
    return pltpu.CompilerParams(dimension_semantics=semantics, vmem_limit_bytes=VMEM_LIMIT_BYTES)


def _rms(x, g):
    return x * lax.rsqrt(jnp.mean(x * x, axis=-1, keepdims=True) + NORM_EPS) * g


def _silu(x):
    return x * (1.0 / (1.0 + jnp.exp(-x)))


def _ffn_step(x_ref, pre_g_ref, wg_ref, wu_ref, wd_ref, post_g_ref, o_ref, h_ref, *, first, last):
    tm = x_ref.shape[0]
    sub = FFN_SUB_ROWS if (first or last) else tm
    for r in range(tm // sub):
        base = r * sub
        rows = slice(base, base + sub)
        if first:
            for c in range(sub // ROW_CHUNK):
                chunk = slice(base + c * ROW_CHUNK, base + (c + 1) * ROW_CHUNK)
                h_ref[chunk, :] = _rms(x_ref[chunk, :], pre_g_ref[...]).astype(BF16)
        h = h_ref[rows, :]
        gate = jnp.dot(h, wg_ref[...], preferred_element_type=F32)
        up = jnp.dot(h, wu_ref[...], preferred_element_type=F32)
        act = (_silu(gate) * up).astype(BF16)
        down = jnp.dot(act, wd_ref[...], preferred_element_type=F32)
        if first:
            o_ref[rows, :] = down
        else:
            o_ref[rows, :] += down
        if last:
            for c in range(sub // ROW_CHUNK):
                chunk = slice(base + c * ROW_CHUNK, base + (c + 1) * ROW_CHUNK)
                o_ref[chunk, :] = x_ref[chunk, :] + 0.5 * _rms(o_ref[chunk, :], post_g_ref[...])


def _ffn_kernel(*refs):
    f = pl.program_id(1)
    nf = pl.num_programs(1)
    pl.when(f == 0)(functools.partial(_ffn_step, *refs, first=True, last=False))
    pl.when(jnp.logical_and(f > 0, f < nf - 1))(functools.partial(_ffn_step, *refs, first=False, last=False))
    pl.when(f == nf - 1)(functools.partial(_ffn_step, *refs, first=False, last=True))


def _ffn_half(x, pre_g, wg, wu, wd, post_g, *, tm=1024, tf=512):
    t, d = x.shape
    dff = wg.shape[1]
    assert t % tm == 0 and dff % tf == 0 and dff // tf >= 2 and tm % FFN_SUB_ROWS == 0
    row = lambda i, f: (i, 0)
    const = lambda i, f: (0, 0)
    return pl.pallas_call(
        _ffn_kernel,
        grid=(t // tm, dff // tf),
        in_specs=[
            pl.BlockSpec((tm, d), row),
            pl.BlockSpec((1, d), const),
            pl.BlockSpec((d, tf), lambda i, f: (0, f)),
            pl.BlockSpec((d, tf), lambda i, f: (0, f)),
            pl.BlockSpec((tf, d), lambda i, f: (f, 0)),
            pl.BlockSpec((1, d), const),
        ],
        out_specs=pl.BlockSpec((tm, d), row),
        out_shape=jax.ShapeDtypeStruct((t, d), F32),
        scratch_shapes=[pltpu.VMEM((tm, d), BF16)],
        compiler_params=_cparams("parallel", "arbitrary"),
        name="ffn_half",
    )(x, pre_g, wg, wu, wd, post_g)


def _rope(x, cos, sin):
    return x * cos + pltpu.roll(x, LANES // 2, 1) * sin


def _inproj_kernel(x_ref, pre_g_ref, w_in_ref, gq_ref, gk_ref, gcq_ref, w_uq_ref, gckv_ref, w_ukv_ref,
                   cos_a_ref, sin_a_ref, cos_b_ref, sin_b_ref, *out_refs):
    for r in range(x_ref.shape[0] // PROJ_SUB_ROWS):
        rows = pl.ds(r * PROJ_SUB_ROWS, PROJ_SUB_ROWS)
        at = lambda ref: ref.at[rows]
        _inproj_rows(at(x_ref), pre_g_ref, w_in_ref, gq_ref, gk_ref, gcq_ref, w_uq_ref, gckv_ref, w_ukv_ref,
                     at(cos_a_ref), at(sin_a_ref), at(cos_b_ref), at(sin_b_ref), *map(at, out_refs))


def _inproj_rows(x_ref, pre_g_ref, w_in_ref, gq_ref, gk_ref, gcq_ref, w_uq_ref, gckv_ref, w_ukv_ref,
                 cos_a_ref, sin_a_ref, cos_b_ref, sin_b_ref,
                 qa_ref, ka_ref, va_ref, qb_ref, kb_ref, vb_ref):
    hn = _rms(x_ref[...], pre_g_ref[...]).astype(BF16)
    y = jnp.dot(hn, w_in_ref[...], preferred_element_type=F32)
    cos_a, sin_a = cos_a_ref[...], sin_a_ref[...]
    cos_b, sin_b = cos_b_ref[...], sin_b_ref[...]
    a_scale = A_HEAD_DIM ** -0.5 * LOG2_E
    b_scale = (B_NOPE_DIM + B_ROPE_DIM) ** -0.5 * LOG2_E
    ones = jnp.ones((x_ref.shape[0], LANES), BF16)
    q_cols = A_HEADS * A_HEAD_DIM
    kv_cols = A_KV_HEADS * A_HEAD_DIM

    for h in range(A_HEADS):
        sl = slice(h * LANES, (h + 1) * LANES)
        qa_ref[:, sl] = (_rope(_rms(y[:, sl], gq_ref[...]), cos_a, sin_a) * a_scale).astype(BF16)
    for h in range(A_KV_HEADS):
        sl = slice(h * LANES, (h + 1) * LANES)
        ka_ref[:, sl] = _rope(_rms(y[:, q_cols + h * LANES:q_cols + (h + 1) * LANES], gk_ref[...]),
                              cos_a, sin_a).astype(BF16)
    off = q_cols + kv_cols
    for h in range(A_KV_HEADS):
        va_ref[:, 2 * h * LANES:(2 * h + 1) * LANES] = y[:, off + h * LANES:off + (h + 1) * LANES].astype(BF16)
        va_ref[:, (2 * h + 1) * LANES:(2 * h + 2) * LANES] = ones
    off += kv_cols

    cq = _rms(y[:, off:off + B_Q_LORA], gcq_ref[...]).astype(BF16)
    off += B_Q_LORA
    ckv = _rms(y[:, off:off + B_KV_LORA], gckv_ref[...]).astype(BF16)
    off += B_KV_LORA
    k_rope = _rope(y[:, off:off + LANES], cos_b, sin_b).astype(BF16)

    qf = jnp.dot(cq, w_uq_ref[...], preferred_element_type=F32)
    kv = jnp.dot(ckv, w_ukv_ref[...], preferred_element_type=F32)
    for h in range(B_HEADS):
        base = h * B_QK_PAD
        qb_ref[:, base:base + LANES] = (qf[:, base:base + LANES] * b_scale).astype(BF16)
        qb_ref[:, base + LANES:base + B_QK_PAD] = (
            _rope(qf[:, base + LANES:base + B_QK_PAD], cos_b, sin_b) * b_scale).astype(BF16)
        kb_ref[:, base:base + LANES] = kv[:, h * LANES:(h + 1) * LANES].astype(BF16)
        kb_ref[:, base + LANES:base + B_QK_PAD] = k_rope
        v_col = B_HEADS * B_NOPE_DIM + h * LANES
        vb_ref[:, 2 * h * LANES:(2 * h + 1) * LANES] = kv[:, v_col:v_col + LANES].astype(BF16)
        vb_ref[:, (2 * h + 1) * LANES:(2 * h + 2) * LANES] = ones


def _mixer_inproj(x, seq, pre_g, w_in, gq, gk, gcq, w_uq, gckv, w_ukv, tabs, *, tm=512):
    t, d = x.shape
    assert seq % tm == 0 and tm % PROJ_SUB_ROWS == 0
    nblk = seq // tm
    row = lambda i: (i, 0)
    const = lambda i: (0, 0)
    pos = lambda i: (i % nblk, 0)
    full = lambda a: pl.BlockSpec(a.shape, const, pipeline_mode=pl.Buffered(1))
    widths = (A_HEADS * A_HEAD_DIM, A_KV_HEADS * A_HEAD_DIM, 2 * A_KV_HEADS * A_HEAD_DIM,
              B_HEADS * B_QK_PAD, B_HEADS * B_QK_PAD, 2 * B_HEADS * B_V_DIM)
    return pl.pallas_call(
        _inproj_kernel,
        grid=(t // tm,),
        in_specs=[pl.BlockSpec((tm, d), row), full(pre_g), full(w_in), full(gq), full(gk), full(gcq),
                  full(w_uq), full(gckv), full(w_ukv)] + [pl.BlockSpec((tm, LANES), pos)] * 4,
        out_specs=[pl.BlockSpec((tm, w), row) for w in widths],
        out_shape=[jax.ShapeDtypeStruct((t, w), BF16) for w in widths],
        compiler_params=_cparams("parallel"),
        name="mixer_inproj",
    )(x, pre_g, w_in, gq, gk, gcq, w_uq, gckv, w_ukv, *tabs)


def _attn_kernel(q_ref, k_ref, v_ref, o_ref, *, group, dk, dv, tk, rb):
    tq = q_ref.shape[0]
    seq = k_ref.shape[0]
    per_head = tq // rb
    nblocks = group * per_head

    def q_block(i):
        j, r = divmod(i, per_head)
        return q_ref[r * rb:(r + 1) * rb, j * dk:(j + 1) * dk]

    def body(c, carry):
        ms, accs = carry
        start = pl.multiple_of(c * tk, tk)
        k = k_ref[pl.ds(start, tk), :]
        v = v_ref[pl.ds(start, tk), :]
        new_ms, new_accs = [], []
        for i in range(nblocks):
            s = lax.dot_general(q_block(i), k, (((1,), (1,)), ((), ())), preferred_element_type=F32)
            m_new = jnp.maximum(ms[i], jnp.max(s, axis=-1, keepdims=True))
            alpha = jnp.exp2(ms[i] - m_new)
            p = jnp.exp2(s - m_new).astype(BF16)
            new_accs.append(alpha * accs[i] + jnp.dot(p, v, preferred_element_type=F32))
            new_ms.append(m_new)
        return tuple(new_ms), tuple(new_accs)

    init = (tuple(jnp.full((rb, 1), -jnp.inf, F32) for _ in range(nblocks)),
            tuple(jnp.zeros((rb, 2 * dv), F32) for _ in range(nblocks)))
    _, accs = lax.fori_loop(0, seq // tk, body, init, unroll=ATTN_CHUNK_UNROLL)
    for i in range(nblocks):
        j, r = divmod(i, per_head)
        out = accs[i][:, :dv] * (1.0 / accs[i][:, dv:])
        o_ref[r * rb:(r + 1) * rb, j * dv:(j + 1) * dv] = out.astype(o_ref.dtype)


def _attention(q, k, v, *, kv_heads, group, dk, dv, tq, tk=512, rb=2048):
    b, seq, _ = q.shape
    tq, tk = min(tq, seq), min(tk, seq)
    rb = min(rb, tq)
    assert seq % tq == 0 and seq % tk == 0 and tq % rb == 0
    kern = functools.partial(_attn_kernel, group=group, dk=dk, dv=dv, tk=tk, rb=rb)
    return pl.pallas_call(
        kern,
        grid=(b, kv_heads, seq // tq),
        in_specs=[
            pl.BlockSpec((None, tq, group * dk), lambda bi, h, i: (bi, i, h)),
            pl.BlockSpec((None, seq, dk), lambda bi, h, i: (bi, 0, h)),
            pl.BlockSpec((None, seq, 2 * dv), lambda bi, h, i: (bi, 0, h)),
        ],
        out_specs=pl.BlockSpec((None, tq, group * dv), lambda bi, h, i: (bi, i, h)),
        out_shape=jax.ShapeDtypeStruct((b, seq, kv_heads * group * dv), BF16),
        compiler_params=_cparams("parallel", "parallel", "arbitrary"),
        name=f"attention_g{group}_dk{dk}",
    )(q, k, v)


def _outproj_kernel(x_ref, oa_ref, ob_ref, wa_ref, wb_ref, post_g_ref, o_ref):
    for r in range(x_ref.shape[0] // PROJ_SUB_ROWS):
        rows = slice(r * PROJ_SUB_ROWS, (r + 1) * PROJ_SUB_ROWS)
        m = jnp.dot(oa_ref[rows, :], wa_ref[...], preferred_element_type=F32)
        m += jnp.dot(ob_ref[rows, :], wb_ref[...], preferred_element_type=F32)
        o_ref[rows, :] = x_ref[rows, :] + _rms(m, post_g_ref[...])


def _mixer_outproj(x, oa, ob, wa, wb, post_g, *, tm=512):
    t, d = x.shape
    row = lambda i: (i, 0)
    const = lambda i: (0, 0)
    return pl.pallas_call(
        _outproj_kernel,
        grid=(t // tm,),
        in_specs=[pl.BlockSpec((tm, d), row), pl.BlockSpec((tm, oa.shape[1]), row),
                  pl.BlockSpec((tm, ob.shape[1]), row),
                  pl.BlockSpec(wa.shape, const, pipeline_mode=pl.Buffered(1)),
                  pl.BlockSpec(wb.shape, const, pipeline_mode=pl.Buffered(1)), pl.BlockSpec((1, d), const)],
        out_specs=pl.BlockSpec((tm, d), row),
        out_shape=jax.ShapeDtypeStruct((t, d), F32),
        compiler_params=_cparams("parallel"),
        name="mixer_outproj",
    )(x, oa, ob, wa, wb, post_g)


def _pool_kernel(x_ref, prev_ref, next_ref, pre_g_ref, w_ref, scale_ref, post_g_ref, o_ref, ext_ref, m_ref,
                 *, seq):
    tm, d = x_ref.shape
    group = d // len(POOL_WINDOWS)
    nblk = seq // tm
    blk = pl.program_id(0) % nblk
    g = pre_g_ref[...]
    x = x_ref[...]
    ext_ref[0:POOL_HALO, :] = jnp.where(blk > 0, _rms(prev_ref[...], g), 0.0)
    ext_ref[POOL_HALO:POOL_HALO + tm, :] = _rms(x, g)
    ext_ref[POOL_HALO + tm:, :] = jnp.where(blk < nblk - 1, _rms(next_ref[...], g), 0.0)

    t = blk * tm + lax.broadcasted_iota(jnp.int32, (tm, 1), 0)
    n_ext = tm + 2 * POOL_HALO
    for gi, w in enumerate(POOL_WINDOWS):
        cols = slice(gi * group, (gi + 1) * group)
        half = w // 2
        run = ext_ref[:, cols]
        span = 1
        while span < w:
            run = run + pltpu.roll(run, span, 0)
            span *= 2
        lead = half - 1
        if lead:
            run = pltpu.roll(run, n_ext - lead, 0)
        tot = run[POOL_HALO:POOL_HALO + tm]
        cnt = (jnp.minimum(t + half, seq) - jnp.maximum(t - half, 0)).astype(F32)
        pooled = tot * (1.0 / cnt) - ext_ref[POOL_HALO:POOL_HALO + tm, cols]
        m_ref[:, cols] = jnp.dot(pooled.astype(BF16), w_ref[gi], preferred_element_type=F32)
    o_ref[...] = x + _rms(m_ref[...] * scale_ref[...], post_g_ref[...])


def _pool_mixer(x, seq, pre_g, pool_w, pool_scale, post_g, *, tm=512):
    t, d = x.shape
    assert seq % tm == 0 and max(POOL_WINDOWS) // 2 <= POOL_HALO
    hb = tm // POOL_HALO
    last = t // POOL_HALO - 1
    row = lambda i: (i, 0)
    const = lambda i: (0, 0)
    return pl.pallas_call(
        functools.partial(_pool_kernel, seq=seq),
        grid=(t // tm,),
        in_specs=[
            pl.BlockSpec((tm, d), row),
            pl.BlockSpec((POOL_HALO, d), lambda i: (jnp.maximum(i * hb - 1, 0), 0)),
            pl.BlockSpec((POOL_HALO, d), lambda i: (jnp.minimum((i + 1) * hb, last), 0)),
            pl.BlockSpec((1, d), const),
            pl.BlockSpec(pool_w.shape, lambda i: (0, 0, 0)),
            pl.BlockSpec((1, d), const),
            pl.BlockSpec((1, d), const),
        ],
        out_specs=pl.BlockSpec((tm, d), row),
        out_shape=jax.ShapeDtypeStruct((t, d), F32),
        scratch_shapes=[pltpu.VMEM((tm + 2 * POOL_HALO, d), F32), pltpu.VMEM((tm, d), F32)],
        compiler_params=_cparams("parallel"),
        name="pool_mixer",
    )(x, x, x, pre_g, pool_w, pool_scale, post_g)


def _rope_tables(seq):
    rows = seq // GRID_W
    row = jnp.repeat(jnp.arange(rows, dtype=F32), GRID_W)
    col = jnp.tile(jnp.arange(GRID_W, dtype=F32), rows)

    def angles(rot_dim):
        half = rot_dim // 2
        freqs = ROPE_BASE ** (-jnp.arange(0, half, 2, dtype=F32) / half)
        return jnp.concatenate([row[:, None] * freqs, col[:, None] * freqs], axis=-1)

    ang_a = angles(A_HEAD_DIM)
    cos_a = jnp.concatenate([jnp.cos(ang_a)] * 2, axis=-1)
    sin_a = jnp.concatenate([-jnp.sin(ang_a), jnp.sin(ang_a)], axis=-1)
    ang_b = angles(B_ROPE_DIM)
    z = jnp.zeros_like(ang_b)
    cos_b = jnp.concatenate([jnp.cos(ang_b), z, jnp.cos(ang_b), z], axis=-1)
    sin_b = jnp.concatenate([-jnp.sin(ang_b), z, jnp.sin(ang_b), z], axis=-1)
    return cos_a, sin_a, cos_b, sin_b


def _pair_split_perm(n):
    return np.concatenate([np.arange(0, n, 2), np.arange(1, n, 2)])


def _spread_rope_cols(w):
    half = B_ROPE_DIM // 2
    z = jnp.zeros((w.shape[0], half), w.dtype)
    return jnp.concatenate([w[:, 0::2], z, w[:, 1::2], z], axis=-1)


def _prep_mixer_weights(w_in, gq, gk, w_uq, w_ukv, w_out):
    d = w_in.shape[0]
    w_in, w_uq, w_ukv = w_in.astype(BF16), w_uq.astype(BF16), w_ukv.astype(BF16)
    perm = _pair_split_perm(A_HEAD_DIM)
    nq, nkv = A_HEADS * A_HEAD_DIM, A_KV_HEADS * A_HEAD_DIM
    wq = w_in[:, :nq].reshape(d, A_HEADS, A_HEAD_DIM)[:, :, perm].reshape(d, nq)
    wk = w_in[:, nq:nq + nkv].reshape(d, A_KV_HEADS, A_HEAD_DIM)[:, :, perm].reshape(d, nkv)
    rest = w_in[:, nq + nkv:-B_ROPE_DIM]
    w_in_p = jnp.concatenate([wq, wk, rest, _spread_rope_cols(w_in[:, -B_ROPE_DIM:])], axis=-1)

    uq = w_uq.reshape(B_Q_LORA, B_HEADS, B_NOPE_DIM + B_ROPE_DIM)
    uq_rope = _spread_rope_cols(uq[:, :, B_NOPE_DIM:].reshape(B_Q_LORA * B_HEADS, B_ROPE_DIM))
    uq_p = jnp.concatenate([uq[:, :, :B_NOPE_DIM], uq_rope.reshape(B_Q_LORA, B_HEADS, LANES)], axis=-1)
    uq_p = uq_p.reshape(B_Q_LORA, B_HEADS * B_QK_PAD)

    ukv = w_ukv.reshape(B_KV_LORA, B_HEADS, B_NOPE_DIM + B_V_DIM)
    ukv_p = jnp.concatenate([ukv[:, :, :B_NOPE_DIM].reshape(B_KV_LORA, -1),
                             ukv[:, :, B_NOPE_DIM:].reshape(B_KV_LORA, -1)], axis=-1)
    na = A_HEADS * A_HEAD_DIM
    return (w_in_p, gq[perm][None, :], gk[perm][None, :], uq_p, ukv_p,
            w_out[:na].astype(BF16), w_out[na:].astype(BF16))


def _trunk(x, ffn_params, mix0, mix1, tabs):
    b, seq, d = x.shape
    x = x.reshape(b * seq, d)
    (pre0, w_in, gq, gk, gcq, w_uq, gckv, w_ukv, wa, wb, post0) = mix0
    (pre1, pool_w, pool_scale, post1) = mix1

    x = _ffn_half(x, *ffn_params[0])
    qa, ka, va, qb, kb, vb = _mixer_inproj(x, seq, pre0, w_in, gq, gk, gcq, w_uq, gckv, w_ukv, tabs)
    shp = lambda a: a.reshape(b, seq, a.shape[-1])
    oa = _attention(shp(qa), shp(ka), shp(va), kv_heads=A_KV_HEADS, group=A_HEADS // A_KV_HEADS,
                    dk=A_HEAD_DIM, dv=A_HEAD_DIM, tq=512)
    ob = _attention(shp(qb), shp(kb), shp(vb), kv_heads=B_HEADS, group=1, dk=B_QK_PAD, dv=B_V_DIM, tq=2048)
    x = _mixer_outproj(x, oa.reshape(b * seq, -1), ob.reshape(b * seq, -1), wa, wb, post0)
    x = _ffn_half(x, *ffn_params[1])
    x = _ffn_half(x, *ffn_params[2])
    x = _pool_mixer(x, seq, pre1, pool_w, pool_scale, post1)
    x = _ffn_half(x, *ffn_params[3])
    return x.reshape(b, seq, d)


def kernel(x_prompt, x_sample, l0_ffn1_pre_g, l0_ffn1_w_gate, l0_ffn1_w_up, l0_ffn1_w_down, l0_ffn1_post_g, l0_mix_pre_g, l0_w_in, l0_a_q_norm_g, l0_a_k_norm_g, l0_b_cq_norm_g, l0_b_w_uq, l0_b_ckv_norm_g, l0_b_w_ukv, l0_w_out, l0_mix_post_g, l0_ffn2_pre_g, l0_ffn2_w_gate, l0_ffn2_w_up, l0_ffn2_w_down, l0_ffn2_post_g, l1_ffn1_pre_g, l1_ffn1_w_gate, l1_ffn1_w_up, l1_ffn1_w_down, l1_ffn1_post_g, l1_mix_pre_g, l1_pool_w, l1_pool_scale, l1_mix_post_g, l1_ffn2_pre_g, l1_ffn2_w_gate, l1_ffn2_w_up, l1_ffn2_w_down, l1_ffn2_post_g):
    vec = lambda g: g[None, :]
    ffn = lambda pre, wg, wu, wd, post: (vec(pre), wg.astype(BF16), wu.astype(BF16), wd.astype(BF16), vec(post))
    ffn_params = (
        ffn(l0_ffn1_pre_g, l0_ffn1_w_gate, l0_ffn1_w_up, l0_ffn1_w_down, l0_ffn1_post_g),
        ffn(l0_ffn2_pre_g, l0_ffn2_w_gate, l0_ffn2_w_up, l0_ffn2_w_down, l0_ffn2_post_g),
        ffn(l1_ffn1_pre_g, l1_ffn1_w_gate, l1_ffn1_w_up, l1_ffn1_w_down, l1_ffn1_post_g),
        ffn(l1_ffn2_pre_g, l1_ffn2_w_gate, l1_ffn2_w_up, l1_ffn2_w_down, l1_ffn2_post_g),
    )
    w_in, gq, gk, w_uq, w_ukv, wa, wb = _prep_mixer_weights(
        l0_w_in, l0_a_q_norm_g, l0_a_k_norm_g, l0_b_w_uq, l0_b_w_ukv, l0_w_out)
    mix0 = (vec(l0_mix_pre_g), w_in, gq, gk, vec(l0_b_cq_norm_g), w_uq, vec(l0_b_ckv_norm_g), w_ukv,
            wa, wb, vec(l0_mix_post_g))
    mix1 = (vec(l1_mix_pre_g), l1_pool_w.astype(BF16), vec(l1_pool_scale), vec(l1_mix_post_g))

    outs = []
    for x in (x_prompt, x_sample):
        tabs = _rope_tables(x.shape[1])
        outs.append(_trunk(x, ffn_params, mix0, mix1, tabs))
    return tuple(outs)
```

```python
import functools
import math

import jax
import jax.numpy as jnp
import numpy as np
from jax import lax
from jax.experimental import pallas as pl
from jax.experimental.pallas import tpu as pltpu

F32 = jnp.float32
BF16 = jnp.bfloat16

NORM_EPS = 1e-6
ROPE_BASE = 10000.0
GRID_W = 64
A_HEADS = 8
A_KV_HEADS = 2
A_HEAD_DIM = 128
B_HEADS = 8
B_Q_LORA = 512
B_KV_LORA = 256
B_NOPE_DIM = 128
B_ROPE_DIM = 64
B_V_DIM = 128
POOL_WINDOWS = (2, 4, 8, 16)
LANES = 128
LOG2_E = math.log2(math.e)
ATTN_CHUNK_UNROLL = 16
B_QK_PAD = 2 * LANES
ROW_CHUNK = 32
PROJ_SUB_ROWS = 512
FFN_SUB_ROWS = 512
POOL_HALO = 8

VMEM_LIMIT_BYTES = 60 * 1024 * 1024


def _cparams(*semantics):
    return pltpu.CompilerParams(dimension_semantics=semantics, vmem_limit_bytes=VMEM_LIMIT_BYTES)


def _rms(x, g):
    return x * lax.rsqrt(jnp.mean(x * x, axis=-1, keepdims=True) + NORM_EPS) * g


def _silu(x):
    return x * (1.0 / (1.0 + jnp.exp(-x)))


def _ffn_step(x_ref, pre_g_ref, wg_ref, wu_ref, wd_ref, post_g_ref, o_ref, h_ref, *, first, last):
    tm = x_ref.shape[0]
    sub = FFN_SUB_ROWS if (first or last) else tm
    for r in range(tm // sub):
        base = r * sub
        rows = slice(base, base + sub)
        if first:
            for c in range(sub // ROW_CHUNK):
                chunk = slice(base + c * ROW_CHUNK, base + (c + 1) * ROW_CHUNK)
                h_ref[chunk, :] = _rms(x_ref[chunk, :], pre_g_ref[...]).astype(BF16)
        h = h_ref[rows, :]
        gate = jnp.dot(h, wg_ref[...], preferred_element_type=F32)
        up = jnp.dot(h, wu_ref[...], preferred_element_type=F32)
        act = (_silu(gate) * up).astype(BF16)
        down = jnp.dot(act, wd_ref[...], preferred_element_type=F32)
        if first:
            o_ref[rows, :] = down
        else:
            o_ref[rows, :] += down
        if last:
            for c in range(sub // ROW_CHUNK):
                chunk = slice(base + c * ROW_CHUNK, base + (c + 1) * ROW_CHUNK)
                o_ref[chunk, :] = x_ref[chunk, :] + 0.5 * _rms(o_ref[chunk, :], post_g_ref[...])


def _ffn_kernel(*refs):
    f = pl.program_id(1)
    nf = pl.num_programs(1)
    pl.when(f == 0)(functools.partial(_ffn_step, *refs, first=True, last=False))
    pl.when(jnp.logical_and(f > 0, f < nf - 1))(functools.partial(_ffn_step, *refs, first=False, last=False))
    pl.when(f == nf - 1)(functools.partial(_ffn_step, *refs, first=False, last=True))


def _ffn_half(x, pre_g, wg, wu, wd, post_g, *, tm=1024, tf=512):
    t, d = x.shape
    dff = wg.shape[1]
    assert t % tm == 0 and dff % tf == 0 and dff // tf >= 2 and tm % FFN_SUB_ROWS == 0
    row = lambda i, f: (i, 0)
    const = lambda i, f: (0, 0)
    return pl.pallas_call(
        _ffn_kernel,
        grid=(t // tm, dff // tf),
        in_specs=[
            pl.BlockSpec((tm, d), row),
            pl.BlockSpec((1, d), const),
            pl.BlockSpec((d, tf), lambda i, f: (0, f)),
            pl.BlockSpec((d, tf), lambda i, f: (0, f)),
            pl.BlockSpec((tf, d), lambda i, f: (f, 0)),
            pl.BlockSpec((1, d), const),
        ],
        out_specs=pl.BlockSpec((tm, d), row),
        out_shape=jax.ShapeDtypeStruct((t, d), F32),
        scratch_shapes=[pltpu.VMEM((tm, d), BF16)],
        compiler_params=_cparams("parallel", "arbitrary"),
        name="ffn_half",
    )(x, pre_g, wg, wu, wd, post_g)


def _rope(x, cos, sin):
    return x * cos + pltpu.roll(x, LANES // 2, 1) * sin


def _inproj_kernel(x_ref, pre_g_ref, w_in_ref, gq_ref, gk_ref, gcq_ref, w_uq_ref, gckv_ref, w_ukv_ref,
                   cos_a_ref, sin_a_ref, cos_b_ref, sin_b_ref, *out_refs):
    for r in range(x_ref.shape[0] // PROJ_SUB_ROWS):
        rows = pl.ds(r * PROJ_SUB_ROWS, PROJ_SUB_ROWS)
        at = lambda ref: ref.at[rows]
        _inproj_rows(at(x_ref), pre_g_ref, w_in_ref, gq_ref, gk_ref, gcq_ref, w_uq_ref, gckv_ref, w_ukv_ref,
                     at(cos_a_ref), at(sin_a_ref), at(cos_b_ref), at(sin_b_ref), *map(at, out_refs))


def _inproj_rows(x_ref, pre_g_ref, w_in_ref, gq_ref, gk_ref, gcq_ref, w_uq_ref, gckv_ref, w_ukv_ref,
                 cos_a_ref, sin_a_ref, cos_b_ref, sin_b_ref,
                 qa_ref, ka_ref, va_ref, qb_ref, kb_ref, vb_ref):
    hn = _rms(x_ref[...], pre_g_ref[...]).astype(BF16)
    y = jnp.dot(hn, w_in_ref[...], preferred_element_type=F32)
    cos_a, sin_a = cos_a_ref[...], sin_a_ref[...]
    cos_b, sin_b = cos_b_ref[...], sin_b_ref[...]
    a_scale = A_HEAD_DIM ** -0.5 * LOG2_E
    b_scale = (B_NOPE_DIM + B_ROPE_DIM) ** -0.5 * LOG2_E
    ones = jnp.ones((x_ref.shape[0], LANES), BF16)
    q_cols = A_HEADS * A_HEAD_DIM
    kv_cols = A_KV_HEADS * A_HEAD_DIM

    for h in range(A_HEADS):
        sl = slice(h * LANES, (h + 1) * LANES)
        qa_ref[:, sl] = (_rope(_rms(y[:, sl], gq_ref[...]), cos_a, sin_a) * a_scale).astype(BF16)
    for h in range(A_KV_HEADS):
        sl = slice(h * LANES, (h + 1) * LANES)
        ka_ref[:, sl] = _rope(_rms(y[:, q_cols + h * LANES:q_cols + (h + 1) * LANES], gk_ref[...]),
                              cos_a, sin_a).astype(BF16)
    off = q_cols + kv_cols
    for h in range(A_KV_HEADS):
        va_ref[:, 2 * h * LANES:(2 * h + 1) * LANES] = y[:, off + h * LANES:off + (h + 1) * LANES].astype(BF16)
        va_ref[:, (2 * h + 1) * LANES:(2 * h + 2) * LANES] = ones
    off += kv_cols

    cq = _rms(y[:, off:off + B_Q_LORA], gcq_ref[...]).astype(BF16)
    off += B_Q_LORA
    ckv = _rms(y[:, off:off + B_KV_LORA], gckv_ref[...]).astype(BF16)
    off += B_KV_LORA
    k_rope = _rope(y[:, off:off + LANES], cos_b, sin_b).astype(BF16)

    qf = jnp.dot(cq, w_uq_ref[...], preferred_element_type=F32)
    kv = jnp.dot(ckv, w_ukv_ref[...], preferred_element_type=F32)
    for h in range(B_HEADS):
        base = h * B_QK_PAD
        qb_ref[:, base:base + LANES] = (qf[:, base:base + LANES] * b_scale).astype(BF16)
        qb_ref[:, base + LANES:base + B_QK_PAD] = (
            _rope(qf[:, base + LANES:base + B_QK_PAD], cos_b, sin_b) * b_scale).astype(BF16)
        kb_ref[:, base:base + LANES] = kv[:, h * LANES:(h + 1) * LANES].astype(BF16)
        kb_ref[:, base + LANES:base + B_QK_PAD] = k_rope
        v_col = B_HEADS * B_NOPE_DIM + h * LANES
        vb_ref[:, 2 * h * LANES:(2 * h + 1) * LANES] = kv[:, v_col:v_col + LANES].astype(BF16)
        vb_ref[:, (2 * h + 1) * LANES:(2 * h + 2) * LANES] = ones


def _mixer_inproj(x, seq, pre_g, w_in, gq, gk, gcq, w_uq, gckv, w_ukv, tabs, *, tm=512):
    t, d = x.shape
    assert seq % tm == 0 and tm % PROJ_SUB_ROWS == 0
    nblk = seq // tm
    row = lambda i: (i, 0)
    const = lambda i: (0, 0)
    pos = lambda i: (i % nblk, 0)
    full = lambda a: pl.BlockSpec(a.shape, const, pipeline_mode=pl.Buffered(1))
    widths = (A_HEADS * A_HEAD_DIM, A_KV_HEADS * A_HEAD_DIM, 2 * A_KV_HEADS * A_HEAD_DIM,
              B_HEADS * B_QK_PAD, B_HEADS * B_QK_PAD, 2 * B_HEADS * B_V_DIM)
    return pl.pallas_call(
        _inproj_kernel,
        grid=(t // tm,),
        in_specs=[pl.BlockSpec((tm, d), row), full(pre_g), full(w_in), full(gq), full(gk), full(gcq),
                  full(w_uq), full(gckv), full(w_ukv)] + [pl.BlockSpec((tm, LANES), pos)] * 4,
        out_specs=[pl.BlockSpec((tm, w), row) for w in widths],
        out_shape=[jax.ShapeDtypeStruct((t, w), BF16) for w in widths],
        compiler_params=_cparams("parallel"),
        name="mixer_inproj",
    )(x, pre_g, w_in, gq, gk, gcq, w_uq, gckv, w_ukv, *tabs)


def _attn_kernel(q_ref, k_ref, v_ref, o_ref, *scratch, group, dk, dv, tk, rb):
    tq = q_ref.shape[0]
    seq = k_ref.shape[0]
    nblocks = group * tq // rb
    if group > 1:
        (qs_ref,) = scratch
        for j in range(group):
            qs_ref[j * tq:(j + 1) * tq, :] = q_ref[:, j * dk:(j + 1) * dk]
    else:
        qs_ref = q_ref

    def body(c, carry):
        ms, accs = carry
        start = pl.multiple_of(c * tk, tk)
        k = k_ref[pl.ds(start, tk), :]
        v = v_ref[pl.ds(start, tk), :]
        new_ms, new_accs = [], []
        for i in range(nblocks):
            q = qs_ref[i * rb:(i + 1) * rb, :]
            s = lax.dot_general(q, k, (((1,), (1,)), ((), ())), preferred_element_type=F32)
            m_new = jnp.maximum(ms[i], jnp.max(s, axis=-1, keepdims=True))
            alpha = jnp.exp2(ms[i] - m_new)
            p = jnp.exp2(s - m_new).astype(BF16)
            new_accs.append(alpha * accs[i] + jnp.dot(p, v, preferred_element_type=F32))
            new_ms.append(m_new)
        return tuple(new_ms), tuple(new_accs)

    init = (tuple(jnp.full((rb, 1), -jnp.inf, F32) for _ in range(nblocks)),
            tuple(jnp.zeros((rb, 2 * dv), F32) for _ in range(nblocks)))
    _, accs = lax.fori_loop(0, seq // tk, body, init, unroll=ATTN_CHUNK_UNROLL)
    piece = min(rb, tq)
    for i in range(nblocks):
        out = (accs[i][:, :dv] * (1.0 / accs[i][:, dv:])).astype(o_ref.dtype)
        for n in range(rb // piece):
            j, r = divmod(i * rb + n * piece, tq)
            o_ref[r:r + piece, j * dv:(j + 1) * dv] = out[n * piece:(n + 1) * piece]


def _attention(q, k, v, *, kv_heads, group, dk, dv, tq, tk=512, rb=1024):
    b, seq, _ = q.shape
    tq, tk = min(tq, seq), min(tk, seq)
    rb = min(rb, group * tq)
    assert seq % tq == 0 and seq % tk == 0 and (group * tq) % rb == 0 and (rb % tq == 0 or tq % rb == 0)
    kern = functools.partial(_attn_kernel, group=group, dk=dk, dv=dv, tk=tk, rb=rb)
    return pl.pallas_call(
        kern,
        grid=(b, kv_heads, seq // tq),
        in_specs=[
            pl.BlockSpec((None, tq, group * dk), lambda bi, h, i: (bi, i, h)),
            pl.BlockSpec((None, seq, dk), lambda bi, h, i: (bi, 0, h)),
            pl.BlockSpec((None, seq, 2 * dv), lambda bi, h, i: (bi, 0, h)),
        ],
        out_specs=pl.BlockSpec((None, tq, group * dv), lambda bi, h, i: (bi, i, h)),
        out_shape=jax.ShapeDtypeStruct((b, seq, kv_heads * group * dv), BF16),
        scratch_shapes=[pltpu.VMEM((group * tq, dk), BF16)] if group > 1 else [],
        compiler_params=_cparams("parallel", "parallel", "arbitrary"),
        name=f"attention_g{group}_dk{dk}",
    )(q, k, v)


def _outproj_kernel(x_ref, oa_ref, ob_ref, wa_ref, wb_ref, post_g_ref, o_ref):
    for r in range(x_ref.shape[0] // PROJ_SUB_ROWS):
        rows = slice(r * PROJ_SUB_ROWS, (r + 1) * PROJ_SUB_ROWS)
        m = jnp.dot(oa_ref[rows, :], wa_ref[...], preferred_element_type=F32)
        m += jnp.dot(ob_ref[rows, :], wb_ref[...], preferred_element_type=F32)
        o_ref[rows, :] = x_ref[rows, :] + _rms(m, post_g_ref[...])


def _mixer_outproj(x, oa, ob, wa, wb, post_g, *, tm=512):
    t, d = x.shape
    row = lambda i: (i, 0)
    const = lambda i: (0, 0)
    return pl.pallas_call(
        _outproj_kernel,
        grid=(t // tm,),
        in_specs=[pl.BlockSpec((tm, d), row), pl.BlockSpec((tm, oa.shape[1]), row),
                  pl.BlockSpec((tm, ob.shape[1]), row),
                  pl.BlockSpec(wa.shape, const, pipeline_mode=pl.Buffered(1)),
                  pl.BlockSpec(wb.shape, const, pipeline_mode=pl.Buffered(1)), pl.BlockSpec((1, d), const)],
        out_specs=pl.BlockSpec((tm, d), row),
        out_shape=jax.ShapeDtypeStruct((t, d), F32),
        compiler_params=_cparams("parallel"),
        name="mixer_outproj",
    )(x, oa, ob, wa, wb, post_g)


def _pool_kernel(x_ref, prev_ref, next_ref, pre_g_ref, w_ref, scale_ref, post_g_ref, o_ref, ext_ref, m_ref,
                 *, seq):
    tm, d = x_ref.shape
    group = d // len(POOL_WINDOWS)
    nblk = seq // tm
    blk = pl.program_id(0) % nblk
    g = pre_g_ref[...]
    x = x_ref[...]
    ext_ref[0:POOL_HALO, :] = jnp.where(blk > 0, _rms(prev_ref[...], g), 0.0)
    ext_ref[POOL_HALO:POOL_HALO + tm, :] = _rms(x, g)
    ext_ref[POOL_HALO + tm:, :] = jnp.where(blk < nblk - 1, _rms(next_ref[...], g), 0.0)

    t = blk * tm + lax.broadcasted_iota(jnp.int32, (tm, 1), 0)
    n_ext = tm + 2 * POOL_HALO
    for gi, w in enumerate(POOL_WINDOWS):
        cols = slice(gi * group, (gi + 1) * group)
        half = w // 2
        run = ext_ref[:, cols]
        span = 1
        while span < w:
            run = run + pltpu.roll(run, span, 0)
            span *= 2
        lead = half - 1
        if lead:
            run = pltpu.roll(run, n_ext - lead, 0)
        tot = run[POOL_HALO:POOL_HALO + tm]
        cnt = (jnp.minimum(t + half, seq) - jnp.maximum(t - half, 0)).astype(F32)
        pooled = tot * (1.0 / cnt) - ext_ref[POOL_HALO:POOL_HALO + tm, cols]
        m_ref[:, cols] = jnp.dot(pooled.astype(BF16), w_ref[gi], preferred_element_type=F32)
    o_ref[...] = x + _rms(m_ref[...] * scale_ref[...], post_g_ref[...])


def _pool_mixer(x, seq, pre_g, pool_w, pool_scale, post_g, *, tm=512):
    t, d = x.shape
    assert seq % tm == 0 and max(POOL_WINDOWS) // 2 <= POOL_HALO
    hb = tm // POOL_HALO
    last = t // POOL_HALO - 1
    row = lambda i: (i, 0)
    const = lambda i: (0, 0)
    return pl.pallas_call(
        functools.partial(_pool_kernel, seq=seq),
        grid=(t // tm,),
        in_specs=[
            pl.BlockSpec((tm, d), row),
            pl.BlockSpec((POOL_HALO, d), lambda i: (jnp.maximum(i * hb - 1, 0), 0)),
            pl.BlockSpec((POOL_HALO, d), lambda i: (jnp.minimum((i + 1) * hb, last), 0)),
            pl.BlockSpec((1, d), const),
            pl.BlockSpec(pool_w.shape, lambda i: (0, 0, 0)),
            pl.BlockSpec((1, d), const),
            pl.BlockSpec((1, d), const),
        ],
        out_specs=pl.BlockSpec((tm, d), row),
        out_shape=jax.ShapeDtypeStruct((t, d), F32),
        scratch_shapes=[pltpu.VMEM((tm + 2 * POOL_HALO, d), F32), pltpu.VMEM((tm, d), F32)],
        compiler_params=_cparams("parallel"),
        name="pool_mixer",
    )(x, x, x, pre_g, pool_w, pool_scale, post_g)


def _rope_tables(seq):
    rows = seq // GRID_W
    row = jnp.repeat(jnp.arange(rows, dtype=F32), GRID_W)
    col = jnp.tile(jnp.arange(GRID_W, dtype=F32), rows)

    def angles(rot_dim):
        half = rot_dim // 2
        freqs = ROPE_BASE ** (-jnp.arange(0, half, 2, dtype=F32) / half)
        return jnp.concatenate([row[:, None] * freqs, col[:, None] * freqs], axis=-1)

    ang_a = angles(A_HEAD_DIM)
    cos_a = jnp.concatenate([jnp.cos(ang_a)] * 2, axis=-1)
    sin_a = jnp.concatenate([-jnp.sin(ang_a), jnp.sin(ang_a)], axis=-1)
    ang_b = angles(B_ROPE_DIM)
    z = jnp.zeros_like(ang_b)
    cos_b = jnp.concatenate([jnp.cos(ang_b), z, jnp.cos(ang_b), z], axis=-1)
    sin_b = jnp.concatenate([-jnp.sin(ang_b), z, jnp.sin(ang_b), z], axis=-1)
    return cos_a, sin_a, cos_b, sin_b


def _pair_split_perm(n):
    return np.concatenate([np.arange(0, n, 2), np.arange(1, n, 2)])


def _spread_rope_cols(w):
    half = B_ROPE_DIM // 2
    z = jnp.zeros((w.shape[0], half), w.dtype)
    return jnp.concatenate([w[:, 0::2], z, w[:, 1::2], z], axis=-1)


def _prep_mixer_weights(w_in, gq, gk, w_uq, w_ukv, w_out):
    d = w_in.shape[0]
    w_in, w_uq, w_ukv = w_in.astype(BF16), w_uq.astype(BF16), w_ukv.astype(BF16)
    perm = _pair_split_perm(A_HEAD_DIM)
    nq, nkv = A_HEADS * A_HEAD_DIM, A_KV_HEADS * A_HEAD_DIM
    wq = w_in[:, :nq].reshape(d, A_HEADS, A_HEAD_DIM)[:, :, perm].reshape(d, nq)
    wk = w_in[:, nq:nq + nkv].reshape(d, A_KV_HEADS, A_HEAD_DIM)[:, :, perm].reshape(d, nkv)
    rest = w_in[:, nq + nkv:-B_ROPE_DIM]
    w_in_p = jnp.concatenate([wq, wk, rest, _spread_rope_cols(w_in[:, -B_ROPE_DIM:])], axis=-1)

    uq = w_uq.reshape(B_Q_LORA, B_HEADS, B_NOPE_DIM + B_ROPE_DIM)
    uq_rope = _spread_rope_cols(uq[:, :, B_NOPE_DIM:].reshape(B_Q_LORA * B_HEADS, B_ROPE_DIM))
    uq_p = jnp.concatenate([uq[:, :, :B_NOPE_DIM], uq_rope.reshape(B_Q_LORA, B_HEADS, LANES)], axis=-1)
    uq_p = uq_p.reshape(B_Q_LORA, B_HEADS * B_QK_PAD)

    ukv = w_ukv.reshape(B_KV_LORA, B_HEADS, B_NOPE_DIM + B_V_DIM)
    ukv_p = jnp.concatenate([ukv[:, :, :B_NOPE_DIM].reshape(B_KV_LORA, -1),
                             ukv[:, :, B_NOPE_DIM:].reshape(B_KV_LORA, -1)], axis=-1)
    na = A_HEADS * A_HEAD_DIM
    return (w_in_p, gq[perm][None, :], gk[perm][None, :], uq_p, ukv_p,
            w_out[:na].astype(BF16), w_out[na:].astype(BF16))


def _trunk(x, ffn_params, mix0, mix1, tabs):
    b, seq, d = x.shape
    x = x.reshape(b * seq, d)
    (pre0, w_in, gq, gk, gcq, w_uq, gckv, w_ukv, wa, wb, post0) = mix0
    (pre1, pool_w, pool_scale, post1) = mix1

    x = _ffn_half(x, *ffn_params[0])
    qa, ka, va, qb, kb, vb = _mixer_inproj(x, seq, pre0, w_in, gq, gk, gcq, w_uq, gckv, w_ukv, tabs)
    shp = lambda a: a.reshape(b, seq, a.shape[-1])
    oa = _attention(shp(qa), shp(ka), shp(va), kv_heads=A_KV_HEADS, group=A_HEADS // A_KV_HEADS,
                    dk=A_HEAD_DIM, dv=A_HEAD_DIM, tq=512)
    ob = _attention(shp(qb), shp(kb), shp(vb), kv_heads=B_HEADS, group=1, dk=B_QK_PAD, dv=B_V_DIM, tq=2048)
    x = _mixer_outproj(x, oa.reshape(b * seq, -1), ob.reshape(b * seq, -1), wa, wb, post0)
    x = _ffn_half(x, *ffn_params[1])
    x = _ffn_half(x, *ffn_params[2])
    x = _pool_mixer(x, seq, pre1, pool_w, pool_scale, post1)
    x = _ffn_half(x, *ffn_params[3])
    return x.reshape(b, seq, d)


def kernel(x_prompt, x_sample, l0_ffn1_pre_g, l0_ffn1_w_gate, l0_ffn1_w_up, l0_ffn1_w_down, l0_ffn1_post_g, l0_mix_pre_g, l0_w_in, l0_a_q_norm_g, l0_a_k_norm_g, l0_b_cq_norm_g, l0_b_w_uq, l0_b_ckv_norm_g, l0_b_w_ukv, l0_w_out, l0_mix_post_g, l0_ffn2_pre_g, l0_ffn2_w_gate, l0_ffn2_w_up, l0_ffn2_w_down, l0_ffn2_post_g, l1_ffn1_pre_g, l1_ffn1_w_gate, l1_ffn1_w_up, l1_ffn1_w_down, l1_ffn1_post_g, l1_mix_pre_g, l1_pool_w, l1_pool_scale, l1_mix_post_g, l1_ffn2_pre_g, l1_ffn2_w_gate, l1_ffn2_w_up, l1_ffn2_w_down, l1_ffn2_post_g):
    vec = lambda g: g[None, :]
    ffn = lambda pre, wg, wu, wd, post: (vec(pre), wg.astype(BF16), wu.astype(BF16), wd.astype(BF16), vec(post))
    ffn_params = (
        ffn(l0_ffn1_pre_g, l0_ffn1_w_gate, l0_ffn1_w_up, l0_ffn1_w_down, l0_ffn1_post_g),
        ffn(l0_ffn2_pre_g, l0_ffn2_w_gate, l0_ffn2_w_up, l0_ffn2_w_down, l0_ffn2_post_g),
        ffn(l1_ffn1_pre_g, l1_ffn1_w_gate, l1_ffn1_w_up, l1_ffn1_w_down, l1_ffn1_post_g),
        ffn(l1_ffn2_pre_g, l1_ffn2_w_gate, l1_ffn2_w_up, l1_ffn2_w_down, l1_ffn2_post_g),
    )
    w_in, gq, gk, w_uq, w_ukv, wa, wb = _prep_mixer_weights(
        l0_w_in, l0_a_q_norm_g, l0_a_k_norm_g, l0_b_w_uq, l0_b_w_ukv, l0_w_out)
    mix0 = (vec(l0_mix_pre_g), w_in, gq, gk, vec(l0_b_cq_norm_g), w_uq, vec(l0_b_ckv_norm_g), w_ukv,
            wa, wb, vec(l0_mix_post_g))
    mix1 = (vec(l1_mix_pre_g), l1_pool_w.astype(BF16), vec(l1_pool_scale), vec(l1_mix_post_g))

    outs = []
    for x in (x_prompt, x_sample):
        tabs = _rope_tables(x.shape[1])
        outs.append(_trunk(x, ffn_params, mix0, mix1, tabs))
    return tuple(outs)
```

```python
import functools
import math

import jax
import jax.numpy as jnp
import numpy as np
from jax import lax
from jax.experimental import pallas as pl
from jax.experimental.pallas import tpu as pltpu

F32 = jnp.float32
BF16 = jnp.bfloat16

NORM_EPS = 1e-6
ROPE_BASE = 10000.0
GRID_W = 64
A_HEADS = 8
A_KV_HEADS = 2
A_HEAD_DIM = 128
B_HEADS = 8
B_Q_LORA = 512
B_KV_LORA = 256
B_NOPE_DIM = 128
B_ROPE_DIM = 64
B_V_DIM = 128
POOL_WINDOWS = (2, 4, 8, 16)
LANES = 128
LOG2_E = math.log2(math.e)
ATTN_CHUNK_UNROLL = 5
B_QK_PAD = 2 * LANES
ROW_CHUNK = 32
PROJ_SUB_ROWS = 512
FFN_SUB_ROWS = 512
POOL_HALO = 8

VMEM_LIMIT_BYTES = 60 * 1024 * 1024


def _cparams(*semantics):
    return pltpu.CompilerParams(dimension_semantics=semantics, vmem_limit_bytes=VMEM_LIMIT_BYTES)


def _rms(x, g):
    return x * lax.rsqrt(jnp.mean(x * x, axis=-1, keepdims=True) + NORM_EPS) * g


def _silu(x):
    return x * (1.0 / (1.0 + jnp.exp(-x)))


def _ffn_step(x_ref, pre_g_ref, wg_ref, wu_ref, wd_ref, post_g_ref, o_ref, h_ref, *, first, last):
    tm = x_ref.shape[0]
    sub = FFN_SUB_ROWS if (first or last) else tm
    for r in range(tm // sub):
        base = r * sub
        rows = slice(base, base + sub)
        if first:
            for c in range(sub // ROW_CHUNK):
                chunk = slice(base + c * ROW_CHUNK, base + (c + 1) * ROW_CHUNK)
                h_ref[chunk, :] = _rms(x_ref[chunk, :], pre_g_ref[...]).astype(BF16)
        h = h_ref[rows, :]
        gate = jnp.dot(h, wg_ref[...], preferred_element_type=F32)
        up = jnp.dot(h, wu_ref[...], preferred_element_type=F32)
        act = (_silu(gate) * up).astype(BF16)
        down = jnp.dot(act, wd_ref[...], preferred_element_type=F32)
        if first:
            o_ref[rows, :] = down
        else:
            o_ref[rows, :] += down
        if last:
            for c in range(sub // ROW_CHUNK):
                chunk = slice(base + c * ROW_CHUNK, base + (c + 1) * ROW_CHUNK)
                o_ref[chunk, :] = x_ref[chunk, :] + 0.5 * _rms(o_ref[chunk, :], post_g_ref[...])


def _ffn_kernel(*refs):
    f = pl.program_id(1)
    nf = pl.num_programs(1)
    pl.when(f == 0)(functools.partial(_ffn_step, *refs, first=True, last=False))
    pl.when(jnp.logical_and(f > 0, f < nf - 1))(functools.partial(_ffn_step, *refs, first=False, last=False))
    pl.when(f == nf - 1)(functools.partial(_ffn_step, *refs, first=False, last=True))


def _ffn_half(x, pre_g, wg, wu, wd, post_g, *, tm=1024, tf=512):
    t, d = x.shape
    dff = wg.shape[1]
    assert t % tm == 0 and dff % tf == 0 and dff // tf >= 2 and tm % FFN_SUB_ROWS == 0
    row = lambda i, f: (i, 0)
    const = lambda i, f: (0, 0)
    return pl.pallas_call(
        _ffn_kernel,
        grid=(t // tm, dff // tf),
        in_specs=[
            pl.BlockSpec((tm, d), row),
            pl.BlockSpec((1, d), const),
            pl.BlockSpec((d, tf), lambda i, f: (0, f)),
            pl.BlockSpec((d, tf), lambda i, f: (0, f)),
            pl.BlockSpec((tf, d), lambda i, f: (f, 0)),
            pl.BlockSpec((1, d), const),
        ],
        out_specs=pl.BlockSpec((tm, d), row),
        out_shape=jax.ShapeDtypeStruct((t, d), F32),
        scratch_shapes=[pltpu.VMEM((tm, d), BF16)],
        compiler_params=_cparams("parallel", "arbitrary"),
        name="ffn_half",
    )(x, pre_g, wg, wu, wd, post_g)


def _rope(x, cos, sin):
    return x * cos + pltpu.roll(x, LANES // 2, 1) * sin


def _inproj_kernel(x_ref, pre_g_ref, w_in_ref, gq_ref, gk_ref, gcq_ref, w_uq_ref, gckv_ref, w_ukv_ref,
                   cos_a_ref, sin_a_ref, cos_b_ref, sin_b_ref, *out_refs):
    for r in range(x_ref.shape[0] // PROJ_SUB_ROWS):
        rows = pl.ds(r * PROJ_SUB_ROWS, PROJ_SUB_ROWS)
        at = lambda ref: ref.at[rows]
        _inproj_rows(at(x_ref), pre_g_ref, w_in_ref, gq_ref, gk_ref, gcq_ref, w_uq_ref, gckv_ref, w_ukv_ref,
                     at(cos_a_ref), at(sin_a_ref), at(cos_b_ref), at(sin_b_ref), *map(at, out_refs))


def _inproj_rows(x_ref, pre_g_ref, w_in_ref, gq_ref, gk_ref, gcq_ref, w_uq_ref, gckv_ref, w_ukv_ref,
                 cos_a_ref, sin_a_ref, cos_b_ref, sin_b_ref,
                 qa_ref, ka_ref, va_ref, qb_ref, kb_ref, vb_ref):
    hn = _rms(x_ref[...], pre_g_ref[...]).astype(BF16)
    y = jnp.dot(hn, w_in_ref[...], preferred_element_type=F32)
    cos_a, sin_a = cos_a_ref[...], sin_a_ref[...]
    cos_b, sin_b = cos_b_ref[...], sin_b_ref[...]
    a_scale = A_HEAD_DIM ** -0.5 * LOG2_E
    b_scale = (B_NOPE_DIM + B_ROPE_DIM) ** -0.5 * LOG2_E
    ones = jnp.ones((x_ref.shape[0], LANES), BF16)
    q_cols = A_HEADS * A_HEAD_DIM
    kv_cols = A_KV_HEADS * A_HEAD_DIM

    for h in range(A_HEADS):
        sl = slice(h * LANES, (h + 1) * LANES)
        qa_ref[:, sl] = (_rope(_rms(y[:, sl], gq_ref[...]), cos_a, sin_a) * a_scale).astype(BF16)
    for h in range(A_KV_HEADS):
        sl = slice(h * LANES, (h + 1) * LANES)
        ka_ref[:, sl] = _rope(_rms(y[:, q_cols + h * LANES:q_cols + (h + 1) * LANES], gk_ref[...]),
                              cos_a, sin_a).astype(BF16)
    off = q_cols + kv_cols
    for h in range(A_KV_HEADS):
        va_ref[:, 2 * h * LANES:(2 * h + 1) * LANES] = y[:, off + h * LANES:off + (h + 1) * LANES].astype(BF16)
        va_ref[:, (2 * h + 1) * LANES:(2 * h + 2) * LANES] = ones
    off += kv_cols

    cq = _rms(y[:, off:off + B_Q_LORA], gcq_ref[...]).astype(BF16)
    off += B_Q_LORA
    ckv = _rms(y[:, off:off + B_KV_LORA], gckv_ref[...]).astype(BF16)
    off += B_KV_LORA
    k_rope = _rope(y[:, off:off + LANES], cos_b, sin_b).astype(BF16)

    qf = jnp.dot(cq, w_uq_ref[...], preferred_element_type=F32)
    kv = jnp.dot(ckv, w_ukv_ref[...], preferred_element_type=F32)
    for h in range(B_HEADS):
        base = h * B_QK_PAD
        qb_ref[:, base:base + LANES] = (qf[:, base:base + LANES] * b_scale).astype(BF16)
        qb_ref[:, base + LANES:base + B_QK_PAD] = (
            _rope(qf[:, base + LANES:base + B_QK_PAD], cos_b, sin_b) * b_scale).astype(BF16)
        kb_ref[:, base:base + LANES] = kv[:, h * LANES:(h + 1) * LANES].astype(BF16)
        kb_ref[:, base + LANES:base + B_QK_PAD] = k_rope
        v_col = B_HEADS * B_NOPE_DIM + h * LANES
        vb_ref[:, 2 * h * LANES:(2 * h + 1) * LANES] = kv[:, v_col:v_col + LANES].astype(BF16)
        vb_ref[:, (2 * h + 1) * LANES:(2 * h + 2) * LANES] = ones


def _mixer_inproj(x, seq, pre_g, w_in, gq, gk, gcq, w_uq, gckv, w_ukv, tabs, *, tm=512):
    t, d = x.shape
    assert seq % tm == 0 and tm % PROJ_SUB_ROWS == 0
    nblk = seq // tm
    row = lambda i: (i, 0)
    const = lambda i: (0, 0)
    pos = lambda i: (i % nblk, 0)
    full = lambda a: pl.BlockSpec(a.shape, const, pipeline_mode=pl.Buffered(1))
    widths = (A_HEADS * A_HEAD_DIM, A_KV_HEADS * A_HEAD_DIM, 2 * A_KV_HEADS * A_HEAD_DIM,
              B_HEADS * B_QK_PAD, B_HEADS * B_QK_PAD, 2 * B_HEADS * B_V_DIM)
    return pl.pallas_call(
        _inproj_kernel,
        grid=(t // tm,),
        in_specs=[pl.BlockSpec((tm, d), row), full(pre_g), full(w_in), full(gq), full(gk), full(gcq),
                  full(w_uq), full(gckv), full(w_ukv)] + [pl.BlockSpec((tm, LANES), pos)] * 4,
        out_specs=[pl.BlockSpec((tm, w), row) for w in widths],
        out_shape=[jax.ShapeDtypeStruct((t, w), BF16) for w in widths],
        compiler_params=_cparams("parallel"),
        name="mixer_inproj",
    )(x, pre_g, w_in, gq, gk, gcq, w_uq, gckv, w_ukv, *tabs)


def _attn_kernel(q_ref, k_ref, v_ref, o_ref, *scratch, group, dk, dv, tk, rb):
    tq = q_ref.shape[0]
    seq = k_ref.shape[0]
    nblocks = group * tq // rb
    if group > 1:
        (qs_ref,) = scratch
        for j in range(group):
            qs_ref[j * tq:(j + 1) * tq, :] = q_ref[:, j * dk:(j + 1) * dk]
    else:
        qs_ref = q_ref

    def scores(i, c, m_old):
        start = c * tk if isinstance(c, int) else pl.multiple_of(c * tk, tk)
        q = qs_ref[i * rb:(i + 1) * rb, :]
        s = lax.dot_general(q, k_ref[pl.ds(start, tk), :], (((1,), (1,)), ((), ())),
                            preferred_element_type=F32)
        m_new = jnp.maximum(m_old, jnp.max(s, axis=-1, keepdims=True))
        return m_new, jnp.exp2(m_old - m_new), jnp.exp2(s - m_new).astype(BF16)

    def accumulate(c, acc, alpha, p):
        start = c * tk if isinstance(c, int) else pl.multiple_of(c * tk, tk)
        return alpha * acc + jnp.dot(p, v_ref[pl.ds(start, tk), :], preferred_element_type=F32)

    def body(c, carry):
        ms, accs, alphas, ps = carry
        new = [], [], [], []
        for i in range(nblocks):
            acc = accumulate(c - 1, accs[i], alphas[i], ps[i])
            m_new, alpha, p = scores(i, c, ms[i])
            for lst, val in zip(new, (m_new, acc, alpha, p)):
                lst.append(val)
        return tuple(map(tuple, new))

    first = [scores(i, 0, jnp.full((rb, 1), -jnp.inf, F32)) for i in range(nblocks)]
    init = (tuple(f[0] for f in first),
            tuple(jnp.zeros((rb, 2 * dv), F32) for _ in range(nblocks)),
            tuple(f[1] for f in first), tuple(f[2] for f in first))
    nchunks = seq // tk
    ms, accs, alphas, ps = lax.fori_loop(1, nchunks, body, init, unroll=ATTN_CHUNK_UNROLL)
    accs = [accumulate(nchunks - 1, accs[i], alphas[i], ps[i]) for i in range(nblocks)]
    piece = min(rb, tq)
    for i in range(nblocks):
        out = (accs[i][:, :dv] * (1.0 / accs[i][:, dv:])).astype(o_ref.dtype)
        for n in range(rb // piece):
            j, r = divmod(i * rb + n * piece, tq)
            o_ref[r:r + piece, j * dv:(j + 1) * dv] = out[n * piece:(n + 1) * piece]


def _attention(q, k, v, *, kv_heads, group, dk, dv, tq, tk=512, rb=1024):
    b, seq, _ = q.shape
    tq, tk = min(tq, seq), min(tk, seq)
    rb = min(rb, group * tq)
    assert seq % tq == 0 and seq % tk == 0 and (group * tq) % rb == 0 and (rb % tq == 0 or tq % rb == 0)
    kern = functools.partial(_attn_kernel, group=group, dk=dk, dv=dv, tk=tk, rb=rb)
    return pl.pallas_call(
        kern,
        grid=(b, kv_heads, seq // tq),
        in_specs=[
            pl.BlockSpec((None, tq, group * dk), lambda bi, h, i: (bi, i, h)),
            pl.BlockSpec((None, seq, dk), lambda bi, h, i: (bi, 0, h)),
            pl.BlockSpec((None, seq, 2 * dv), lambda bi, h, i: (bi, 0, h)),
        ],
        out_specs=pl.BlockSpec((None, tq, group * dv), lambda bi, h, i: (bi, i, h)),
        out_shape=jax.ShapeDtypeStruct((b, seq, kv_heads * group * dv), BF16),
        scratch_shapes=[pltpu.VMEM((group * tq, dk), BF16)] if group > 1 else [],
        compiler_params=_cparams("parallel", "parallel", "arbitrary"),
        name=f"attention_g{group}_dk{dk}",
    )(q, k, v)


def _outproj_kernel(x_ref, oa_ref, ob_ref, wa_ref, wb_ref, post_g_ref, o_ref):
    for r in range(x_ref.shape[0] // PROJ_SUB_ROWS):
        rows = slice(r * PROJ_SUB_ROWS, (r + 1) * PROJ_SUB_ROWS)
        m = jnp.dot(oa_ref[rows, :], wa_ref[...], preferred_element_type=F32)
        m += jnp.dot(ob_ref[rows, :], wb_ref[...], preferred_element_type=F32)
        o_ref[rows, :] = x_ref[rows, :] + _rms(m, post_g_ref[...])


def _mixer_outproj(x, oa, ob, wa, wb, post_g, *, tm=512):
    t, d = x.shape
    row = lambda i: (i, 0)
    const = lambda i: (0, 0)
    return pl.pallas_call(
        _outproj_kernel,
        grid=(t // tm,),
        in_specs=[pl.BlockSpec((tm, d), row), pl.BlockSpec((tm, oa.shape[1]), row),
                  pl.BlockSpec((tm, ob.shape[1]), row),
                  pl.BlockSpec(wa.shape, const, pipeline_mode=pl.Buffered(1)),
                  pl.BlockSpec(wb.shape, const, pipeline_mode=pl.Buffered(1)), pl.BlockSpec((1, d), const)],
        out_specs=pl.BlockSpec((tm, d), row),
        out_shape=jax.ShapeDtypeStruct((t, d), F32),
        compiler_params=_cparams("parallel"),
        name="mixer_outproj",
    )(x, oa, ob, wa, wb, post_g)


def _pool_kernel(x_ref, prev_ref, next_ref, pre_g_ref, w_ref, scale_ref, post_g_ref, o_ref, ext_ref, m_ref,
                 *, seq):
    tm, d = x_ref.shape
    group = d // len(POOL_WINDOWS)
    nblk = seq // tm
    blk = pl.program_id(0) % nblk
    g = pre_g_ref[...]
    x = x_ref[...]
    ext_ref[0:POOL_HALO, :] = jnp.where(blk > 0, _rms(prev_ref[...], g), 0.0)
    ext_ref[POOL_HALO:POOL_HALO + tm, :] = _rms(x, g)
    ext_ref[POOL_HALO + tm:, :] = jnp.where(blk < nblk - 1, _rms(next_ref[...], g), 0.0)

    t = blk * tm + lax.broadcasted_iota(jnp.int32, (tm, 1), 0)
    n_ext = tm + 2 * POOL_HALO
    for gi, w in enumerate(POOL_WINDOWS):
        cols = slice(gi * group, (gi + 1) * group)
        half = w // 2
        run = ext_ref[:, cols]
        span = 1
        while span < w:
            run = run + pltpu.roll(run, span, 0)
            span *= 2
        lead = half - 1
        if lead:
            run = pltpu.roll(run, n_ext - lead, 0)
        tot = run[POOL_HALO:POOL_HALO + tm]
        cnt = (jnp.minimum(t + half, seq) - jnp.maximum(t - half, 0)).astype(F32)
        pooled = tot * (1.0 / cnt) - ext_ref[POOL_HALO:POOL_HALO + tm, cols]
        m_ref[:, cols] = jnp.dot(pooled.astype(BF16), w_ref[gi], preferred_element_type=F32)
    o_ref[...] = x + _rms(m_ref[...] * scale_ref[...], post_g_ref[...])


def _pool_mixer(x, seq, pre_g, pool_w, pool_scale, post_g, *, tm=512):
    t, d = x.shape
    assert seq % tm == 0 and max(POOL_WINDOWS) // 2 <= POOL_HALO
    hb = tm // POOL_HALO
    last = t // POOL_HALO - 1
    row = lambda i: (i, 0)
    const = lambda i: (0, 0)
    return pl.pallas_call(
        functools.partial(_pool_kernel, seq=seq),
        grid=(t // tm,),
        in_specs=[
            pl.BlockSpec((tm, d), row),
            pl.BlockSpec((POOL_HALO, d), lambda i: (jnp.maximum(i * hb - 1, 0), 0)),
            pl.BlockSpec((POOL_HALO, d), lambda i: (jnp.minimum((i + 1) * hb, last), 0)),
            pl.BlockSpec((1, d), const),
            pl.BlockSpec(pool_w.shape, lambda i: (0, 0, 0)),
            pl.BlockSpec((1, d), const),
            pl.BlockSpec((1, d), const),
        ],
        out_specs=pl.BlockSpec((tm, d), row),
        out_shape=jax.ShapeDtypeStruct((t, d), F32),
        scratch_shapes=[pltpu.VMEM((tm + 2 * POOL_HALO, d), F32), pltpu.VMEM((tm, d), F32)],
        compiler_params=_cparams("parallel"),
        name="pool_mixer",
    )(x, x, x, pre_g, pool_w, pool_scale, post_g)


def _rope_tables(seq):
    rows = seq // GRID_W
    row = jnp.repeat(jnp.arange(rows, dtype=F32), GRID_W)
    col = jnp.tile(jnp.arange(GRID_W, dtype=F32), rows)

    def angles(rot_dim):
        half = rot_dim // 2
        freqs = ROPE_BASE ** (-jnp.arange(0, half, 2, dtype=F32) / half)
        return jnp.concatenate([row[:, None] * freqs, col[:, None] * freqs], axis=-1)

    ang_a = angles(A_HEAD_DIM)
    cos_a = jnp.concatenate([jnp.cos(ang_a)] * 2, axis=-1)
    sin_a = jnp.concatenate([-jnp.sin(ang_a), jnp.sin(ang_a)], axis=-1)
    ang_b = angles(B_ROPE_DIM)
    z = jnp.zeros_like(ang_b)
    cos_b = jnp.concatenate([jnp.cos(ang_b), z, jnp.cos(ang_b), z], axis=-1)
    sin_b = jnp.concatenate([-jnp.sin(ang_b), z, jnp.sin(ang_b), z], axis=-1)
    return cos_a, sin_a, cos_b, sin_b


def _pair_split_perm(n):
    return np.concatenate([np.arange(0, n, 2), np.arange(1, n, 2)])


def _spread_rope_cols(w):
    half = B_ROPE_DIM // 2
    z = jnp.zeros((w.shape[0], half), w.dtype)
    return jnp.concatenate([w[:, 0::2], z, w[:, 1::2], z], axis=-1)


def _prep_mixer_weights(w_in, gq, gk, w_uq, w_ukv, w_out):
    d = w_in.shape[0]
    w_in, w_uq, w_ukv = w_in.astype(BF16), w_uq.astype(BF16), w_ukv.astype(BF16)
    perm = _pair_split_perm(A_HEAD_DIM)
    nq, nkv = A_HEADS * A_HEAD_DIM, A_KV_HEADS * A_HEAD_DIM
    wq = w_in[:, :nq].reshape(d, A_HEADS, A_HEAD_DIM)[:, :, perm].reshape(d, nq)
    wk = w_in[:, nq:nq + nkv].reshape(d, A_KV_HEADS, A_HEAD_DIM)[:, :, perm].reshape(d, nkv)
    rest = w_in[:, nq + nkv:-B_ROPE_DIM]
    w_in_p = jnp.concatenate([wq, wk, rest, _spread_rope_cols(w_in[:, -B_ROPE_DIM:])], axis=-1)

    uq = w_uq.reshape(B_Q_LORA, B_HEADS, B_NOPE_DIM + B_ROPE_DIM)
    uq_rope = _spread_rope_cols(uq[:, :, B_NOPE_DIM:].reshape(B_Q_LORA * B_HEADS, B_ROPE_DIM))
    uq_p = jnp.concatenate([uq[:, :, :B_NOPE_DIM], uq_rope.reshape(B_Q_LORA, B_HEADS, LANES)], axis=-1)
    uq_p = uq_p.reshape(B_Q_LORA, B_HEADS * B_QK_PAD)

    ukv = w_ukv.reshape(B_KV_LORA, B_HEADS, B_NOPE_DIM + B_V_DIM)
    ukv_p = jnp.concatenate([ukv[:, :, :B_NOPE_DIM].reshape(B_KV_LORA, -1),
                             ukv[:, :, B_NOPE_DIM:].reshape(B_KV_LORA, -1)], axis=-1)
    na = A_HEADS * A_HEAD_DIM
    return (w_in_p, gq[perm][None, :], gk[perm][None, :], uq_p, ukv_p,
            w_out[:na].astype(BF16), w_out[na:].astype(BF16))


def _trunk(x, ffn_params, mix0, mix1, tabs):
    b, seq, d = x.shape
    x = x.reshape(b * seq, d)
    (pre0, w_in, gq, gk, gcq, w_uq, gckv, w_ukv, wa, wb, post0) = mix0
    (pre1, pool_w, pool_scale, post1) = mix1

    x = _ffn_half(x, *ffn_params[0])
    qa, ka, va, qb, kb, vb = _mixer_inproj(x, seq, pre0, w_in, gq, gk, gcq, w_uq, gckv, w_ukv, tabs)
    shp = lambda a: a.reshape(b, seq, a.shape[-1])
    oa = _attention(shp(qa), shp(ka), shp(va), kv_heads=A_KV_HEADS, group=A_HEADS // A_KV_HEADS,
                    dk=A_HEAD_DIM, dv=A_HEAD_DIM, tq=512)
    ob = _attention(shp(qb), shp(kb), shp(vb), kv_heads=B_HEADS, group=1, dk=B_QK_PAD, dv=B_V_DIM, tq=2048)
    x = _mixer_outproj(x, oa.reshape(b * seq, -1), ob.reshape(b * seq, -1), wa, wb, post0)
    x = _ffn_half(x, *ffn_params[1])
    x = _ffn_half(x, *ffn_params[2])
    x = _pool_mixer(x, seq, pre1, pool_w, pool_scale, post1)
    x = _ffn_half(x, *ffn_params[3])
    return x.reshape(b, seq, d)


def kernel(x_prompt, x_sample, l0_ffn1_pre_g, l0_ffn1_w_gate, l0_ffn1_w_up, l0_ffn1_w_down, l0_ffn1_post_g, l0_mix_pre_g, l0_w_in, l0_a_q_norm_g, l0_a_k_norm_g, l0_b_cq_norm_g, l0_b_w_uq, l0_b_ckv_norm_g, l0_b_w_ukv, l0_w_out, l0_mix_post_g, l0_ffn2_pre_g, l0_ffn2_w_gate, l0_ffn2_w_up, l0_ffn2_w_down, l0_ffn2_post_g, l1_ffn1_pre_g, l1_ffn1_w_gate, l1_ffn1_w_up, l1_ffn1_w_down, l1_ffn1_post_g, l1_mix_pre_g, l1_pool_w, l1_pool_scale, l1_mix_post_g, l1_ffn2_pre_g, l1_ffn2_w_gate, l1_ffn2_w_up, l1_ffn2_w_down, l1_ffn2_post_g):
    vec = lambda g: g[None, :]
    ffn = lambda pre, wg, wu, wd, post: (vec(pre), wg.astype(BF16), wu.astype(BF16), wd.astype(BF16), vec(post))
    ffn_params = (
        ffn(l0_ffn1_pre_g, l0_ffn1_w_gate, l0_ffn1_w_up, l0_ffn1_w_down, l0_ffn1_post_g),
        ffn(l0_ffn2_pre_g, l0_ffn2_w_gate, l0_ffn2_w_up, l0_ffn2_w_down, l0_ffn2_post_g),
        ffn(l1_ffn1_pre_g, l1_ffn1_w_gate, l1_ffn1_w_up, l1_ffn1_w_down, l1_ffn1_post_g),
        ffn(l1_ffn2_pre_g, l1_ffn2_w_gate, l1_ffn2_w_up, l1_ffn2_w_down, l1_ffn2_post_g),
    )
    w_in, gq, gk, w_uq, w_ukv, wa, wb = _prep_mixer_weights(
        l0_w_in, l0_a_q_norm_g, l0_a_k_norm_g, l0_b_w_uq, l0_b_w_ukv, l0_w_out)
    mix0 = (vec(l0_mix_pre_g), w_in, gq, gk, vec(l0_b_cq_norm_g), w_uq, vec(l0_b_ckv_norm_g), w_ukv,
            wa, wb, vec(l0_mix_post_g))
    mix1 = (vec(l1_mix_pre_g), l1_pool_w.astype(BF16), vec(l1_pool_scale), vec(l1_mix_post_g))

    outs = []
    for x in (x_prompt, x_sample):
        tabs = _rope_tables(x.shape[1])
        outs.append(_trunk(x, ffn_params, mix0, mix1, tabs))
    return tuple(outs)
```
